```python
import math
import jax, jax.numpy as jnp
from jax import lax
import numpy as np

D_MODEL = 4096
BATCH = 2
SEQ = 4096
DEPTH = 4

GRID_W = 64
CTX_LEN = 256

MLA_V = 128
MLA_HEADS = (D_MODEL // 2) // MLA_V
MLA_NOPE = 128
MLA_ROPE = 64
Q_RANK = D_MODEL // 4
KV_RANK = D_MODEL // 8
D_MLA = MLA_HEADS * MLA_V
F_GROUPS = 4
F_DIM = D_MODEL // 16
D_FOURIER = F_GROUPS * F_DIM
C_HEADS = 4
C_DIM = D_MODEL // 16
D_CHUNK = C_HEADS * C_DIM
CHUNK = 128
D_MIX = D_MLA + D_FOURIER + D_CHUNK
D_IN = Q_RANK + KV_RANK + MLA_ROPE + D_FOURIER + 2 * D_CHUNK
SPLITS = (Q_RANK, Q_RANK + KV_RANK, Q_RANK + KV_RANK + MLA_ROPE,
          Q_RANK + KV_RANK + MLA_ROPE + D_FOURIER)
N_GROUPS = 4
EXP_PER_GROUP = 4
N_EXPERTS = N_GROUPS * EXP_PER_GROUP
TOP_K = 2
D_EXPERT = 3 * D_MODEL // 32

ROPE_THETA = 10000.0
Q_BLOCK = 128
ALPHA = (2.0 * DEPTH) ** 0.25
BETA = (8.0 * DEPTH) ** -0.25
EPS = 1e-6

kernel_name = 'hybrid_mla_fnet_gmlp_hmoe_prefix_dit'


def layer_norm(x, g, b):
    xf = x.astype(jnp.float32)
    mu = jnp.mean(xf, axis=-1, keepdims=True)
    var = jnp.mean(jnp.square(xf - mu), axis=-1, keepdims=True)
    return ((xf - mu) * lax.rsqrt(var + EPS) * g + b).astype(x.dtype)


def standardize(x):
    xf = x.astype(jnp.float32)
    mu = jnp.mean(xf, axis=-1, keepdims=True)
    var = jnp.mean(jnp.square(xf - mu), axis=-1, keepdims=True)
    return ((xf - mu) * lax.rsqrt(var + EPS)).astype(x.dtype)


def rms_norm(x, g):
    xf = x.astype(jnp.float32)
    return (xf * lax.rsqrt(jnp.mean(xf * xf, axis=-1, keepdims=True) + EPS) * g).astype(x.dtype)


def axial_rope_tables(row, col, dtype):
    half = MLA_ROPE // 2
    inv_freq = ROPE_THETA ** (-jnp.arange(0, half, 2, dtype=jnp.float32) / half)
    ang_r = row.astype(jnp.float32)[:, None] * inv_freq
    ang_c = col.astype(jnp.float32)[:, None] * inv_freq
    ang = jnp.concatenate([ang_r, ang_r, ang_c, ang_c], axis=-1)
    return jnp.cos(ang).astype(dtype), jnp.sin(ang).astype(dtype)


def rotate_half(a):
    a1, a2 = jnp.split(a, 2, axis=-1)
    return jnp.concatenate([-a2, a1], axis=-1)


def apply_rope(x, cos, sin):
    xr, xc = jnp.split(x, 2, axis=-1)
    rot = jnp.concatenate([rotate_half(xr), rotate_half(xc)], axis=-1)
    return x * cos[:, None, :] + rot * sin[:, None, :]


def adaln(silu_cond, w, b):
    return jnp.split(silu_cond @ w + b, 6, axis=-1)


def mla_queries(c_q, g, w_uq):
    b, n, _ = c_q.shape
    q = (rms_norm(c_q, g) @ w_uq).reshape(b, n, MLA_HEADS, MLA_NOPE + MLA_ROPE)
    return q[..., :MLA_NOPE], q[..., MLA_NOPE:]


def mla_keys_values(c_kv, k_rope, g, w_ukv):
    b, n, _ = c_kv.shape
    kv = (rms_norm(c_kv, g) @ w_ukv).reshape(b, n, MLA_HEADS, MLA_NOPE + MLA_V)
    return kv[..., :MLA_NOPE], k_rope[:, :, None, :], kv[..., MLA_NOPE:]


def join_heads(nope, rope):
    rope = jnp.broadcast_to(rope, nope.shape[:-1] + (rope.shape[-1],))
    return jnp.concatenate([nope, rope], axis=-1)


def attend(q, k, v):
    b, n, h, dq = q.shape
    nb = n // Q_BLOCK
    qb = jnp.moveaxis(q.reshape(b, nb, Q_BLOCK, h, dq), 1, 0)
    scale = 1.0 / math.sqrt(dq)

    def block(qi):
        s = jnp.einsum('bqhd,bkhd->bhqk', qi, k, preferred_element_type=jnp.float32) * scale
        p = jax.nn.softmax(s, axis=-1).astype(v.dtype)
        return jnp.einsum('bhqk,bkhd->bqhd', p, v)

    o = lax.map(block, qb)
    return jnp.moveaxis(o, 0, 1).reshape(b, n, h * MLA_V)


def fourier_mix(f_in, w_f):
    b, n, _ = f_in.shape
    f = f_in.reshape(b, n, F_GROUPS, F_DIM).astype(jnp.float32)
    fr = jnp.fft.fft2(f, axes=(1, 3), norm='ortho').real.astype(f_in.dtype)
    return jnp.einsum('bngc,gcd->bngd', fr, w_f).reshape(b, n, D_FOURIER)


def chunk_mix(cm_in, w_s, b_s):
    b, n, _ = cm_in.shape
    u, v = jnp.split(jax.nn.gelu(cm_in), 2, axis=-1)
    shp = (b, n // CHUNK, CHUNK, C_HEADS, C_DIM)
    u = u.reshape(shp)
    v = standardize(v.reshape(shp))
    sv = jnp.einsum('hpq,bcqhd->bcphd', w_s, v) + b_s.T[:, :, None]
    return (u * sv).reshape(b, n, D_CHUNK)


def hier_moe(h, w_rg, b_rg, w_re, b_re, w_gate, w_up, w_down):
    lead = h.shape[:-1]
    g_logits = (h @ w_rg + b_rg).astype(jnp.float32)
    g_prob = jax.nn.softmax(g_logits, axis=-1)
    g_idx = jnp.argmax(g_logits, axis=-1)
    g_w = jnp.take_along_axis(g_prob, g_idx[..., None], axis=-1)
    e_logits = (h @ w_re + b_re).astype(jnp.float32).reshape(lead + (N_GROUPS, EXP_PER_GROUP))
    e_logits = jnp.take_along_axis(e_logits, g_idx[..., None, None], axis=-2)[..., 0, :]
    top_v, top_i = lax.top_k(e_logits, TOP_K)
    top_w = jax.nn.softmax(top_v, axis=-1) * g_w
    expert_id = g_idx[..., None] * EXP_PER_GROUP + top_i
    combine = jnp.sum(jax.nn.one_hot(expert_id, N_EXPERTS, dtype=jnp.float32) * top_w[..., None], axis=-2)
    hid = jax.nn.silu(jnp.einsum('bnd,edf->bnef', h, w_gate)) * jnp.einsum('bnd,edf->bnef', h, w_up)
    hid = hid * combine.astype(h.dtype)[..., None]
    return jnp.einsum('bnef,efd->bnd', hid, w_down)


def setup_inputs(seed: int = 0) -> dict:
    key = jax.random.key(seed)
    ks = jax.random.split(key, 26)
    L, D = DEPTH, D_MODEL

    def nrm(k, shape, fan_in, scale=1.0):
        return jax.random.normal(k, shape, jnp.float32) * (scale * fan_in ** -0.5)

    def gain(k, shape):
        return 1.0 + 0.05 * jax.random.normal(k, shape, jnp.float32)

    def small(k, shape, s=0.02):
        return s * jax.random.normal(k, shape, jnp.float32)

    return {
        'x': jax.random.normal(ks[0], (BATCH, SEQ, D), jnp.float32),
        'c': jax.random.normal(ks[1], (BATCH, D), jnp.float32),
        'ctx': jax.random.normal(ks[2], (BATCH, CTX_LEN, D), jnp.float32),
        'c_ctx': jax.random.normal(ks[3], (D,), jnp.float32),
        'w_ada': nrm(ks[4], (L, D, 6 * D), D, 0.5),
        'b_ada': small(ks[5], (L, 6 * D)),
        'w_in': nrm(ks[6], (L, D, D_IN), D),
        'q_norm_g': gain(ks[7], (L, Q_RANK)),
        'kv_norm_g': gain(ks[8], (L, KV_RANK)),
        'w_uq': nrm(ks[9], (L, Q_RANK, MLA_HEADS * (MLA_NOPE + MLA_ROPE)), Q_RANK),
        'w_ukv': nrm(ks[10], (L, KV_RANK, MLA_HEADS * (MLA_NOPE + MLA_V)), KV_RANK),
        'w_fourier': nrm(ks[11], (L, F_GROUPS, F_DIM, F_DIM), F_DIM),
        'w_spatial': nrm(ks[12], (L, C_HEADS, CHUNK, CHUNK), CHUNK),
        'b_spatial': gain(ks[13], (L, C_HEADS, CHUNK)),
        'w_out': nrm(ks[14], (L, D_MIX, D), D_MIX, BETA),
        'ln1_g': gain(ks[15], (L, D)),
        'ln1_b': small(ks[16], (L, D)),
        'w_router_group': nrm(ks[17], (L, D, N_GROUPS), D),
        'b_router_group': small(ks[18], (L, N_GROUPS), 0.01),
        'w_router_expert': nrm(ks[19], (L, D, N_EXPERTS), D),
        'b_router_expert': small(ks[20], (L, N_EXPERTS), 0.01),
        'w_gate': nrm(ks[21], (L, N_EXPERTS, D, D_EXPERT), D),
        'w_up': nrm(ks[22], (L, N_EXPERTS, D, D_EXPERT), D),
        'w_down': nrm(ks[23], (L, N_EXPERTS, D_EXPERT, D), D_EXPERT, BETA),
        'ln2_g': gain(ks[24], (L, D)),
        'ln2_b': small(ks[25], (L, D)),
    }


def reference(x, c, ctx, c_ctx, w_ada, b_ada, w_in, q_norm_g, kv_norm_g, w_uq, w_ukv,
              w_fourier, w_spatial, b_spatial, w_out, ln1_g, ln1_b,
              w_router_group, b_router_group, w_router_expert, b_router_expert,
              w_gate, w_up, w_down, ln2_g, ln2_b):
    n_lat = x.shape[1]
    rows_n = n_lat // GRID_W
    row = jnp.repeat(jnp.arange(rows_n), GRID_W)
    col = jnp.tile(jnp.arange(GRID_W), rows_n)
    cos, sin = axial_rope_tables(row, col, x.dtype)
    silu_c = jax.nn.silu(c)[:, None, :]
    silu_cc = jax.nn.silu(c_ctx)

    xl, xc = x, ctx
    for l in range(DEPTH):
        last = l == DEPTH - 1
        sh_m, sc_m, g_m, sh_f, sc_f, g_f = adaln(silu_c, w_ada[l], b_ada[l])
        csh_m, csc_m, cg_m, csh_f, csc_f, cg_f = adaln(silu_cc, w_ada[l], b_ada[l])

        hl = xl * (1.0 + sc_m) + sh_m
        hc = xc * (1.0 + csc_m) + csh_m
        cq_l, ckv_l, kr_l, f_l, cm_l = jnp.split(hl @ w_in[l], SPLITS, axis=-1)
        cq_c, ckv_c, kr_c, f_c, cm_c = jnp.split(hc @ w_in[l], SPLITS, axis=-1)

        kn_c, kr_c, v_c = mla_keys_values(ckv_c, kr_c, kv_norm_g[l], w_ukv[l])
        k_c = join_heads(kn_c, kr_c)
        qn_l, qr_l = mla_queries(cq_l, q_norm_g[l], w_uq[l])
        kn_l, kr_l, v_l = mla_keys_values(ckv_l, kr_l, kv_norm_g[l], w_ukv[l])
        q_l = join_heads(qn_l, apply_rope(qr_l, cos, sin))
        k_l = join_heads(kn_l, apply_rope(kr_l, cos, sin))
        att_l = attend(q_l, jnp.concatenate([k_c, k_l], axis=1), jnp.concatenate([v_c, v_l], axis=1))
        mix_l = jnp.concatenate([att_l, fourier_mix(f_l, w_fourier[l]),
                                 chunk_mix(cm_l, w_spatial[l], b_spatial[l])], axis=-1)
        xl = layer_norm(ALPHA * xl + g_m * (mix_l @ w_out[l]), ln1_g[l], ln1_b[l])

        if not last:
            qn_c, qr_c = mla_queries(cq_c, q_norm_g[l], w_uq[l])
            att_c = attend(join_heads(qn_c, qr_c), k_c, v_c)
            mix_c = jnp.concatenate([att_c, fourier_mix(f_c, w_fourier[l]),
                                     chunk_mix(cm_c, w_spatial[l], b_spatial[l])], axis=-1)
            xc = layer_norm(ALPHA * xc + cg_m * (mix_c @ w_out[l]), ln1_g[l], ln1_b[l])

        hl2 = xl * (1.0 + sc_f) + sh_f
        ff_l = hier_moe(hl2, w_router_group[l], b_router_group[l], w_router_expert[l],
                        b_router_expert[l], w_gate[l], w_up[l], w_down[l])
        xl = layer_norm(ALPHA * xl + g_f * ff_l, ln2_g[l], ln2_b[l])
        if not last:
            hc2 = xc * (1.0 + csc_f) + csh_f
            ff_c = hier_moe(hc2, w_router_group[l], b_router_group[l], w_router_expert[l],
                            b_router_expert[l], w_gate[l], w_up[l], w_down[l])
            xc = layer_norm(ALPHA * xc + cg_f * ff_c, ln2_g[l], ln2_b[l])
    return xl
```

```python
import functools
import math

import jax
import jax.numpy as jnp
from jax import lax
from jax.experimental import pallas as pl
from jax.experimental.pallas import tpu as pltpu

_BF = jnp.bfloat16
_F32 = jnp.float32

ROPE_THETA = 10000.0
EPS = 1e-6
GRID_W = 64
HEAD_V = 128
HEAD_ROPE = 64
HEAD_QK = 256
F_GROUPS = 4
C_HEADS = 4
CHUNK = 128
N_GROUPS = 4
EXP_PER_GROUP = 4
LANES = 128
MOD_ROWS = 8

_MIB = 1024 * 1024


def _params(sem, vmem_mib):
    return pltpu.CompilerParams(dimension_semantics=sem, vmem_limit_bytes=vmem_mib * _MIB)


def _tile(n, prefs):
    for t in prefs:
        if n % t == 0:
            return t
    raise ValueError(f"no tile in {prefs} divides {n}")


def _adaln_kernel(c_ref, w_ref, b_ref, o_ref):
    s = jax.nn.silu(c_ref[...]).astype(_BF)
    o_ref[0] = jnp.dot(s, w_ref[0].astype(_BF), preferred_element_type=_F32) + b_ref[0]


def _adaln(c_rows, w_ada, b_ada):
    L, D, N = w_ada.shape
    tn = 512
    return pl.pallas_call(
        _adaln_kernel,
        grid=(L, N // tn),
        in_specs=[pl.BlockSpec((MOD_ROWS, D), lambda l, j: (0, 0)),
                  pl.BlockSpec((1, D, tn), lambda l, j: (l, 0, j)),
                  pl.BlockSpec((1, 1, tn), lambda l, j: (l, 0, j))],
        out_specs=pl.BlockSpec((1, MOD_ROWS, tn), lambda l, j: (l, 0, j)),
        out_shape=jax.ShapeDtypeStruct((L, MOD_ROWS, N), _F32),
        compiler_params=_params(("parallel", "parallel"), 40),
        name="adaln",
    )(c_rows, w_ada, b_ada.reshape(L, 1, N))


def _mod_spec(chunk, D, tiles_per_seq, n_batch):
    return pl.BlockSpec((1, 1, D), lambda i, *_: (jnp.minimum(i // tiles_per_seq, n_batch) * 6 + chunk, 0, 0))


def _modulate_kernel(x_ref, sc_ref, sh_ref, o_ref):
    o_ref[...] = (x_ref[...] * (1.0 + sc_ref[0]) + sh_ref[0]).astype(_BF)


def _modulate(x, mod, sc_chunk, sh_chunk, seq, n_batch):
    T, D = x.shape
    tm = _tile(seq, (256,))
    tps = seq // tm
    return pl.pallas_call(
        _modulate_kernel,
        grid=(T // tm,),
        in_specs=[pl.BlockSpec((tm, D), lambda i: (i, 0)),
                  _mod_spec(sc_chunk, D, tps, n_batch), _mod_spec(sh_chunk, D, tps, n_batch)],
        out_specs=pl.BlockSpec((tm, D), lambda i: (i, 0)),
        out_shape=jax.ShapeDtypeStruct((T, D), _BF),
        compiler_params=_params(("parallel",), 32),
        name="modulate",
    )(x, mod, mod)


def _ln_tail(y, g_ref, b_ref, sc_ref, sh_ref, xo_ref, ho_ref):
    mu = jnp.mean(y, axis=-1, keepdims=True)
    d = y - mu
    var = jnp.mean(d * d, axis=-1, keepdims=True)
    xn = d * lax.rsqrt(var + EPS) * g_ref[...] + b_ref[...]
    xo_ref[...] = xn
    if ho_ref is not None:
        ho_ref[...] = (xn * (1.0 + sc_ref[0]) + sh_ref[0]).astype(_BF)


def _resid_ln_kernel(x_ref, a_ref, gate_ref, g_ref, b_ref, sc_ref, sh_ref, xo_ref, ho_ref, *, alpha):
    y = alpha * x_ref[...] + gate_ref[0] * a_ref[...]
    _ln_tail(y, g_ref, b_ref, sc_ref, sh_ref, xo_ref, ho_ref)


def _resid_ln(x, a, mod, gate_chunk, ln_g, ln_b, sc_chunk, sh_chunk, seq, n_batch, alpha):
    T, D = x.shape
    tm = _tile(seq, (256,))
    tps = seq // tm
    row = pl.BlockSpec((tm, D), lambda i: (i, 0))
    vec = pl.BlockSpec((1, D), lambda i: (0, 0))
    return pl.pallas_call(
        functools.partial(_resid_ln_kernel, alpha=alpha),
        grid=(T // tm,),
        in_specs=[row, row, _mod_spec(gate_chunk, D, tps, n_batch), vec, vec,
                  _mod_spec(sc_chunk, D, tps, n_batch), _mod_spec(sh_chunk, D, tps, n_batch)],
        out_specs=[row, row],
        out_shape=[jax.ShapeDtypeStruct((T, D), _F32), jax.ShapeDtypeStruct((T, D), _BF)],
        compiler_params=_params(("parallel",), 40),
        name="resid_ln",
    )(x, a, mod, ln_g.reshape(1, D), ln_b.reshape(1, D), mod, mod)


def _row_gather(src_hbm, idx_ref, base, n, dst_ref, sem):
    def _copy(r):
        return pltpu.make_async_copy(src_hbm.at[pl.ds(idx_ref[base + r], 1)], dst_ref.at[pl.ds(r, 1)], sem)

    def _start(r, carry):
        _copy(r).start()
        return carry

    def _wait(r, carry):
        _copy(r).wait()
        return carry

    lax.fori_loop(0, n, _start, 0)
    lax.fori_loop(0, n, _wait, 0)


def _resid_ln_gather_kernel(idx_ref, x_ref, ys_hbm, gate_ref, g_ref, b_ref, sc_ref, sh_ref, *rest, alpha, tm, emit_h):
    if emit_h:
        xo_ref, ho_ref, buf, sem = rest
    else:
        xo_ref, buf, sem = rest
        ho_ref = None
    _row_gather(ys_hbm, idx_ref, pl.program_id(0) * tm, tm, buf, sem)
    y = alpha * x_ref[...] + gate_ref[0] * buf[...]
    _ln_tail(y, g_ref, b_ref, sc_ref, sh_ref, xo_ref, ho_ref)


def _resid_ln_gather(x, ys, dest, mod, gate_chunk, ln_g, ln_b, sc_chunk, sh_chunk, seq, n_batch, alpha, emit_h, mod_next):
    T, D = x.shape
    tm = _tile(seq, (256,))
    tps = seq // tm
    row = pl.BlockSpec((tm, D), lambda i, idx: (i, 0))
    vec = pl.BlockSpec((1, D), lambda i, idx: (0, 0))
    out_specs = [row, row] if emit_h else [row]
    out_shape = [jax.ShapeDtypeStruct((T, D), _F32)] + ([jax.ShapeDtypeStruct((T, D), _BF)] if emit_h else [])
    res = pl.pallas_call(
        functools.partial(_resid_ln_gather_kernel, alpha=alpha, tm=tm, emit_h=emit_h),
        grid_spec=pltpu.PrefetchScalarGridSpec(
            num_scalar_prefetch=1,
            grid=(T // tm,),
            in_specs=[row, pl.BlockSpec(memory_space=pl.ANY), _mod_spec(gate_chunk, D, tps, n_batch), vec, vec,
                      _mod_spec(sc_chunk, D, tps, n_batch), _mod_spec(sh_chunk, D, tps, n_batch)],
            out_specs=out_specs,
            scratch_shapes=[pltpu.VMEM((tm, D), _F32), pltpu.SemaphoreType.DMA(())]),
        out_shape=out_shape,
        compiler_params=_params(("arbitrary",), 40),
        name="resid_ln_gather",
    )(dest, x, ys, mod, ln_g.reshape(1, D), ln_b.reshape(1, D), mod_next, mod_next)
    return res if emit_h else (res[0], None)


def _mm_kernel(a_ref, b_ref, o_ref):
    o_ref[...] = jnp.dot(a_ref[...], b_ref[...], preferred_element_type=_F32).astype(o_ref.dtype)


def _matmul(a, b, out_dtype, name):
    M, K = a.shape
    N = b.shape[1]
    tm = _tile(M, (1088, 768, 512, 256))
    tn = 512
    return pl.pallas_call(
        _mm_kernel,
        grid=(M // tm, N // tn),
        in_specs=[pl.BlockSpec((tm, K), lambda i, j: (i, 0)), pl.BlockSpec((K, tn), lambda i, j: (0, j))],
        out_specs=pl.BlockSpec((tm, tn), lambda i, j: (i, j)),
        out_shape=jax.ShapeDtypeStruct((M, N), out_dtype),
        compiler_params=_params(("parallel", "parallel"), 48),
        name=name,
    )(a, b)


def _mm3_kernel(a1_ref, a2_ref, a3_ref, b1_ref, b2_ref, b3_ref, o_ref):
    acc = jnp.dot(a1_ref[...], b1_ref[...], preferred_element_type=_F32)
    acc = acc + jnp.dot(a2_ref[...], b2_ref[...], preferred_element_type=_F32)
    acc = acc + jnp.dot(a3_ref[...], b3_ref[...], preferred_element_type=_F32)
    o_ref[...] = acc


def _out_proj(att, fmix, cmix, w_out):
    M = att.shape[0]
    k1, k2, k3 = att.shape[1], fmix.shape[1], cmix.shape[1]
    N = w_out.shape[1]
    assert k2 == k3 and k1 % k2 == 0
    tm = _tile(M, (1088, 768, 512, 256))
    tn = 512
    return pl.pallas_call(
        _mm3_kernel,
        grid=(M // tm, N // tn),
        in_specs=[pl.BlockSpec((tm, k1), lambda i, j: (i, 0)),
                  pl.BlockSpec((tm, k2), lambda i, j: (i, 0)),
                  pl.BlockSpec((tm, k3), lambda i, j: (i, 0)),
                  pl.BlockSpec((k1, tn), lambda i, j: (0, j)),
                  pl.BlockSpec((k2, tn), lambda i, j: (k1 // k2, j)),
                  pl.BlockSpec((k3, tn), lambda i, j: (k1 // k2 + 1, j))],
        out_specs=pl.BlockSpec((tm, tn), lambda i, j: (i, j)),
        out_shape=jax.ShapeDtypeStruct((M, N), _F32),
        compiler_params=_params(("parallel", "parallel"), 48),
        name="out_proj",
    )(att, fmix, cmix, w_out, w_out, w_out)


def _rms_bf16(x, g):
    return (x * lax.rsqrt(jnp.mean(x * x, axis=-1, keepdims=True) + EPS) * g).astype(_BF)


def _rope_pair(t):
    return t + pltpu.roll(t, HEAD_ROPE, 1)


def _qproj_kernel(cq_ref, g_ref, w_ref, cs_ref, o_ref, *, heads, scale):
    xn = _rms_bf16(cq_ref[...], g_ref[...])
    y = jnp.dot(xn, w_ref[...], preferred_element_type=_F32)
    cs = cs_ref[...]
    for h in range(heads):
        lo = h * HEAD_QK
        o_ref[:, lo:lo + HEAD_V] = (y[:, lo:lo + HEAD_V] * scale).astype(_BF)
        t = y[:, lo + HEAD_V:lo + HEAD_QK] * cs
        o_ref[:, lo + HEAD_V:lo + HEAD_QK] = (_rope_pair(t) * scale).astype(_BF)


def _q_proj(y_in, g_q, w_uq_p, cs, q_rank, scale):
    T = y_in.shape[0]
    N = w_uq_p.shape[1]
    tm = _tile(T, (512, 256))
    hb = 4
    tn = hb * HEAD_QK
    return pl.pallas_call(
        functools.partial(_qproj_kernel, heads=hb, scale=scale),
        grid=(T // tm, N // tn),
        in_specs=[pl.BlockSpec((tm, q_rank), lambda i, j: (i, 0)),
                  pl.BlockSpec((1, q_rank), lambda i, j: (0, 0)),
                  pl.BlockSpec((q_rank, tn), lambda i, j: (0, j)),
                  pl.BlockSpec((tm, LANES), lambda i, j: (i, 0))],
        out_specs=pl.BlockSpec((tm, tn), lambda i, j: (i, j)),
        out_shape=jax.ShapeDtypeStruct((T, N), _BF),
        compiler_params=_params(("parallel", "parallel"), 32),
        name="q_proj",
    )(y_in, g_q.reshape(1, q_rank), w_uq_p, cs)


def _kvproj_kernel(ckv_ref, g_ref, wk_ref, wv_ref, kr_ref, cs_ref, k_ref, v_ref, *, heads):
    xn = _rms_bf16(ckv_ref[...], g_ref[...])
    kn = jnp.dot(xn, wk_ref[...], preferred_element_type=_F32)
    v_ref[...] = jnp.dot(xn, wv_ref[...], preferred_element_type=_F32).astype(_BF)
    t = kr_ref[...] * cs_ref[...]
    lane = lax.broadcasted_iota(jnp.int32, t.shape, 1)
    krf = jnp.where(lane < HEAD_ROPE, _rope_pair(t), 0.0).astype(_BF)
    for h in range(heads):
        k_ref[:, h * HEAD_QK:h * HEAD_QK + HEAD_V] = kn[:, h * HEAD_V:(h + 1) * HEAD_V].astype(_BF)
        k_ref[:, h * HEAD_QK + HEAD_V:(h + 1) * HEAD_QK] = krf


def _kv_proj(y_in, g_kv, w_ukv_p, cs, kv_rank, ckv_off, kr_off, n_heads):
    T = y_in.shape[0]
    tm = _tile(T, (512, 256))
    hb = 4
    nb = n_heads // hb
    return pl.pallas_call(
        functools.partial(_kvproj_kernel, heads=hb),
        grid=(T // tm, nb),
        in_specs=[pl.BlockSpec((tm, kv_rank), lambda i, j: (i, ckv_off // kv_rank)),
                  pl.BlockSpec((1, kv_rank), lambda i, j: (0, 0)),
                  pl.BlockSpec((kv_rank, hb * HEAD_V), lambda i, j: (0, j)),
                  pl.BlockSpec((kv_rank, hb * HEAD_V), lambda i, j: (0, nb + j)),
                  pl.BlockSpec((tm, LANES), lambda i, j: (i, kr_off // LANES)),
                  pl.BlockSpec((tm, LANES), lambda i, j: (i, 0))],
        out_specs=[pl.BlockSpec((tm, hb * HEAD_QK), lambda i, j: (i, j)),
                   pl.BlockSpec((tm, hb * HEAD_V), lambda i, j: (i, j))],
        out_shape=[jax.ShapeDtypeStruct((T, n_heads * HEAD_QK), _BF),
                   jax.ShapeDtypeStruct((T, n_heads * HEAD_V), _BF)],
        compiler_params=_params(("parallel", "parallel"), 32),
        name="kv_proj",
    )(y_in, g_kv.reshape(1, kv_rank), w_ukv_p, w_ukv_p, y_in, cs)


_NT = (((1,), (1,)), ((), ()))


def _attn_kernel(q_ref, kl_ref, vl_ref, kc_ref, vc_ref, o_ref, *, n_lat_tiles):
    qi = pl.program_id(2)
    q = q_ref[...]
    s_c = lax.dot_general(q, kc_ref[...], _NT, preferred_element_type=_F32)
    m_c = jnp.max(s_c, axis=-1, keepdims=True)

    @pl.when(qi < n_lat_tiles)
    def _():
        s_l = lax.dot_general(q, kl_ref[...], _NT, preferred_element_type=_F32)
        m = jnp.maximum(m_c, jnp.max(s_l, axis=-1, keepdims=True))
        p_c = jnp.exp(s_c - m)
        p_l = jnp.exp(s_l - m)
        den = jnp.sum(p_c, axis=-1, keepdims=True) + jnp.sum(p_l, axis=-1, keepdims=True)
        o = jnp.dot(p_c.astype(_BF), vc_ref[...], preferred_element_type=_F32)
        o = o + jnp.dot(p_l.astype(_BF), vl_ref[...], preferred_element_type=_F32)
        o_ref[...] = (o / den).astype(_BF)

    @pl.when(qi >= n_lat_tiles)
    def _():
        p_c = jnp.exp(s_c - m_c)
        den = jnp.sum(p_c, axis=-1, keepdims=True)
        o = jnp.dot(p_c.astype(_BF), vc_ref[...], preferred_element_type=_F32)
        o_ref[...] = (o / den).astype(_BF)


def _attention(q, k, v, n_batch, seq, ctx, n_heads):
    T = q.shape[0]
    tq = 256
    nlt, nct = seq // tq, ctx // tq
    ctx_blk0 = n_batch * seq // ctx

    def q_rows(b, h, i):
        return jnp.where(i < nlt, b * nlt + i, n_batch * nlt + b * nct + (i - nlt))

    return pl.pallas_call(
        functools.partial(_attn_kernel, n_lat_tiles=nlt),
        grid=(n_batch, n_heads, nlt + nct),
        in_specs=[pl.BlockSpec((tq, HEAD_QK), lambda b, h, i: (q_rows(b, h, i), h)),
                  pl.BlockSpec((seq, HEAD_QK), lambda b, h, i: (b, h)),
                  pl.BlockSpec((seq, HEAD_V), lambda b, h, i: (b, h)),
                  pl.BlockSpec((ctx, HEAD_QK), lambda b, h, i: (ctx_blk0 + b, h)),
                  pl.BlockSpec((ctx, HEAD_V), lambda b, h, i: (ctx_blk0 + b, h))],
        out_specs=pl.BlockSpec((tq, HEAD_V), lambda b, h, i: (q_rows(b, h, i), h)),
        out_shape=jax.ShapeDtypeStruct((T, n_heads * HEAD_V), _BF),
        compiler_params=_params(("parallel", "parallel", "arbitrary"), 48),
        name="attention",
    )(q, k, v, k, v)


def _dft_mats(n):
    j = jnp.arange(n, dtype=jnp.int32)
    jk = (j[:, None] * j[None, :]) % n
    ang = jk.astype(_F32) * (2.0 * math.pi / n)
    return jnp.cos(ang).astype(_BF), jnp.sin(ang).astype(_BF)


def _dft_ch_kernel(f_ref, cs_ref, pc_ref, ps_ref, *, fd):
    y = jnp.dot(f_ref[...].astype(_BF), cs_ref[...], preferred_element_type=_F32)
    pc_ref[...] = y[:, :fd].astype(_BF)
    ps_ref[...] = y[:, fd:].astype(_BF)


def _dft_channels(y_in, cs_mat, f_off, fd):
    T = y_in.shape[0]
    tm = _tile(T, (512, 256))
    out = jax.ShapeDtypeStruct((T, F_GROUPS * fd), _BF)
    return pl.pallas_call(
        functools.partial(_dft_ch_kernel, fd=fd),
        grid=(T // tm, F_GROUPS),
        in_specs=[pl.BlockSpec((tm, fd), lambda i, g: (i, f_off // fd + g)),
                  pl.BlockSpec((fd, 2 * fd), lambda i, g: (0, 0))],
        out_specs=[pl.BlockSpec((tm, fd), lambda i, g: (i, g)), pl.BlockSpec((tm, fd), lambda i, g: (i, g))],
        out_shape=[out, out],
        compiler_params=_params(("parallel", "parallel"), 32),
        name="dft_channels",
    )(y_in, cs_mat)


def _dft_pos_kernel(c_ref, s_ref, pc_ref, ps_ref, wf_ref, o_ref, *, norm, fd, groups):
    acc = jnp.dot(c_ref[...], pc_ref[...], preferred_element_type=_F32)
    acc = acc - jnp.dot(s_ref[...], ps_ref[...], preferred_element_type=_F32)
    fr = (acc * norm).astype(_BF)
    for g in range(groups):
        o_ref[:, g * fd:(g + 1) * fd] = jnp.dot(fr[:, g * fd:(g + 1) * fd], wf_ref[g],
                                                preferred_element_type=_F32).astype(_BF)


def _dft_positions(pc, ps, cmat, smat, w_f, n_batch, n, row0, fd):
    tm = min(512, n)
    gb = 2
    tn = gb * fd
    blk0 = row0 // n
    return pl.pallas_call(
        functools.partial(_dft_pos_kernel, norm=1.0 / math.sqrt(n * fd), fd=fd, groups=gb),
        grid=(n_batch, F_GROUPS // gb, n // tm),
        in_specs=[pl.BlockSpec((tm, n), lambda b, j, m: (m, 0)),
                  pl.BlockSpec((tm, n), lambda b, j, m: (m, 0)),
                  pl.BlockSpec((n, tn), lambda b, j, m: (blk0 + b, j)),
                  pl.BlockSpec((n, tn), lambda b, j, m: (blk0 + b, j)),
                  pl.BlockSpec((gb, fd, fd), lambda b, j, m: (j, 0, 0))],
        out_specs=pl.BlockSpec((tm, tn), lambda b, j, m: (b * (n // tm) + m, j)),
        out_shape=jax.ShapeDtypeStruct((n_batch * n, F_GROUPS * fd), _BF),
        compiler_params=_params(("parallel", "parallel", "arbitrary"), 48),
        name=f"dft_positions_{n}",
    )(cmat, smat, pc, ps, w_f)


def _chunk_kernel(u_ref, v_ref, ws_ref, bs_ref, o_ref, *, heads, cdim, n_chunks):
    u = jax.nn.gelu(u_ref[...])
    v = jax.nn.gelu(v_ref[...])
    for h in range(heads):
        vh = v[:, h * cdim:(h + 1) * cdim]
        mu = jnp.mean(vh, axis=-1, keepdims=True)
        d = vh - mu
        var = jnp.mean(d * d, axis=-1, keepdims=True)
        vs = (d * lax.rsqrt(var + EPS)).astype(_BF)
        w = ws_ref[h]
        bias = bs_ref[h]
        for c in range(n_chunks):
            rows = slice(c * CHUNK, (c + 1) * CHUNK)
            sv = jnp.dot(w, vs[rows], preferred_element_type=_F32) + bias
            o_ref[rows, h * cdim:(h + 1) * cdim] = (u[rows, h * cdim:(h + 1) * cdim] * sv).astype(_BF)


def _chunk_mix(y_in, w_s, b_s, u_off, width):
    T = y_in.shape[0]
    tm = _tile(T, (512, 256))
    cdim = width // C_HEADS
    return pl.pallas_call(
        functools.partial(_chunk_kernel, heads=C_HEADS, cdim=cdim, n_chunks=tm // CHUNK),
        grid=(T // tm,),
        in_specs=[pl.BlockSpec((tm, width), lambda i: (i, u_off // width)),
                  pl.BlockSpec((tm, width), lambda i: (i, u_off // width + 1)),
                  pl.BlockSpec((C_HEADS, CHUNK, CHUNK), lambda i: (0, 0, 0)),
                  pl.BlockSpec((C_HEADS, CHUNK, 1), lambda i: (0, 0, 0))],
        out_specs=pl.BlockSpec((tm, width), lambda i: (i, 0)),
        out_shape=jax.ShapeDtypeStruct((T, width), _BF),
        compiler_params=_params(("parallel",), 32),
        name="chunk_mix",
    )(y_in, y_in, w_s, b_s.reshape(C_HEADS, CHUNK, 1))


_R_GIDX = N_GROUPS
_R_SEG = N_GROUPS + 1


def _router_kernel(h_ref, w_ref, b_ref, r_ref, *, tiles_per_seq, n_batch):
    lg = jnp.dot(h_ref[...], w_ref[...], preferred_element_type=_F32) + b_ref[...]
    lane = lax.broadcasted_iota(jnp.int32, lg.shape, 1)

    def col(j):
        return jnp.sum(jnp.where(lane == j, lg, 0.0), axis=-1, keepdims=True)

    g = [col(j) for j in range(N_GROUPS)]
    gmax = functools.reduce(jnp.maximum, g)
    gi = jnp.full(gmax.shape, N_GROUPS - 1, jnp.int32)
    for j in reversed(range(N_GROUPS - 1)):
        gi = jnp.where(g[j] == gmax, j, gi)
    gw = 1.0 / functools.reduce(lambda a, b: a + b, [jnp.exp(x - gmax) for x in g])

    e = []
    for j in range(EXP_PER_GROUP):
        ej = col(N_GROUPS + j)
        for gg in range(1, N_GROUPS):
            ej = jnp.where(gi == gg, col(N_GROUPS + gg * EXP_PER_GROUP + j), ej)
        e.append(ej)
    v1 = functools.reduce(jnp.maximum, e)
    i1 = jnp.full(v1.shape, EXP_PER_GROUP - 1, jnp.int32)
    for j in reversed(range(EXP_PER_GROUP - 1)):
        i1 = jnp.where(e[j] == v1, j, i1)
    e2 = [jnp.where(i1 == j, -jnp.inf, e[j]) for j in range(EXP_PER_GROUP)]
    v2 = functools.reduce(jnp.maximum, e2)
    i2 = jnp.full(v2.shape, EXP_PER_GROUP - 1, jnp.int32)
    for j in reversed(range(EXP_PER_GROUP - 1)):
        i2 = jnp.where(e2[j] == v2, j, i2)
    t = jnp.exp(v2 - v1)
    w1 = (1.0 / (1.0 + t)) * gw
    w2 = (t / (1.0 + t)) * gw

    seg = jnp.minimum(pl.program_id(0) // tiles_per_seq, n_batch).astype(_F32)
    rec = jnp.where(lane == _R_GIDX, gi.astype(_F32), 0.0) + jnp.where(lane == _R_SEG, seg, 0.0)
    for j in range(EXP_PER_GROUP):
        cw = jnp.where(i1 == j, w1, 0.0) + jnp.where(i2 == j, w2, 0.0)
        rec = rec + jnp.where(lane == j, cw, 0.0)
    r_ref[...] = rec


def _router(h2, w_r, b_r, seq, n_batch):
    T, D = h2.shape
    tm = _tile(seq, (512, 256))
    return pl.pallas_call(
        functools.partial(_router_kernel, tiles_per_seq=seq // tm, n_batch=n_batch),
        grid=(T // tm,),
        in_specs=[pl.BlockSpec((tm, D), lambda i: (i, 0)),
                  pl.BlockSpec((D, LANES), lambda i: (0, 0)),
                  pl.BlockSpec((1, LANES), lambda i: (0, 0))],
        out_specs=pl.BlockSpec((tm, LANES), lambda i: (i, 0)),
        out_shape=jax.ShapeDtypeStruct((T, LANES), _F32),
        compiler_params=_params(("parallel",), 32),
        name="router",
    )(h2, w_r, b_r)


def _route_plan(rec, tm):
    T = rec.shape[0]
    n_tiles = T // tm + N_GROUPS
    gidx = rec[:, _R_GIDX].astype(jnp.int32)
    onehot = (gidx[:, None] == jnp.arange(N_GROUPS, dtype=jnp.int32)[None, :]).astype(jnp.int32)
    rank = jnp.sum((jnp.cumsum(onehot, axis=0) - onehot) * onehot, axis=1)
    counts = jnp.sum(onehot, axis=0)
    tiles_g = (counts + tm - 1) // tm
    tile_end = jnp.cumsum(tiles_g)
    start = (tile_end - tiles_g) * tm
    dest = (jnp.sum(onehot * start[None, :], axis=1) + rank).astype(jnp.int32)
    src = jnp.zeros((n_tiles * tm,), jnp.int32).at[dest].set(jnp.arange(T, dtype=jnp.int32))
    tile_id = jnp.arange(n_tiles, dtype=jnp.int32)
    tile_group = jnp.minimum(jnp.sum((tile_id[:, None] >= tile_end[None, :]).astype(jnp.int32), axis=1), N_GROUPS - 1)
    tile_valid = (tile_id < tile_end[-1]).astype(jnp.int32)
    return dest, src, tile_group.astype(jnp.int32), tile_valid


def _moe_gather_kernel(idx_ref, x_hbm, r_hbm, mod_ref, xs_ref, rs_ref, xbuf, rbuf, sem_x, sem_r, *, tm, n_seg, sc_chunk, sh_chunk, d):
    base = pl.program_id(0) * tm
    _row_gather(x_hbm, idx_ref, base, tm, xbuf, sem_x)
    _row_gather(r_hbm, idx_ref, base, tm, rbuf, sem_r)
    rec = rbuf[...]
    rs_ref[...] = rec
    lane = lax.broadcasted_iota(jnp.int32, rec.shape, 1)
    seg = jnp.sum(jnp.where(lane == _R_SEG, rec, 0.0), axis=-1, keepdims=True)
    sc = mod_ref[0:1, sc_chunk * d:(sc_chunk + 1) * d]
    sh = mod_ref[0:1, sh_chunk * d:(sh_chunk + 1) * d]
    for s in range(1, n_seg):
        sc = jnp.where(seg == float(s), mod_ref[s:s + 1, sc_chunk * d:(sc_chunk + 1) * d], sc)
        sh = jnp.where(seg == float(s), mod_ref[s:s + 1, sh_chunk * d:(sh_chunk + 1) * d], sh)
    xs_ref[...] = (xbuf[...] * (1.0 + sc) + sh).astype(_BF)


def _moe_gather(x, rec, src, mod_rows, n_seg, sc_chunk, sh_chunk):
    T, D = x.shape
    P = src.shape[0]
    tm = 256
    return pl.pallas_call(
        functools.partial(_moe_gather_kernel, tm=tm, n_seg=n_seg, sc_chunk=sc_chunk, sh_chunk=sh_chunk, d=D),
        grid_spec=pltpu.PrefetchScalarGridSpec(
            num_scalar_prefetch=1,
            grid=(P // tm,),
            in_specs=[pl.BlockSpec(memory_space=pl.ANY), pl.BlockSpec(memory_space=pl.ANY),
                      pl.BlockSpec((MOD_ROWS, 6 * D), lambda i, idx: (0, 0))],
            out_specs=[pl.BlockSpec((tm, D), lambda i, idx: (i, 0)), pl.BlockSpec((tm, LANES), lambda i, idx: (i, 0))],
            scratch_shapes=[pltpu.VMEM((tm, D), _F32), pltpu.VMEM((tm, LANES), _F32),
                            pltpu.SemaphoreType.DMA(()), pltpu.SemaphoreType.DMA(())]),
        out_shape=[jax.ShapeDtypeStruct((P, D), _BF), jax.ShapeDtypeStruct((P, LANES), _F32)],
        compiler_params=_params(("arbitrary",), 32),
        name="moe_gather",
    )(src, x, rec, mod_rows)


def _expert_kernel(tg_ref, tv_ref, xs_ref, rs_ref, wg_ref, wu_ref, wd_ref, o_ref, *, tc, d_expert):
    j, c = pl.program_id(0), pl.program_id(1)

    @pl.when(tv_ref[j] == 1)
    def _():
        x = xs_ref[...]
        gate = jnp.dot(x, wg_ref[0], preferred_element_type=_F32)
        up = jnp.dot(x, wu_ref[0], preferred_element_type=_F32)
        rec = rs_ref[...]
        col = c * tc + lax.broadcasted_iota(jnp.int32, (1, tc), 1)
        cw = jnp.zeros(gate.shape, _F32)
        for e in range(EXP_PER_GROUP):
            in_e = (col >= e * d_expert) & (col < (e + 1) * d_expert)
            cw = jnp.where(in_e, rec[:, e:e + 1], cw)
        hid = (jax.nn.silu(gate) * up) * cw
        y = jnp.dot(hid.astype(_BF), wd_ref[0], preferred_element_type=_F32)

        @pl.when(c == 0)
        def _():
            o_ref[...] = y

        @pl.when(c > 0)
        def _():
            o_ref[...] += y

    @pl.when((tv_ref[j] == 0) & (c == 0))
    def _():
        o_ref[...] = jnp.zeros(o_ref.shape, o_ref.dtype)


def _experts(xs, rs, tile_group, tile_valid, w_gate_g, w_up_g, w_down_g, tm, d_expert):
    P, D = xs.shape
    H = w_gate_g.shape[2]
    tc = 256
    return pl.pallas_call(
        functools.partial(_expert_kernel, tc=tc, d_expert=d_expert),
        grid_spec=pltpu.PrefetchScalarGridSpec(
            num_scalar_prefetch=2,
            grid=(P // tm, H // tc),
            in_specs=[pl.BlockSpec((tm, D), lambda j, c, tg, tv: (j, 0)),
                      pl.BlockSpec((tm, LANES), lambda j, c, tg, tv: (j, 0)),
                      pl.BlockSpec((1, D, tc), lambda j, c, tg, tv: (tg[j], 0, c)),
                      pl.BlockSpec((1, D, tc), lambda j, c, tg, tv: (tg[j], 0, c)),
                      pl.BlockSpec((1, tc, D), lambda j, c, tg, tv: (tg[j], c, 0))],
            out_specs=pl.BlockSpec((tm, D), lambda j, c, tg, tv: (j, 0))),
        out_shape=jax.ShapeDtypeStruct((P, D), _F32),
        compiler_params=_params(("parallel", "arbitrary"), 56),
        name="experts",
    )(tile_group, tile_valid, xs, rs, w_gate_g, w_up_g, w_down_g)


def _rot_cols(w):
    q = HEAD_ROPE // 4
    a, b, c, d = w[..., :q], w[..., q:2 * q], w[..., 2 * q:3 * q], w[..., 3 * q:]
    return jnp.concatenate([-b, a, -d, c], axis=-1)


def _prep_w_in(w_in, D):
    q, kv = D // 4, D // 8
    o1, o2, o3 = q, q + kv, q + kv + HEAD_ROPE
    cq, ckv, kr, rest = w_in[..., :o1], w_in[..., o1:o2], w_in[..., o2:o3], w_in[..., o3:]
    cols = [cq, rest, ckv, kr, _rot_cols(kr)]
    width = sum(c.shape[-1] for c in cols)
    pad = (-width) % 512
    cols.append(jnp.zeros(w_in.shape[:-1] + (pad,), w_in.dtype))
    return jnp.concatenate(cols, axis=-1).astype(_BF)


def _prep_w_uq(w_uq, n_heads):
    L, r, _ = w_uq.shape
    w = w_uq.reshape(L, r, n_heads, HEAD_V + HEAD_ROPE)
    rope = w[..., HEAD_V:]
    return jnp.concatenate([w[..., :HEAD_V], rope, _rot_cols(rope)], axis=-1).reshape(L, r, n_heads * HEAD_QK).astype(_BF)


def _prep_w_ukv(w_ukv, n_heads):
    L, r, _ = w_ukv.shape
    w = w_ukv.reshape(L, r, n_heads, 2 * HEAD_V)
    return jnp.concatenate([w[..., :HEAD_V].reshape(L, r, -1), w[..., HEAD_V:].reshape(L, r, -1)], axis=-1).astype(_BF)


def _prep_expert_in(w):
    L, E, D, F = w.shape
    w = w.reshape(L, N_GROUPS, EXP_PER_GROUP, D, F)
    return jnp.transpose(w, (0, 1, 3, 2, 4)).reshape(L, N_GROUPS, D, EXP_PER_GROUP * F).astype(_BF)


def _rope_table(n_batch, seq, ctx):
    half = HEAD_ROPE // 2
    inv_freq = ROPE_THETA ** (-jnp.arange(0, half, 2, dtype=_F32) / half)
    pos = jnp.arange(seq)
    ang_r = (pos // GRID_W).astype(_F32)[:, None] * inv_freq
    ang_c = (pos % GRID_W).astype(_F32)[:, None] * inv_freq
    ang = jnp.concatenate([ang_r, ang_r, ang_c, ang_c], axis=-1)
    lat = jnp.concatenate([jnp.cos(ang), jnp.sin(ang)], axis=-1)
    ctx_rows = jnp.concatenate([jnp.ones((n_batch * ctx, HEAD_ROPE), _F32), jnp.zeros((n_batch * ctx, HEAD_ROPE), _F32)], axis=-1)
    return jnp.concatenate([jnp.tile(lat, (n_batch, 1)), ctx_rows], axis=0)


def kernel(x, c, ctx, c_ctx, w_ada, b_ada, w_in, q_norm_g, kv_norm_g, w_uq, w_ukv, w_fourier, w_spatial, b_spatial, w_out, ln1_g, ln1_b, w_router_group, b_router_group, w_router_expert, b_router_expert, w_gate, w_up, w_down, ln2_g, ln2_b):
    B, SEQ, D = x.shape
    CTX = ctx.shape[1]
    L = w_ada.shape[0]
    n_heads = (D // 2) // HEAD_V
    q_rank, kv_rank = D // 4, D // 8
    fd = D // 16
    d_expert = w_gate.shape[-1]
    alpha = (2.0 * L) ** 0.25
    n_seg = B + 1
    assert n_seg <= MOD_ROWS and SEQ % 256 == 0 and (B * CTX) % 256 == 0 and CTX == 256 and (B * SEQ) % CTX == 0
    moe_tm = 512

    f_off, u_off, ckv_off, kr_off = q_rank, 2 * q_rank, D, D + kv_rank

    w_in_p = _prep_w_in(w_in, D)
    w_uq_p = _prep_w_uq(w_uq, n_heads)
    w_ukv_p = _prep_w_ukv(w_ukv, n_heads)
    w_out_b = w_out.astype(_BF)
    w_f_b = w_fourier.astype(_BF)
    w_s_b = w_spatial.astype(_BF)
    w_gate_g = _prep_expert_in(w_gate)
    w_up_g = _prep_expert_in(w_up)
    w_down_g = w_down.reshape(L, N_GROUPS, EXP_PER_GROUP * d_expert, D).astype(_BF)
    n_logits = N_GROUPS + N_GROUPS * EXP_PER_GROUP
    w_r = jnp.concatenate([w_router_group, w_router_expert, jnp.zeros((L, D, LANES - n_logits), _F32)], axis=-1).astype(_BF)
    b_r = jnp.concatenate([b_router_group, b_router_expert, jnp.zeros((L, LANES - n_logits), _F32)], axis=-1).reshape(L, 1, LANES)
    cs_tab = _rope_table(B, SEQ, CTX)
    c_lat, s_lat = _dft_mats(SEQ)
    c_ctx_m, s_ctx_m = _dft_mats(CTX)
    c_ch, s_ch = _dft_mats(fd)
    cs_ch = jnp.concatenate([c_ch, s_ch], axis=1)

    c_rows = jnp.concatenate([c, c_ctx[None, :], jnp.zeros((MOD_ROWS - n_seg, D), _F32)], axis=0)
    mod_all = _adaln(c_rows, w_ada, b_ada)

    xt = jnp.concatenate([x.reshape(B * SEQ, D), ctx.reshape(B * CTX, D)], axis=0)
    mod3 = [mod_all[l].reshape(MOD_ROWS * 6, 1, D) for l in range(L)]
    h = _modulate(xt, mod3[0], 1, 0, SEQ, B)

    for l in range(L):
        last = l == L - 1
        y_in = _matmul(h, w_in_p[l], _F32, "in_proj")
        q = _q_proj(y_in, q_norm_g[l], w_uq_p[l], cs_tab, q_rank, 1.0 / math.sqrt(HEAD_V + HEAD_ROPE))
        k, v = _kv_proj(y_in, kv_norm_g[l], w_ukv_p[l], cs_tab, kv_rank, ckv_off, kr_off, n_heads)
        att = _attention(q, k, v, B, SEQ, CTX, n_heads)
        pc, ps = _dft_channels(y_in, cs_ch, f_off, fd)
        fmix = jnp.concatenate([
            _dft_positions(pc, ps, c_lat, s_lat, w_f_b[l], B, SEQ, 0, fd),
            _dft_positions(pc, ps, c_ctx_m, s_ctx_m, w_f_b[l], B, CTX, B * SEQ, fd)], axis=0)
        cmix = _chunk_mix(y_in, w_s_b[l], b_spatial[l], u_off, q_rank)
        mixed = _out_proj(att, fmix, cmix, w_out_b[l])
        x1, h2 = _resid_ln(xt, mixed, mod3[l], 2, ln1_g[l], ln1_b[l], 4, 3, SEQ, B, alpha)

        rec = _router(h2, w_r[l], b_r[l], SEQ, B)
        dest, src, tile_group, tile_valid = _route_plan(rec, moe_tm)
        xs, rs = _moe_gather(x1, rec, src, mod_all[l], n_seg, 4, 3)
        ys = _experts(xs, rs, tile_group, tile_valid, w_gate_g[l], w_up_g[l], w_down_g[l], moe_tm, d_expert)
        xt, h = _resid_ln_gather(x1, ys, dest, mod3[l], 5, ln2_g[l], ln2_b[l], 1, 0, SEQ, B, alpha,
                                 emit_h=not last, mod_next=mod3[min(l + 1, L - 1)])

    return xt[:B * SEQ].reshape(B, SEQ, D)
```

```python
import functools
import math

import jax
import jax.numpy as jnp
from jax import lax
from jax.experimental import pallas as pl
from jax.experimental.pallas import tpu as pltpu

_BF = jnp.bfloat16
_F32 = jnp.float32

ROPE_THETA = 10000.0
EPS = 1e-6
GRID_W = 64
HEAD_V = 128
HEAD_ROPE = 64
HEAD_QK = 256
F_GROUPS = 4
C_HEADS = 4
CHUNK = 128
N_GROUPS = 4
EXP_PER_GROUP = 4
LANES = 128
MOD_ROWS = 8

_MIB = 1024 * 1024


def _params(sem, vmem_mib):
    return pltpu.CompilerParams(dimension_semantics=sem, vmem_limit_bytes=vmem_mib * _MIB)


def _tile(n, prefs):
    for t in prefs:
        if n % t == 0:
            return t
    raise ValueError(f"no tile in {prefs} divides {n}")


def _adaln_kernel(c_ref, w_ref, b_ref, o_ref):
    s = jax.nn.silu(c_ref[...]).astype(_BF)
    o_ref[0] = jnp.dot(s, w_ref[0].astype(_BF), preferred_element_type=_F32) + b_ref[0]


def _adaln(c_rows, w_ada, b_ada):
    L, D, N = w_ada.shape
    tn = 512
    return pl.pallas_call(
        _adaln_kernel,
        grid=(L, N // tn),
        in_specs=[pl.BlockSpec((MOD_ROWS, D), lambda l, j: (0, 0)),
                  pl.BlockSpec((1, D, tn), lambda l, j: (l, 0, j)),
                  pl.BlockSpec((1, 1, tn), lambda l, j: (l, 0, j))],
        out_specs=pl.BlockSpec((1, MOD_ROWS, tn), lambda l, j: (l, 0, j)),
        out_shape=jax.ShapeDtypeStruct((L, MOD_ROWS, N), _F32),
        compiler_params=_params(("parallel", "parallel"), 40),
        name="adaln",
    )(c_rows, w_ada, b_ada.reshape(L, 1, N))


def _mod_spec(chunk, D, tiles_per_seq, n_batch):
    return pl.BlockSpec((1, 1, D), lambda i, *_: (jnp.minimum(i // tiles_per_seq, n_batch) * 6 + chunk, 0, 0))


def _modulate_kernel(x_ref, sc_ref, sh_ref, o_ref):
    o_ref[...] = (x_ref[...] * (1.0 + sc_ref[0]) + sh_ref[0]).astype(_BF)


def _modulate(x, mod, sc_chunk, sh_chunk, seq, n_batch):
    T, D = x.shape
    tm = _tile(seq, (256,))
    tps = seq // tm
    return pl.pallas_call(
        _modulate_kernel,
        grid=(T // tm,),
        in_specs=[pl.BlockSpec((tm, D), lambda i: (i, 0)),
                  _mod_spec(sc_chunk, D, tps, n_batch), _mod_spec(sh_chunk, D, tps, n_batch)],
        out_specs=pl.BlockSpec((tm, D), lambda i: (i, 0)),
        out_shape=jax.ShapeDtypeStruct((T, D), _BF),
        compiler_params=_params(("parallel",), 32),
        name="modulate",
    )(x, mod, mod)


def _pack_halves(hb):
    half = hb.shape[1] // 2
    bits = pltpu.bitcast(hb.astype(_F32), jnp.uint32)
    return (bits[:, half:] & jnp.uint32(0xFFFF0000)) | (bits[:, :half] >> 16)


def _unpack_halves(u):
    lo = pltpu.bitcast(u << 16, _F32).astype(_BF)
    hi = pltpu.bitcast(u & jnp.uint32(0xFFFF0000), _F32).astype(_BF)
    return lo, hi


def _ln_tail(y, g_ref, b_ref, sc_ref, sh_ref, xo_ref, ho_ref, hp_ref=None):
    mu = jnp.mean(y, axis=-1, keepdims=True)
    d = y - mu
    var = jnp.mean(d * d, axis=-1, keepdims=True)
    xn = d * lax.rsqrt(var + EPS) * g_ref[...] + b_ref[...]
    xo_ref[...] = xn
    if ho_ref is not None:
        hb = (xn * (1.0 + sc_ref[0]) + sh_ref[0]).astype(_BF)
        ho_ref[...] = hb
        if hp_ref is not None:
            hp_ref[...] = _pack_halves(hb)


def _resid_ln_kernel(x_ref, a_ref, gate_ref, g_ref, b_ref, sc_ref, sh_ref, xo_ref, ho_ref, hp_ref, *, alpha):
    y = alpha * x_ref[...] + gate_ref[0] * a_ref[...]
    _ln_tail(y, g_ref, b_ref, sc_ref, sh_ref, xo_ref, ho_ref, hp_ref)


def _resid_ln(x, a, mod, gate_chunk, ln_g, ln_b, sc_chunk, sh_chunk, seq, n_batch, alpha):
    T, D = x.shape
    tm = _tile(seq, (256,))
    tps = seq // tm
    row = pl.BlockSpec((tm, D), lambda i: (i, 0))
    half = pl.BlockSpec((tm, D // 2), lambda i: (i, 0))
    vec = pl.BlockSpec((1, D), lambda i: (0, 0))
    return pl.pallas_call(
        functools.partial(_resid_ln_kernel, alpha=alpha),
        grid=(T // tm,),
        in_specs=[row, row, _mod_spec(gate_chunk, D, tps, n_batch), vec, vec,
                  _mod_spec(sc_chunk, D, tps, n_batch), _mod_spec(sh_chunk, D, tps, n_batch)],
        out_specs=[row, row, half],
        out_shape=[jax.ShapeDtypeStruct((T, D), _F32), jax.ShapeDtypeStruct((T, D), _BF),
                   jax.ShapeDtypeStruct((T, D // 2), jnp.uint32)],
        compiler_params=_params(("parallel",), 40),
        name="resid_ln",
    )(x, a, mod, ln_g.reshape(1, D), ln_b.reshape(1, D), mod, mod)


def _row_gather(src_hbm, idx_ref, base, n, dst_ref, sem):
    def _copy(r):
        return pltpu.make_async_copy(src_hbm.at[pl.ds(idx_ref[base + r], 1)], dst_ref.at[pl.ds(r, 1)], sem)

    def _start(r, carry):
        _copy(r).start()
        return carry

    def _wait(r, carry):
        _copy(r).wait()
        return carry

    lax.fori_loop(0, n, _start, 0, unroll=8)
    lax.fori_loop(0, n, _wait, 0, unroll=8)


def _resid_ln_gather_kernel(idx_ref, x_ref, ys_hbm, gate_ref, g_ref, b_ref, sc_ref, sh_ref, *rest, alpha, tm, emit_h):
    if emit_h:
        xo_ref, ho_ref, buf, sem = rest
    else:
        xo_ref, buf, sem = rest
        ho_ref = None
    _row_gather(ys_hbm, idx_ref, pl.program_id(0) * tm, tm, buf, sem)
    y = alpha * x_ref[...] + gate_ref[0] * buf[...]
    _ln_tail(y, g_ref, b_ref, sc_ref, sh_ref, xo_ref, ho_ref)


def _resid_ln_gather(x, ys, dest, mod, gate_chunk, ln_g, ln_b, sc_chunk, sh_chunk, seq, n_batch, alpha, emit_h, mod_next):
    T, D = x.shape
    tm = _tile(seq, (256,))
    tps = seq // tm
    row = pl.BlockSpec((tm, D), lambda i, idx: (i, 0))
    vec = pl.BlockSpec((1, D), lambda i, idx: (0, 0))
    out_specs = [row, row] if emit_h else [row]
    out_shape = [jax.ShapeDtypeStruct((T, D), _F32)] + ([jax.ShapeDtypeStruct((T, D), _BF)] if emit_h else [])
    res = pl.pallas_call(
        functools.partial(_resid_ln_gather_kernel, alpha=alpha, tm=tm, emit_h=emit_h),
        grid_spec=pltpu.PrefetchScalarGridSpec(
            num_scalar_prefetch=1,
            grid=(T // tm,),
            in_specs=[row, pl.BlockSpec(memory_space=pl.ANY), _mod_spec(gate_chunk, D, tps, n_batch), vec, vec,
                      _mod_spec(sc_chunk, D, tps, n_batch), _mod_spec(sh_chunk, D, tps, n_batch)],
            out_specs=out_specs,
            scratch_shapes=[pltpu.VMEM((tm, D), _F32), pltpu.SemaphoreType.DMA(())]),
        out_shape=out_shape,
        compiler_params=_params(("arbitrary",), 40),
        name="resid_ln_gather",
    )(dest, x, ys, mod, ln_g.reshape(1, D), ln_b.reshape(1, D), mod_next, mod_next)
    return res if emit_h else (res[0], None)


def _mm_kernel(a_ref, b_ref, o_ref):
    o_ref[...] = jnp.dot(a_ref[...], b_ref[0], preferred_element_type=_F32).astype(o_ref.dtype)


def _matmul(a, b, l, out_dtype, name):
    M, K = a.shape
    N = b.shape[2]
    tm = _tile(M, (1088, 768, 512, 256))
    tn = 512
    return pl.pallas_call(
        _mm_kernel,
        grid=(M // tm, N // tn),
        in_specs=[pl.BlockSpec((tm, K), lambda i, j: (i, 0)), pl.BlockSpec((1, K, tn), lambda i, j: (l, 0, j))],
        out_specs=pl.BlockSpec((tm, tn), lambda i, j: (i, j)),
        out_shape=jax.ShapeDtypeStruct((M, N), out_dtype),
        compiler_params=_params(("parallel", "parallel"), 48),
        name=name,
    )(a, b)


def _mm3_kernel(a1_ref, a2_ref, a3_ref, b1_ref, b2_ref, b3_ref, o_ref):
    acc = jnp.dot(a1_ref[...], b1_ref[0], preferred_element_type=_F32)
    acc = acc + jnp.dot(a2_ref[...], b2_ref[0], preferred_element_type=_F32)
    acc = acc + jnp.dot(a3_ref[...], b3_ref[0], preferred_element_type=_F32)
    o_ref[...] = acc


def _out_proj(att, fmix, cmix, w_out, l):
    M = att.shape[0]
    k1, k2, k3 = att.shape[1], fmix.shape[1], cmix.shape[1]
    N = w_out.shape[2]
    assert k2 == k3 and k1 % k2 == 0
    tm = _tile(M, (1088, 768, 512, 256))
    tn = 512
    return pl.pallas_call(
        _mm3_kernel,
        grid=(M // tm, N // tn),
        in_specs=[pl.BlockSpec((tm, k1), lambda i, j: (i, 0)),
                  pl.BlockSpec((tm, k2), lambda i, j: (i, 0)),
                  pl.BlockSpec((tm, k3), lambda i, j: (i, 0)),
                  pl.BlockSpec((1, k1, tn), lambda i, j: (l, 0, j)),
                  pl.BlockSpec((1, k2, tn), lambda i, j: (l, k1 // k2, j)),
                  pl.BlockSpec((1, k3, tn), lambda i, j: (l, k1 // k2 + 1, j))],
        out_specs=pl.BlockSpec((tm, tn), lambda i, j: (i, j)),
        out_shape=jax.ShapeDtypeStruct((M, N), _F32),
        compiler_params=_params(("parallel", "parallel"), 48),
        name="out_proj",
    )(att, fmix, cmix, w_out, w_out, w_out)


def _rms_bf16(x, g):
    return (x * lax.rsqrt(jnp.mean(x * x, axis=-1, keepdims=True) + EPS) * g).astype(_BF)


def _rope_pair(t):
    return t + pltpu.roll(t, HEAD_ROPE, 1)


def _qproj_kernel(cq_ref, g_ref, w_ref, cs_ref, o_ref, *, heads, scale):
    xn = _rms_bf16(cq_ref[...], g_ref[...])
    y = jnp.dot(xn, w_ref[0], preferred_element_type=_F32)
    cs = cs_ref[...]
    for h in range(heads):
        lo = h * HEAD_QK
        o_ref[:, lo:lo + HEAD_V] = (y[:, lo:lo + HEAD_V] * scale).astype(_BF)
        t = y[:, lo + HEAD_V:lo + HEAD_QK] * cs
        o_ref[:, lo + HEAD_V:lo + HEAD_QK] = (_rope_pair(t) * scale).astype(_BF)


def _q_proj(y_in, g_q, w_uq_p, l, cs, q_rank, scale):
    T = y_in.shape[0]
    N = w_uq_p.shape[2]
    tm = _tile(T, (512, 256))
    hb = 4
    tn = hb * HEAD_QK
    return pl.pallas_call(
        functools.partial(_qproj_kernel, heads=hb, scale=scale),
        grid=(T // tm, N // tn),
        in_specs=[pl.BlockSpec((tm, q_rank), lambda i, j: (i, 0)),
                  pl.BlockSpec((1, q_rank), lambda i, j: (0, 0)),
                  pl.BlockSpec((1, q_rank, tn), lambda i, j: (l, 0, j)),
                  pl.BlockSpec((tm, LANES), lambda i, j: (i, 0))],
        out_specs=pl.BlockSpec((tm, tn), lambda i, j: (i, j)),
        out_shape=jax.ShapeDtypeStruct((T, N), _BF),
        compiler_params=_params(("parallel", "parallel"), 32),
        name="q_proj",
    )(y_in, g_q.reshape(1, q_rank), w_uq_p, cs)


def _kvproj_kernel(ckv_ref, g_ref, wk_ref, wv_ref, kr_ref, cs_ref, k_ref, v_ref, *, heads):
    xn = _rms_bf16(ckv_ref[...], g_ref[...])
    kn = jnp.dot(xn, wk_ref[0], preferred_element_type=_F32)
    v = jnp.dot(xn, wv_ref[0], preferred_element_type=_F32)
    t = kr_ref[...] * cs_ref[...]
    lane = lax.broadcasted_iota(jnp.int32, t.shape, 1)
    krf = jnp.where(lane < HEAD_ROPE, _rope_pair(t), 0.0).astype(_BF)
    ones = jnp.ones((t.shape[0], HEAD_V), _BF)
    for h in range(heads):
        k_ref[:, h * HEAD_QK:h * HEAD_QK + HEAD_V] = kn[:, h * HEAD_V:(h + 1) * HEAD_V].astype(_BF)
        k_ref[:, h * HEAD_QK + HEAD_V:(h + 1) * HEAD_QK] = krf
        v_ref[:, h * HEAD_QK:h * HEAD_QK + HEAD_V] = v[:, h * HEAD_V:(h + 1) * HEAD_V].astype(_BF)
        v_ref[:, h * HEAD_QK + HEAD_V:(h + 1) * HEAD_QK] = ones


def _kv_proj(y_in, g_kv, w_ukv_p, l, cs, kv_rank, ckv_off, kr_off, n_heads):
    T = y_in.shape[0]
    tm = _tile(T, (512, 256))
    hb = 4
    nb = n_heads // hb
    return pl.pallas_call(
        functools.partial(_kvproj_kernel, heads=hb),
        grid=(T // tm, nb),
        in_specs=[pl.BlockSpec((tm, kv_rank), lambda i, j: (i, ckv_off // kv_rank)),
                  pl.BlockSpec((1, kv_rank), lambda i, j: (0, 0)),
                  pl.BlockSpec((1, kv_rank, hb * HEAD_V), lambda i, j: (l, 0, j)),
                  pl.BlockSpec((1, kv_rank, hb * HEAD_V), lambda i, j: (l, 0, nb + j)),
                  pl.BlockSpec((tm, LANES), lambda i, j: (i, kr_off // LANES)),
                  pl.BlockSpec((tm, LANES), lambda i, j: (i, 0))],
        out_specs=[pl.BlockSpec((tm, hb * HEAD_QK), lambda i, j: (i, j)),
                   pl.BlockSpec((tm, hb * HEAD_QK), lambda i, j: (i, j))],
        out_shape=[jax.ShapeDtypeStruct((T, n_heads * HEAD_QK), _BF),
                   jax.ShapeDtypeStruct((T, n_heads * HEAD_QK), _BF)],
        compiler_params=_params(("parallel", "parallel"), 32),
        name="kv_proj",
    )(y_in, g_kv.reshape(1, kv_rank), w_ukv_p, w_ukv_p, y_in, cs)


_NT = (((1,), (1,)), ((), ()))


def _softmax_step(q, k, v1, m, acc):
    s = lax.dot_general(q, k, _NT, preferred_element_type=_F32)
    s_max = jnp.max(s, axis=-1, keepdims=True)
    m_new = s_max if m is None else jnp.maximum(m, s_max)
    p = jnp.exp(s - m_new).astype(_BF)
    pv = jnp.dot(p, v1, preferred_element_type=_F32)
    if acc is not None:
        pv = jnp.exp(m - m_new) * acc + pv
    return m_new, pv


def _attn_finish(acc, o_ref):
    o_ref[...] = (acc[:, :HEAD_V] / acc[:, HEAD_V:]).astype(_BF)


def _attn_lat_kernel(q_ref, kc_ref, vc_ref, kl_ref, vl_ref, o_ref, *, tk, n_chunks):
    q = q_ref[...]
    m, acc = _softmax_step(q, kc_ref[...], vc_ref[...], None, None)
    for c in range(n_chunks):
        m, acc = _softmax_step(q, kl_ref[c * tk:(c + 1) * tk, :], vl_ref[c * tk:(c + 1) * tk, :], m, acc)
    _attn_finish(acc, o_ref)


def _attn_ctx_kernel(q_ref, kc_ref, vc_ref, o_ref):
    _, acc = _softmax_step(q_ref[...], kc_ref[...], vc_ref[...], None, None)
    _attn_finish(acc, o_ref)


def _attention(q, k, v1, n_batch, seq, ctx, n_heads):
    tq = _tile(seq, (512, 256))
    tk = _tile(seq, (512, 256))
    nlt = seq // tq
    ctx_blk0 = n_batch * seq // ctx
    att_lat = pl.pallas_call(
        functools.partial(_attn_lat_kernel, tk=tk, n_chunks=seq // tk),
        grid=(n_batch, n_heads, nlt),
        in_specs=[pl.BlockSpec((tq, HEAD_QK), lambda b, h, i: (b * nlt + i, h)),
                  pl.BlockSpec((ctx, HEAD_QK), lambda b, h, i: (ctx_blk0 + b, h)),
                  pl.BlockSpec((ctx, HEAD_QK), lambda b, h, i: (ctx_blk0 + b, h)),
                  pl.BlockSpec((seq, HEAD_QK), lambda b, h, i: (b, h)),
                  pl.BlockSpec((seq, HEAD_QK), lambda b, h, i: (b, h))],
        out_specs=pl.BlockSpec((tq, HEAD_V), lambda b, h, i: (b * nlt + i, h)),
        out_shape=jax.ShapeDtypeStruct((n_batch * seq, n_heads * HEAD_V), _BF),
        compiler_params=_params(("parallel", "parallel", "arbitrary"), 48),
        name="attention",
    )(q, k, v1, k, v1)
    att_ctx = pl.pallas_call(
        _attn_ctx_kernel,
        grid=(n_batch, n_heads),
        in_specs=[pl.BlockSpec((ctx, HEAD_QK), lambda b, h: (ctx_blk0 + b, h)),
                  pl.BlockSpec((ctx, HEAD_QK), lambda b, h: (ctx_blk0 + b, h)),
                  pl.BlockSpec((ctx, HEAD_QK), lambda b, h: (ctx_blk0 + b, h))],
        out_specs=pl.BlockSpec((ctx, HEAD_V), lambda b, h: (b, h)),
        out_shape=jax.ShapeDtypeStruct((n_batch * ctx, n_heads * HEAD_V), _BF),
        compiler_params=_params(("parallel", "parallel"), 32),
        name="attention_ctx",
    )(q, k, v1)
    return jnp.concatenate([att_lat, att_ctx], axis=0)


def _dft_mats(n):
    j = jnp.arange(n, dtype=jnp.int32)
    jk = (j[:, None] * j[None, :]) % n
    ang = jk.astype(_F32) * (2.0 * math.pi / n)
    return jnp.cos(ang).astype(_BF), jnp.sin(ang).astype(_BF)


def _dft_ch_kernel(f_ref, cs_ref, pc_ref, ps_ref, *, fd):
    y = jnp.dot(f_ref[...].astype(_BF), cs_ref[...], preferred_element_type=_F32)
    pc_ref[...] = y[:, :fd].astype(_BF)
    ps_ref[...] = y[:, fd:].astype(_BF)


def _dft_channels(y_in, cs_mat, f_off, fd):
    T = y_in.shape[0]
    tm = _tile(T, (512, 256))
    out = jax.ShapeDtypeStruct((T, F_GROUPS * fd), _BF)
    return pl.pallas_call(
        functools.partial(_dft_ch_kernel, fd=fd),
        grid=(T // tm, F_GROUPS),
        in_specs=[pl.BlockSpec((tm, fd), lambda i, g: (i, f_off // fd + g)),
                  pl.BlockSpec((fd, 2 * fd), lambda i, g: (0, 0))],
        out_specs=[pl.BlockSpec((tm, fd), lambda i, g: (i, g)), pl.BlockSpec((tm, fd), lambda i, g: (i, g))],
        out_shape=[out, out],
        compiler_params=_params(("parallel", "parallel"), 32),
        name="dft_channels",
    )(y_in, cs_mat)


def _dft_pos_kernel(c_ref, s_ref, pc_ref, ps_ref, wf_ref, o_ref, *, norm, fd, groups):
    acc = jnp.dot(c_ref[...], pc_ref[...], preferred_element_type=_F32)
    acc = acc - jnp.dot(s_ref[...], ps_ref[...], preferred_element_type=_F32)
    fr = (acc * norm).astype(_BF)
    for g in range(groups):
        o_ref[:, g * fd:(g + 1) * fd] = jnp.dot(fr[:, g * fd:(g + 1) * fd], wf_ref[g],
                                                preferred_element_type=_F32).astype(_BF)


def _dft_positions(pc, ps, cmat, smat, w_f, n_batch, n, row0, fd):
    tm = min(512, n)
    gb = 2
    tn = gb * fd
    blk0 = row0 // n
    return pl.pallas_call(
        functools.partial(_dft_pos_kernel, norm=1.0 / math.sqrt(n * fd), fd=fd, groups=gb),
        grid=(n_batch, F_GROUPS // gb, n // tm),
        in_specs=[pl.BlockSpec((tm, n), lambda b, j, m: (m, 0)),
                  pl.BlockSpec((tm, n), lambda b, j, m: (m, 0)),
                  pl.BlockSpec((n, tn), lambda b, j, m: (blk0 + b, j)),
                  pl.BlockSpec((n, tn), lambda b, j, m: (blk0 + b, j)),
                  pl.BlockSpec((gb, fd, fd), lambda b, j, m: (j, 0, 0))],
        out_specs=pl.BlockSpec((tm, tn), lambda b, j, m: (b * (n // tm) + m, j)),
        out_shape=jax.ShapeDtypeStruct((n_batch * n, F_GROUPS * fd), _BF),
        compiler_params=_params(("parallel", "parallel", "arbitrary"), 48),
        name=f"dft_positions_{n}",
    )(cmat, smat, pc, ps, w_f)


def _chunk_kernel(u_ref, v_ref, ws_ref, bs_ref, o_ref, *, heads, cdim, n_chunks):
    u = jax.nn.gelu(u_ref[...])
    v = jax.nn.gelu(v_ref[...])
    for h in range(heads):
        vh = v[:, h * cdim:(h + 1) * cdim]
        mu = jnp.mean(vh, axis=-1, keepdims=True)
        d = vh - mu
        var = jnp.mean(d * d, axis=-1, keepdims=True)
        vs = (d * lax.rsqrt(var + EPS)).astype(_BF)
        w = ws_ref[h]
        bias = bs_ref[h]
        for c in range(n_chunks):
            rows = slice(c * CHUNK, (c + 1) * CHUNK)
            sv = jnp.dot(w, vs[rows], preferred_element_type=_F32) + bias
            o_ref[rows, h * cdim:(h + 1) * cdim] = (u[rows, h * cdim:(h + 1) * cdim] * sv).astype(_BF)


def _chunk_mix(y_in, w_s, b_s, u_off, width):
    T = y_in.shape[0]
    tm = _tile(T, (512, 256))
    cdim = width // C_HEADS
    return pl.pallas_call(
        functools.partial(_chunk_kernel, heads=C_HEADS, cdim=cdim, n_chunks=tm // CHUNK),
        grid=(T // tm,),
        in_specs=[pl.BlockSpec((tm, width), lambda i: (i, u_off // width)),
                  pl.BlockSpec((tm, width), lambda i: (i, u_off // width + 1)),
                  pl.BlockSpec((C_HEADS, CHUNK, CHUNK), lambda i: (0, 0, 0)),
                  pl.BlockSpec((C_HEADS, CHUNK, 1), lambda i: (0, 0, 0))],
        out_specs=pl.BlockSpec((tm, width), lambda i: (i, 0)),
        out_shape=jax.ShapeDtypeStruct((T, width), _BF),
        compiler_params=_params(("parallel",), 32),
        name="chunk_mix",
    )(y_in, y_in, w_s, b_s.reshape(C_HEADS, CHUNK, 1))


_R_GIDX = N_GROUPS


def _router_kernel(h_ref, w_ref, b_ref, r_ref):
    lg = jnp.dot(h_ref[...], w_ref[0], preferred_element_type=_F32) + b_ref[0]
    lane = lax.broadcasted_iota(jnp.int32, lg.shape, 1)

    def col(j):
        return jnp.sum(jnp.where(lane == j, lg, 0.0), axis=-1, keepdims=True)

    g = [col(j) for j in range(N_GROUPS)]
    gmax = functools.reduce(jnp.maximum, g)
    gi = jnp.full(gmax.shape, N_GROUPS - 1, jnp.int32)
    for j in reversed(range(N_GROUPS - 1)):
        gi = jnp.where(g[j] == gmax, j, gi)
    gw = 1.0 / functools.reduce(lambda a, b: a + b, [jnp.exp(x - gmax) for x in g])

    e = []
    for j in range(EXP_PER_GROUP):
        ej = col(N_GROUPS + j)
        for gg in range(1, N_GROUPS):
            ej = jnp.where(gi == gg, col(N_GROUPS + gg * EXP_PER_GROUP + j), ej)
        e.append(ej)
    v1 = functools.reduce(jnp.maximum, e)
    i1 = jnp.full(v1.shape, EXP_PER_GROUP - 1, jnp.int32)
    for j in reversed(range(EXP_PER_GROUP - 1)):
        i1 = jnp.where(e[j] == v1, j, i1)
    e2 = [jnp.where(i1 == j, -jnp.inf, e[j]) for j in range(EXP_PER_GROUP)]
    v2 = functools.reduce(jnp.maximum, e2)
    i2 = jnp.full(v2.shape, EXP_PER_GROUP - 1, jnp.int32)
    for j in reversed(range(EXP_PER_GROUP - 1)):
        i2 = jnp.where(e2[j] == v2, j, i2)
    t = jnp.exp(v2 - v1)
    w1 = (1.0 / (1.0 + t)) * gw
    w2 = (t / (1.0 + t)) * gw

    rec = jnp.where(lane == _R_GIDX, gi.astype(_F32), 0.0)
    for j in range(EXP_PER_GROUP):
        cw = jnp.where(i1 == j, w1, 0.0) + jnp.where(i2 == j, w2, 0.0)
        rec = rec + jnp.where(lane == j, cw, 0.0)
    r_ref[...] = rec


def _router(h2, w_r, b_r, l):
    T, D = h2.shape
    tm = _tile(T, (512, 256))
    return pl.pallas_call(
        _router_kernel,
        grid=(T // tm,),
        in_specs=[pl.BlockSpec((tm, D), lambda i: (i, 0)),
                  pl.BlockSpec((1, D, LANES), lambda i: (l, 0, 0)),
                  pl.BlockSpec((1, 1, LANES), lambda i: (l, 0, 0))],
        out_specs=pl.BlockSpec((tm, LANES), lambda i: (i, 0)),
        out_shape=jax.ShapeDtypeStruct((T, LANES), _F32),
        compiler_params=_params(("parallel",), 32),
        name="router",
    )(h2, w_r, b_r)


def _route_plan(rec, tm):
    T = rec.shape[0]
    n_tiles = T // tm + N_GROUPS
    gidx = rec[:, _R_GIDX].astype(jnp.int32)
    onehot = (gidx[:, None] == jnp.arange(N_GROUPS, dtype=jnp.int32)[None, :]).astype(jnp.int32)
    rank = jnp.sum((jnp.cumsum(onehot, axis=0) - onehot) * onehot, axis=1)
    counts = jnp.sum(onehot, axis=0)
    tiles_g = (counts + tm - 1) // tm
    tile_end = jnp.cumsum(tiles_g)
    start = (tile_end - tiles_g) * tm
    dest = (jnp.sum(onehot * start[None, :], axis=1) + rank).astype(jnp.int32)
    src = jnp.zeros((n_tiles * tm,), jnp.int32).at[dest].set(jnp.arange(T, dtype=jnp.int32))
    tile_id = jnp.arange(n_tiles, dtype=jnp.int32)
    tile_group = jnp.minimum(jnp.sum((tile_id[:, None] >= tile_end[None, :]).astype(jnp.int32), axis=1), N_GROUPS - 1)
    tile_valid = (tile_id < tile_end[-1]).astype(jnp.int32)
    return dest, src, tile_group.astype(jnp.int32), tile_valid


def _moe_gather_kernel(idx_ref, hp_hbm, xs_ref, buf, sem, *, tm):
    _row_gather(hp_hbm, idx_ref, pl.program_id(0) * tm, tm, buf, sem)
    lo, hi = _unpack_halves(buf[...])
    half = lo.shape[1]
    xs_ref[:, :half] = lo
    xs_ref[:, half:] = hi


def _moe_gather(hp, src):
    T, half = hp.shape
    P = src.shape[0]
    tm = 256
    return pl.pallas_call(
        functools.partial(_moe_gather_kernel, tm=tm),
        grid_spec=pltpu.PrefetchScalarGridSpec(
            num_scalar_prefetch=1,
            grid=(P // tm,),
            in_specs=[pl.BlockSpec(memory_space=pl.ANY)],
            out_specs=pl.BlockSpec((tm, 2 * half), lambda i, idx: (i, 0)),
            scratch_shapes=[pltpu.VMEM((tm, half), jnp.uint32), pltpu.SemaphoreType.DMA(())]),
        out_shape=jax.ShapeDtypeStruct((P, 2 * half), _BF),
        compiler_params=_params(("arbitrary",), 32),
        name="moe_gather",
    )(src, hp)


def _expert_kernel(tg_ref, tv_ref, xs_ref, rs_ref, wg_ref, wu_ref, wd_ref, o_ref):
    j, e = pl.program_id(0), pl.program_id(1)

    @pl.when(tv_ref[j] == 1)
    def _():
        x = xs_ref[...]
        gate = jnp.dot(x, wg_ref[0, 0], preferred_element_type=_F32)
        up = jnp.dot(x, wu_ref[0, 0], preferred_element_type=_F32)
        rec = rs_ref[...]
        cw = rec[:, 0:1]
        for k in range(1, EXP_PER_GROUP):
            cw = jnp.where(e == k, rec[:, k:k + 1], cw)
        hid = (jax.nn.silu(gate) * up) * cw
        y = jnp.dot(hid.astype(_BF), wd_ref[0, 0], preferred_element_type=_F32)

        @pl.when(e == 0)
        def _():
            o_ref[...] = y

        @pl.when(e > 0)
        def _():
            o_ref[...] += y

    @pl.when((tv_ref[j] == 0) & (e == 0))
    def _():
        o_ref[...] = jnp.zeros(o_ref.shape, o_ref.dtype)


def _experts(xs, rs, tile_group, tile_valid, w_gate, w_up, w_down, l, tm):
    P, D = xs.shape
    F = w_gate.shape[3]

    def expert(j, e, tg, tv):
        return tg[j] * EXP_PER_GROUP + e

    return pl.pallas_call(
        _expert_kernel,
        grid_spec=pltpu.PrefetchScalarGridSpec(
            num_scalar_prefetch=2,
            grid=(P // tm, EXP_PER_GROUP),
            in_specs=[pl.BlockSpec((tm, D), lambda j, e, tg, tv: (j, 0)),
                      pl.BlockSpec((tm, LANES), lambda j, e, tg, tv: (j, 0)),
                      pl.BlockSpec((1, 1, D, F), lambda j, e, tg, tv: (l, expert(j, e, tg, tv), 0, 0)),
                      pl.BlockSpec((1, 1, D, F), lambda j, e, tg, tv: (l, expert(j, e, tg, tv), 0, 0)),
                      pl.BlockSpec((1, 1, F, D), lambda j, e, tg, tv: (l, expert(j, e, tg, tv), 0, 0))],
            out_specs=pl.BlockSpec((tm, D), lambda j, e, tg, tv: (j, 0))),
        out_shape=jax.ShapeDtypeStruct((P, D), _F32),
        compiler_params=_params(("parallel", "arbitrary"), 58),
        name="experts",
    )(tile_group, tile_valid, xs, rs, w_gate, w_up, w_down)


def _rot_cols(w):
    q = HEAD_ROPE // 4
    a, b, c, d = w[..., :q], w[..., q:2 * q], w[..., 2 * q:3 * q], w[..., 3 * q:]
    return jnp.concatenate([-b, a, -d, c], axis=-1)


def _prep_w_in(w_in, D):
    q, kv = D // 4, D // 8
    o1, o2, o3 = q, q + kv, q + kv + HEAD_ROPE
    cq, ckv, kr, rest = w_in[..., :o1], w_in[..., o1:o2], w_in[..., o2:o3], w_in[..., o3:]
    cols = [cq, rest, ckv, kr, _rot_cols(kr)]
    width = sum(c.shape[-1] for c in cols)
    pad = (-width) % 512
    cols.append(jnp.zeros(w_in.shape[:-1] + (pad,), w_in.dtype))
    return jnp.concatenate(cols, axis=-1).astype(_BF)


def _prep_w_uq(w_uq, n_heads):
    L, r, _ = w_uq.shape
    w = w_uq.reshape(L, r, n_heads, HEAD_V + HEAD_ROPE)
    rope = w[..., HEAD_V:]
    return jnp.concatenate([w[..., :HEAD_V], rope, _rot_cols(rope)], axis=-1).reshape(L, r, n_heads * HEAD_QK).astype(_BF)


def _prep_w_ukv(w_ukv, n_heads):
    L, r, _ = w_ukv.shape
    w = w_ukv.reshape(L, r, n_heads, 2 * HEAD_V)
    return jnp.concatenate([w[..., :HEAD_V].reshape(L, r, -1), w[..., HEAD_V:].reshape(L, r, -1)], axis=-1).astype(_BF)


def _rope_table(n_batch, seq, ctx):
    half = HEAD_ROPE // 2
    inv_freq = ROPE_THETA ** (-jnp.arange(0, half, 2, dtype=_F32) / half)
    pos = jnp.arange(seq)
    ang_r = (pos // GRID_W).astype(_F32)[:, None] * inv_freq
    ang_c = (pos % GRID_W).astype(_F32)[:, None] * inv_freq
    ang = jnp.concatenate([ang_r, ang_r, ang_c, ang_c], axis=-1)
    lat = jnp.concatenate([jnp.cos(ang), jnp.sin(ang)], axis=-1)
    ctx_rows = jnp.concatenate([jnp.ones((n_batch * ctx, HEAD_ROPE), _F32), jnp.zeros((n_batch * ctx, HEAD_ROPE), _F32)], axis=-1)
    return jnp.concatenate([jnp.tile(lat, (n_batch, 1)), ctx_rows], axis=0)


def kernel(x, c, ctx, c_ctx, w_ada, b_ada, w_in, q_norm_g, kv_norm_g, w_uq, w_ukv, w_fourier, w_spatial, b_spatial, w_out, ln1_g, ln1_b, w_router_group, b_router_group, w_router_expert, b_router_expert, w_gate, w_up, w_down, ln2_g, ln2_b):
    B, SEQ, D = x.shape
    CTX = ctx.shape[1]
    L = w_ada.shape[0]
    n_heads = (D // 2) // HEAD_V
    q_rank, kv_rank = D // 4, D // 8
    fd = D // 16
    alpha = (2.0 * L) ** 0.25
    n_seg = B + 1
    assert n_seg <= MOD_ROWS and SEQ % 256 == 0 and (B * CTX) % 256 == 0 and CTX == 256 and (B * SEQ) % CTX == 0
    moe_tm = 512

    f_off, u_off, ckv_off, kr_off = q_rank, 2 * q_rank, D, D + kv_rank

    w_in_p = _prep_w_in(w_in, D)
    w_uq_p = _prep_w_uq(w_uq, n_heads)
    w_ukv_p = _prep_w_ukv(w_ukv, n_heads)
    w_out_b = w_out.astype(_BF)
    w_f_b = w_fourier.astype(_BF)
    w_s_b = w_spatial.astype(_BF)
    w_gate_b = w_gate.astype(_BF)
    w_up_b = w_up.astype(_BF)
    w_down_b = w_down.astype(_BF)
    n_logits = N_GROUPS + N_GROUPS * EXP_PER_GROUP
    w_r = jnp.concatenate([w_router_group, w_router_expert, jnp.zeros((L, D, LANES - n_logits), _F32)], axis=-1).astype(_BF)
    b_r = jnp.concatenate([b_router_group, b_router_expert, jnp.zeros((L, LANES - n_logits), _F32)], axis=-1).reshape(L, 1, LANES)
    cs_tab = _rope_table(B, SEQ, CTX)
    c_lat, s_lat = _dft_mats(SEQ)
    c_ctx_m, s_ctx_m = _dft_mats(CTX)
    c_ch, s_ch = _dft_mats(fd)
    cs_ch = jnp.concatenate([c_ch, s_ch], axis=1)

    c_rows = jnp.concatenate([c, c_ctx[None, :], jnp.zeros((MOD_ROWS - n_seg, D), _F32)], axis=0)
    mod_all = _adaln(c_rows, w_ada, b_ada)

    xt = jnp.concatenate([x.reshape(B * SEQ, D), ctx.reshape(B * CTX, D)], axis=0)
    mod3 = [mod_all[l].reshape(MOD_ROWS * 6, 1, D) for l in range(L)]
    h = _modulate(xt, mod3[0], 1, 0, SEQ, B)

    for l in range(L):
        last = l == L - 1
        y_in = _matmul(h, w_in_p, l, _F32, "in_proj")
        q = _q_proj(y_in, q_norm_g[l], w_uq_p, l, cs_tab, q_rank, 1.0 / math.sqrt(HEAD_V + HEAD_ROPE))
        k, v1 = _kv_proj(y_in, kv_norm_g[l], w_ukv_p, l, cs_tab, kv_rank, ckv_off, kr_off, n_heads)
        att = _attention(q, k, v1, B, SEQ, CTX, n_heads)
        pc, ps = _dft_channels(y_in, cs_ch, f_off, fd)
        fmix = jnp.concatenate([
            _dft_positions(pc, ps, c_lat, s_lat, w_f_b[l], B, SEQ, 0, fd),
            _dft_positions(pc, ps, c_ctx_m, s_ctx_m, w_f_b[l], B, CTX, B * SEQ, fd)], axis=0)
        cmix = _chunk_mix(y_in, w_s_b[l], b_spatial[l], u_off, q_rank)
        mixed = _out_proj(att, fmix, cmix, w_out_b, l)
        x1, h2, h2p = _resid_ln(xt, mixed, mod3[l], 2, ln1_g[l], ln1_b[l], 4, 3, SEQ, B, alpha)

        rec = _router(h2, w_r, b_r, l)
        dest, src, tile_group, tile_valid = _route_plan(rec, moe_tm)
        xs = _moe_gather(h2p, src)
        ys = _experts(xs, rec[src], tile_group, tile_valid, w_gate_b, w_up_b, w_down_b, l, moe_tm)
        xt, h = _resid_ln_gather(x1, ys, dest, mod3[l], 5, ln2_g[l], ln2_b[l], 1, 0, SEQ, B, alpha,
                                 emit_h=not last, mod_next=mod3[min(l + 1, L - 1)])

    return xt[:B * SEQ].reshape(B, SEQ, D)
```

```python
import functools
import math

import jax
import jax.numpy as jnp
from jax import lax
from jax.experimental import pallas as pl
from jax.experimental.pallas import tpu as pltpu

_BF = jnp.bfloat16
_F32 = jnp.float32

ROPE_THETA = 10000.0
EPS = 1e-6
GRID_W = 64
HEAD_V = 128
HEAD_ROPE = 64
HEAD_QK = 256
F_GROUPS = 4
C_HEADS = 4
CHUNK = 128
N_GROUPS = 4
EXP_PER_GROUP = 4
LANES = 128
MOD_ROWS = 8

_MIB = 1024 * 1024


def _params(sem, vmem_mib):
    return pltpu.CompilerParams(dimension_semantics=sem, vmem_limit_bytes=vmem_mib * _MIB)


def _tile(n, prefs):
    for t in prefs:
        if n % t == 0:
            return t
    raise ValueError(f"no tile in {prefs} divides {n}")


def _adaln_kernel(c_ref, w_ref, b_ref, o_ref):
    s = jax.nn.silu(c_ref[...]).astype(_BF)
    o_ref[0] = jnp.dot(s, w_ref[0].astype(_BF), preferred_element_type=_F32) + b_ref[0]


def _adaln(c_rows, w_ada, b_ada):
    L, D, N = w_ada.shape
    tn = 512
    return pl.pallas_call(
        _adaln_kernel,
        grid=(L, N // tn),
        in_specs=[pl.BlockSpec((MOD_ROWS, D), lambda l, j: (0, 0)),
                  pl.BlockSpec((1, D, tn), lambda l, j: (l, 0, j)),
                  pl.BlockSpec((1, 1, tn), lambda l, j: (l, 0, j))],
        out_specs=pl.BlockSpec((1, MOD_ROWS, tn), lambda l, j: (l, 0, j)),
        out_shape=jax.ShapeDtypeStruct((L, MOD_ROWS, N), _F32),
        compiler_params=_params(("parallel", "parallel"), 40),
        name="adaln",
    )(c_rows, w_ada, b_ada.reshape(L, 1, N))


def _mod_spec(chunk, D, tiles_per_seq, n_batch):
    return pl.BlockSpec((1, 1, D), lambda i, *_: (jnp.minimum(i // tiles_per_seq, n_batch) * 6 + chunk, 0, 0))


def _modulate_kernel(x_ref, sc_ref, sh_ref, o_ref):
    o_ref[...] = (x_ref[...] * (1.0 + sc_ref[0]) + sh_ref[0]).astype(_BF)


def _modulate(x, mod, sc_chunk, sh_chunk, seq, n_batch):
    T, D = x.shape
    tm = _tile(seq, (256,))
    tps = seq // tm
    return pl.pallas_call(
        _modulate_kernel,
        grid=(T // tm,),
        in_specs=[pl.BlockSpec((tm, D), lambda i: (i, 0)),
                  _mod_spec(sc_chunk, D, tps, n_batch), _mod_spec(sh_chunk, D, tps, n_batch)],
        out_specs=pl.BlockSpec((tm, D), lambda i: (i, 0)),
        out_shape=jax.ShapeDtypeStruct((T, D), _BF),
        compiler_params=_params(("parallel",), 32),
        name="modulate",
    )(x, mod, mod)


def _pack_halves(hb):
    half = hb.shape[1] // 2
    bits = pltpu.bitcast(hb.astype(_F32), jnp.uint32)
    return (bits[:, half:] & jnp.uint32(0xFFFF0000)) | (bits[:, :half] >> 16)


def _unpack_halves(u):
    lo = pltpu.bitcast(u << 16, _F32).astype(_BF)
    hi = pltpu.bitcast(u & jnp.uint32(0xFFFF0000), _F32).astype(_BF)
    return lo, hi


def _ln_tail(y, g_ref, b_ref, sc_ref, sh_ref, xo_ref, ho_ref):
    mu = jnp.mean(y, axis=-1, keepdims=True)
    d = y - mu
    var = jnp.mean(d * d, axis=-1, keepdims=True)
    xn = d * lax.rsqrt(var + EPS) * g_ref[...] + b_ref[...]
    xo_ref[...] = xn
    if sc_ref is None:
        return None
    hb = (xn * (1.0 + sc_ref[0]) + sh_ref[0]).astype(_BF)
    if ho_ref is not None:
        ho_ref[...] = hb
    return hb


def _resid_ln_route_kernel(x_ref, a_ref, gate_ref, g_ref, b_ref, sc_ref, sh_ref, wr_ref, br_ref,
                           xo_ref, hp_ref, r_ref, *, alpha):
    y = alpha * x_ref[...] + gate_ref[0] * a_ref[...]
    hb = _ln_tail(y, g_ref, b_ref, sc_ref, sh_ref, xo_ref, None)
    hp_ref[...] = _pack_halves(hb)
    r_ref[...] = _route(jnp.dot(hb, wr_ref[0], preferred_element_type=_F32) + br_ref[0])


def _resid_ln_route(x, a, mod, gate_chunk, ln_g, ln_b, sc_chunk, sh_chunk, w_r, b_r, l, seq, n_batch, alpha):
    T, D = x.shape
    tm = _tile(seq, (256,))
    tps = seq // tm
    row = pl.BlockSpec((tm, D), lambda i: (i, 0))
    vec = pl.BlockSpec((1, D), lambda i: (0, 0))
    return pl.pallas_call(
        functools.partial(_resid_ln_route_kernel, alpha=alpha),
        grid=(T // tm,),
        in_specs=[row, row, _mod_spec(gate_chunk, D, tps, n_batch), vec, vec,
                  _mod_spec(sc_chunk, D, tps, n_batch), _mod_spec(sh_chunk, D, tps, n_batch),
                  pl.BlockSpec((1, D, LANES), lambda i: (l, 0, 0)),
                  pl.BlockSpec((1, 1, LANES), lambda i: (l, 0, 0))],
        out_specs=[row, pl.BlockSpec((tm, D // 2), lambda i: (i, 0)), pl.BlockSpec((tm, LANES), lambda i: (i, 0))],
        out_shape=[jax.ShapeDtypeStruct((T, D), _F32), jax.ShapeDtypeStruct((T, D // 2), jnp.uint32),
                   jax.ShapeDtypeStruct((T, LANES), _F32)],
        compiler_params=_params(("parallel",), 40),
        name="resid_ln_route",
    )(x, a, mod, ln_g.reshape(1, D), ln_b.reshape(1, D), mod, mod, w_r, b_r)


def _gather_tile(src_hbm, idx_ref, tile, tm, buf, sem, wait):
    slot = tile % 2

    def _step(r, carry):
        cp = pltpu.make_async_copy(src_hbm.at[pl.ds(idx_ref[tile * tm + r], 1)], buf.at[slot, pl.ds(r, 1)],
                                   sem.at[slot])
        if wait:
            cp.wait()
        else:
            cp.start()
        return carry

    lax.fori_loop(0, tm, _step, 0, unroll=8)


def _gather_pipelined(src_hbm, idx_ref, tile, n_tiles, tm, buf, sem):
    @pl.when(tile == 0)
    def _():
        _gather_tile(src_hbm, idx_ref, tile, tm, buf, sem, wait=False)

    _gather_tile(src_hbm, idx_ref, tile, tm, buf, sem, wait=True)

    @pl.when(tile + 1 < n_tiles)
    def _():
        _gather_tile(src_hbm, idx_ref, tile + 1, tm, buf, sem, wait=False)


def _resid_ln_gather_kernel(idx_ref, x_ref, ys_hbm, gate_ref, g_ref, b_ref, sc_ref, sh_ref, *rest, alpha, tm, emit_h):
    if emit_h:
        xo_ref, ho_ref, buf, sem = rest
    else:
        xo_ref, buf, sem = rest
        ho_ref = sc_ref = sh_ref = None
    i = pl.program_id(0)
    _gather_pipelined(ys_hbm, idx_ref, i, pl.num_programs(0), tm, buf, sem)
    y = alpha * x_ref[...] + gate_ref[0] * buf[i % 2]
    _ln_tail(y, g_ref, b_ref, sc_ref, sh_ref, xo_ref, ho_ref)


def _resid_ln_gather(x, ys, dest, mod, gate_chunk, ln_g, ln_b, sc_chunk, sh_chunk, seq, n_batch, alpha, emit_h, mod_next):
    T, D = x.shape
    tm = _tile(seq, (256,))
    tps = seq // tm
    row = pl.BlockSpec((tm, D), lambda i, idx: (i, 0))
    vec = pl.BlockSpec((1, D), lambda i, idx: (0, 0))
    out_specs = [row, row] if emit_h else [row]
    out_shape = [jax.ShapeDtypeStruct((T, D), _F32)] + ([jax.ShapeDtypeStruct((T, D), _BF)] if emit_h else [])
    res = pl.pallas_call(
        functools.partial(_resid_ln_gather_kernel, alpha=alpha, tm=tm, emit_h=emit_h),
        grid_spec=pltpu.PrefetchScalarGridSpec(
            num_scalar_prefetch=1,
            grid=(T // tm,),
            in_specs=[row, pl.BlockSpec(memory_space=pl.ANY), _mod_spec(gate_chunk, D, tps, n_batch), vec, vec,
                      _mod_spec(sc_chunk, D, tps, n_batch), _mod_spec(sh_chunk, D, tps, n_batch)],
            out_specs=out_specs,
            scratch_shapes=[pltpu.VMEM((2, tm, D), _F32), pltpu.SemaphoreType.DMA((2,))]),
        out_shape=out_shape,
        compiler_params=_params(("arbitrary",), 40),
        name="resid_ln_gather",
    )(dest, x, ys, mod, ln_g.reshape(1, D), ln_b.reshape(1, D), mod_next, mod_next)
    return res if emit_h else (res[0], None)


def _mm_kernel(a_ref, b_ref, o_ref):
    o_ref[...] = jnp.dot(a_ref[...], b_ref[0], preferred_element_type=_F32).astype(o_ref.dtype)


def _matmul(a, b, l, out_dtype, name):
    M, K = a.shape
    N = b.shape[2]
    tm = _tile(M, (1088, 768, 512, 256))
    tn = 512
    return pl.pallas_call(
        _mm_kernel,
        grid=(M // tm, N // tn),
        in_specs=[pl.BlockSpec((tm, K), lambda i, j: (i, 0)), pl.BlockSpec((1, K, tn), lambda i, j: (l, 0, j))],
        out_specs=pl.BlockSpec((tm, tn), lambda i, j: (i, j)),
        out_shape=jax.ShapeDtypeStruct((M, N), out_dtype),
        compiler_params=_params(("parallel", "parallel"), 48),
        name=name,
    )(a, b)


def _mm3_kernel(a1_ref, a2_ref, a3_ref, b1_ref, b2_ref, b3_ref, o_ref):
    acc = jnp.dot(a1_ref[...], b1_ref[0], preferred_element_type=_F32)
    acc = acc + jnp.dot(a2_ref[...], b2_ref[0], preferred_element_type=_F32)
    acc = acc + jnp.dot(a3_ref[...], b3_ref[0], preferred_element_type=_F32)
    o_ref[...] = acc


def _out_proj(att, fmix, cmix, w_out, l):
    M = att.shape[0]
    k1, k2, k3 = att.shape[1], fmix.shape[1], cmix.shape[1]
    N = w_out.shape[2]
    assert k2 == k3 and k1 % k2 == 0
    tm = _tile(M, (1088, 768, 512, 256))
    tn = 512
    return pl.pallas_call(
        _mm3_kernel,
        grid=(M // tm, N // tn),
        in_specs=[pl.BlockSpec((tm, k1), lambda i, j: (i, 0)),
                  pl.BlockSpec((tm, k2), lambda i, j: (i, 0)),
                  pl.BlockSpec((tm, k3), lambda i, j: (i, 0)),
                  pl.BlockSpec((1, k1, tn), lambda i, j: (l, 0, j)),
                  pl.BlockSpec((1, k2, tn), lambda i, j: (l, k1 // k2, j)),
                  pl.BlockSpec((1, k3, tn), lambda i, j: (l, k1 // k2 + 1, j))],
        out_specs=pl.BlockSpec((tm, tn), lambda i, j: (i, j)),
        out_shape=jax.ShapeDtypeStruct((M, N), _F32),
        compiler_params=_params(("parallel", "parallel"), 48),
        name="out_proj",
    )(att, fmix, cmix, w_out, w_out, w_out)


def _rms_bf16(x, g):
    return (x * lax.rsqrt(jnp.mean(x * x, axis=-1, keepdims=True) + EPS) * g).astype(_BF)


def _rope_pair(t):
    return t + pltpu.roll(t, HEAD_ROPE, 1)


def _qproj_kernel(cq_ref, g_ref, w_ref, cs_ref, o_ref, *, heads, scale):
    xn = _rms_bf16(cq_ref[...], g_ref[...])
    y = jnp.dot(xn, w_ref[0], preferred_element_type=_F32)
    cs = cs_ref[...]
    for h in range(heads):
        lo = h * HEAD_QK
        o_ref[:, lo:lo + HEAD_V] = (y[:, lo:lo + HEAD_V] * scale).astype(_BF)
        t = y[:, lo + HEAD_V:lo + HEAD_QK] * cs
        o_ref[:, lo + HEAD_V:lo + HEAD_QK] = (_rope_pair(t) * scale).astype(_BF)


def _q_proj(y_in, g_q, w_uq_p, l, cs, q_rank, scale):
    T = y_in.shape[0]
    N = w_uq_p.shape[2]
    tm = _tile(T, (512, 256))
    hb = 4
    tn = hb * HEAD_QK
    return pl.pallas_call(
        functools.partial(_qproj_kernel, heads=hb, scale=scale),
        grid=(T // tm, N // tn),
        in_specs=[pl.BlockSpec((tm, q_rank), lambda i, j: (i, 0)),
                  pl.BlockSpec((1, q_rank), lambda i, j: (0, 0)),
                  pl.BlockSpec((1, q_rank, tn), lambda i, j: (l, 0, j)),
                  pl.BlockSpec((tm, LANES), lambda i, j: (i, 0))],
        out_specs=pl.BlockSpec((tm, tn), lambda i, j: (i, j)),
        out_shape=jax.ShapeDtypeStruct((T, N), _BF),
        compiler_params=_params(("parallel", "parallel"), 32),
        name="q_proj",
    )(y_in, g_q.reshape(1, q_rank), w_uq_p, cs)


def _kvproj_kernel(ckv_ref, g_ref, wk_ref, wv_ref, kr_ref, cs_ref, k_ref, v_ref, *, heads):
    xn = _rms_bf16(ckv_ref[...], g_ref[...])
    kn = jnp.dot(xn, wk_ref[0], preferred_element_type=_F32)
    v = jnp.dot(xn, wv_ref[0], preferred_element_type=_F32)
    t = kr_ref[...] * cs_ref[...]
    lane = lax.broadcasted_iota(jnp.int32, t.shape, 1)
    krf = jnp.where(lane < HEAD_ROPE, _rope_pair(t), 0.0).astype(_BF)
    ones = jnp.ones((t.shape[0], HEAD_V), _BF)
    for h in range(heads):
        k_ref[:, h * HEAD_QK:h * HEAD_QK + HEAD_V] = kn[:, h * HEAD_V:(h + 1) * HEAD_V].astype(_BF)
        k_ref[:, h * HEAD_QK + HEAD_V:(h + 1) * HEAD_QK] = krf
        v_ref[:, h * HEAD_QK:h * HEAD_QK + HEAD_V] = v[:, h * HEAD_V:(h + 1) * HEAD_V].astype(_BF)
        v_ref[:, h * HEAD_QK + HEAD_V:(h + 1) * HEAD_QK] = ones


def _kv_proj(y_in, g_kv, w_ukv_p, l, cs, kv_rank, ckv_off, kr_off, n_heads):
    T = y_in.shape[0]
    tm = _tile(T, (512, 256))
    hb = 4
    nb = n_heads // hb
    return pl.pallas_call(
        functools.partial(_kvproj_kernel, heads=hb),
        grid=(T // tm, nb),
        in_specs=[pl.BlockSpec((tm, kv_rank), lambda i, j: (i, ckv_off // kv_rank)),
                  pl.BlockSpec((1, kv_rank), lambda i, j: (0, 0)),
                  pl.BlockSpec((1, kv_rank, hb * HEAD_V), lambda i, j: (l, 0, j)),
                  pl.BlockSpec((1, kv_rank, hb * HEAD_V), lambda i, j: (l, 0, nb + j)),
                  pl.BlockSpec((tm, LANES), lambda i, j: (i, kr_off // LANES)),
                  pl.BlockSpec((tm, LANES), lambda i, j: (i, 0))],
        out_specs=[pl.BlockSpec((tm, hb * HEAD_QK), lambda i, j: (i, j)),
                   pl.BlockSpec((tm, hb * HEAD_QK), lambda i, j: (i, j))],
        out_shape=[jax.ShapeDtypeStruct((T, n_heads * HEAD_QK), _BF),
                   jax.ShapeDtypeStruct((T, n_heads * HEAD_QK), _BF)],
        compiler_params=_params(("parallel", "parallel"), 32),
        name="kv_proj",
    )(y_in, g_kv.reshape(1, kv_rank), w_ukv_p, w_ukv_p, y_in, cs)


_NT = (((1,), (1,)), ((), ()))


def _softmax_step(q, k, v1, m, acc):
    s = lax.dot_general(q, k, _NT, preferred_element_type=_F32)
    s_max = jnp.max(s, axis=-1, keepdims=True)
    m_new = s_max if m is None else jnp.maximum(m, s_max)
    p = jnp.exp2(s - m_new).astype(_BF)
    pv = jnp.dot(p, v1, preferred_element_type=_F32)
    if acc is not None:
        pv = jnp.exp2(m - m_new) * acc + pv
    return m_new, pv


def _attn_finish(acc, o_ref):
    o_ref[...] = (acc[:, :HEAD_V] / acc[:, HEAD_V:]).astype(_BF)


def _attn_lat_kernel(q_ref, kc_ref, vc_ref, kl_ref, vl_ref, o_ref, *, tk, n_chunks):
    q = q_ref[...]
    m, acc = _softmax_step(q, kc_ref[...], vc_ref[...], None, None)
    for c in range(n_chunks):
        m, acc = _softmax_step(q, kl_ref[c * tk:(c + 1) * tk, :], vl_ref[c * tk:(c + 1) * tk, :], m, acc)
    _attn_finish(acc, o_ref)


def _attn_ctx_kernel(q_ref, kc_ref, vc_ref, o_ref):
    _, acc = _softmax_step(q_ref[...], kc_ref[...], vc_ref[...], None, None)
    _attn_finish(acc, o_ref)


def _attention(q, k, v1, n_batch, seq, ctx, n_heads):
    tq = _tile(seq, (512, 256))
    tk = _tile(seq, (512, 256))
    nlt = seq // tq
    ctx_blk0 = n_batch * seq // ctx
    att_lat = pl.pallas_call(
        functools.partial(_attn_lat_kernel, tk=tk, n_chunks=seq // tk),
        grid=(n_batch, n_heads, nlt),
        in_specs=[pl.BlockSpec((tq, HEAD_QK), lambda b, h, i: (b * nlt + i, h)),
                  pl.BlockSpec((ctx, HEAD_QK), lambda b, h, i: (ctx_blk0 + b, h)),
                  pl.BlockSpec((ctx, HEAD_QK), lambda b, h, i: (ctx_blk0 + b, h)),
                  pl.BlockSpec((seq, HEAD_QK), lambda b, h, i: (b, h)),
                  pl.BlockSpec((seq, HEAD_QK), lambda b, h, i: (b, h))],
        out_specs=pl.BlockSpec((tq, HEAD_V), lambda b, h, i: (b * nlt + i, h)),
        out_shape=jax.ShapeDtypeStruct((n_batch * seq, n_heads * HEAD_V), _BF),
        compiler_params=_params(("parallel", "parallel", "arbitrary"), 48),
        name="attention",
    )(q, k, v1, k, v1)
    att_ctx = pl.pallas_call(
        _attn_ctx_kernel,
        grid=(n_batch, n_heads),
        in_specs=[pl.BlockSpec((ctx, HEAD_QK), lambda b, h: (ctx_blk0 + b, h)),
                  pl.BlockSpec((ctx, HEAD_QK), lambda b, h: (ctx_blk0 + b, h)),
                  pl.BlockSpec((ctx, HEAD_QK), lambda b, h: (ctx_blk0 + b, h))],
        out_specs=pl.BlockSpec((ctx, HEAD_V), lambda b, h: (b, h)),
        out_shape=jax.ShapeDtypeStruct((n_batch * ctx, n_heads * HEAD_V), _BF),
        compiler_params=_params(("parallel", "parallel"), 32),
        name="attention_ctx",
    )(q, k, v1)
    return jnp.concatenate([att_lat, att_ctx], axis=0)


def _dft_mats(n):
    j = jnp.arange(n, dtype=jnp.int32)
    jk = (j[:, None] * j[None, :]) % n
    ang = jk.astype(_F32) * (2.0 * math.pi / n)
    return jnp.cos(ang).astype(_BF), jnp.sin(ang).astype(_BF)


def _dft_ch_kernel(f_ref, cs_ref, pc_ref, ps_ref, *, fd):
    y = jnp.dot(f_ref[...].astype(_BF), cs_ref[...], preferred_element_type=_F32)
    pc_ref[...] = y[:, :fd].astype(_BF)
    ps_ref[...] = y[:, fd:].astype(_BF)


def _dft_channels(y_in, cs_mat, f_off, fd):
    T = y_in.shape[0]
    tm = _tile(T, (512, 256))
    out = jax.ShapeDtypeStruct((T, F_GROUPS * fd), _BF)
    return pl.pallas_call(
        functools.partial(_dft_ch_kernel, fd=fd),
        grid=(T // tm, F_GROUPS),
        in_specs=[pl.BlockSpec((tm, fd), lambda i, g: (i, f_off // fd + g)),
                  pl.BlockSpec((fd, 2 * fd), lambda i, g: (0, 0))],
        out_specs=[pl.BlockSpec((tm, fd), lambda i, g: (i, g)), pl.BlockSpec((tm, fd), lambda i, g: (i, g))],
        out_shape=[out, out],
        compiler_params=_params(("parallel", "parallel"), 32),
        name="dft_channels",
    )(y_in, cs_mat)


def _dft_pos_kernel(c_ref, s_ref, pc_ref, ps_ref, wf_ref, o_ref, *, norm, fd, groups):
    acc = jnp.dot(c_ref[...], pc_ref[...], preferred_element_type=_F32)
    acc = acc - jnp.dot(s_ref[...], ps_ref[...], preferred_element_type=_F32)
    fr = (acc * norm).astype(_BF)
    for g in range(groups):
        o_ref[:, g * fd:(g + 1) * fd] = jnp.dot(fr[:, g * fd:(g + 1) * fd], wf_ref[g],
                                                preferred_element_type=_F32).astype(_BF)


def _dft_positions(pc, ps, cmat, smat, w_f, n_batch, n, row0, fd):
    tm = min(512, n)
    gb = 2
    tn = gb * fd
    blk0 = row0 // n
    return pl.pallas_call(
        functools.partial(_dft_pos_kernel, norm=1.0 / math.sqrt(n * fd), fd=fd, groups=gb),
        grid=(n_batch, F_GROUPS // gb, n // tm),
        in_specs=[pl.BlockSpec((tm, n), lambda b, j, m: (m, 0)),
                  pl.BlockSpec((tm, n), lambda b, j, m: (m, 0)),
                  pl.BlockSpec((n, tn), lambda b, j, m: (blk0 + b, j)),
                  pl.BlockSpec((n, tn), lambda b, j, m: (blk0 + b, j)),
                  pl.BlockSpec((gb, fd, fd), lambda b, j, m: (j, 0, 0))],
        out_specs=pl.BlockSpec((tm, tn), lambda b, j, m: (b * (n // tm) + m, j)),
        out_shape=jax.ShapeDtypeStruct((n_batch * n, F_GROUPS * fd), _BF),
        compiler_params=_params(("parallel", "parallel", "arbitrary"), 48),
        name=f"dft_positions_{n}",
    )(cmat, smat, pc, ps, w_f)


def _chunk_kernel(u_ref, v_ref, ws_ref, bs_ref, o_ref, *, heads, cdim, n_chunks):
    u = jax.nn.gelu(u_ref[...])
    v = jax.nn.gelu(v_ref[...])
    for h in range(heads):
        vh = v[:, h * cdim:(h + 1) * cdim]
        mu = jnp.mean(vh, axis=-1, keepdims=True)
        d = vh - mu
        var = jnp.mean(d * d, axis=-1, keepdims=True)
        vs = (d * lax.rsqrt(var + EPS)).astype(_BF)
        w = ws_ref[h]
        bias = bs_ref[h]
        for c in range(n_chunks):
            rows = slice(c * CHUNK, (c + 1) * CHUNK)
            sv = jnp.dot(w, vs[rows], preferred_element_type=_F32) + bias
            o_ref[rows, h * cdim:(h + 1) * cdim] = (u[rows, h * cdim:(h + 1) * cdim] * sv).astype(_BF)


def _chunk_mix(y_in, w_s, b_s, u_off, width):
    T = y_in.shape[0]
    tm = _tile(T, (512, 256))
    cdim = width // C_HEADS
    return pl.pallas_call(
        functools.partial(_chunk_kernel, heads=C_HEADS, cdim=cdim, n_chunks=tm // CHUNK),
        grid=(T // tm,),
        in_specs=[pl.BlockSpec((tm, width), lambda i: (i, u_off // width)),
                  pl.BlockSpec((tm, width), lambda i: (i, u_off // width + 1)),
                  pl.BlockSpec((C_HEADS, CHUNK, CHUNK), lambda i: (0, 0, 0)),
                  pl.BlockSpec((C_HEADS, CHUNK, 1), lambda i: (0, 0, 0))],
        out_specs=pl.BlockSpec((tm, width), lambda i: (i, 0)),
        out_shape=jax.ShapeDtypeStruct((T, width), _BF),
        compiler_params=_params(("parallel",), 32),
        name="chunk_mix",
    )(y_in, y_in, w_s, b_s.reshape(C_HEADS, CHUNK, 1))


_R_GIDX = N_GROUPS


def _route(lg):
    lane = lax.broadcasted_iota(jnp.int32, lg.shape, 1)

    def col(j):
        return jnp.sum(jnp.where(lane == j, lg, 0.0), axis=-1, keepdims=True)

    g = [col(j) for j in range(N_GROUPS)]
    gmax = functools.reduce(jnp.maximum, g)
    gi = jnp.full(gmax.shape, N_GROUPS - 1, jnp.int32)
    for j in reversed(range(N_GROUPS - 1)):
        gi = jnp.where(g[j] == gmax, j, gi)
    gw = 1.0 / functools.reduce(lambda a, b: a + b, [jnp.exp(x - gmax) for x in g])

    e = []
    for j in range(EXP_PER_GROUP):
        ej = col(N_GROUPS + j)
        for gg in range(1, N_GROUPS):
            ej = jnp.where(gi == gg, col(N_GROUPS + gg * EXP_PER_GROUP + j), ej)
        e.append(ej)
    v1 = functools.reduce(jnp.maximum, e)
    i1 = jnp.full(v1.shape, EXP_PER_GROUP - 1, jnp.int32)
    for j in reversed(range(EXP_PER_GROUP - 1)):
        i1 = jnp.where(e[j] == v1, j, i1)
    e2 = [jnp.where(i1 == j, -jnp.inf, e[j]) for j in range(EXP_PER_GROUP)]
    v2 = functools.reduce(jnp.maximum, e2)
    i2 = jnp.full(v2.shape, EXP_PER_GROUP - 1, jnp.int32)
    for j in reversed(range(EXP_PER_GROUP - 1)):
        i2 = jnp.where(e2[j] == v2, j, i2)
    t = jnp.exp(v2 - v1)
    w1 = (1.0 / (1.0 + t)) * gw
    w2 = (t / (1.0 + t)) * gw

    rec = jnp.where(lane == _R_GIDX, gi.astype(_F32), 0.0)
    for j in range(EXP_PER_GROUP):
        cw = jnp.where(i1 == j, w1, 0.0) + jnp.where(i2 == j, w2, 0.0)
        rec = rec + jnp.where(lane == j, cw, 0.0)
    return rec


def _route_plan(rec, tm):
    T = rec.shape[0]
    n_tiles = T // tm + N_GROUPS
    gidx = rec[:, _R_GIDX].astype(jnp.int32)
    onehot = (gidx[:, None] == jnp.arange(N_GROUPS, dtype=jnp.int32)[None, :]).astype(jnp.int32)
    rank = jnp.sum((jnp.cumsum(onehot, axis=0) - onehot) * onehot, axis=1)
    counts = jnp.sum(onehot, axis=0)
    tiles_g = (counts + tm - 1) // tm
    tile_end = jnp.cumsum(tiles_g)
    start = (tile_end - tiles_g) * tm
    dest = (jnp.sum(onehot * start[None, :], axis=1) + rank).astype(jnp.int32)
    src = jnp.zeros((n_tiles * tm,), jnp.int32).at[dest].set(jnp.arange(T, dtype=jnp.int32))
    tile_id = jnp.arange(n_tiles, dtype=jnp.int32)
    tile_group = jnp.minimum(jnp.sum((tile_id[:, None] >= tile_end[None, :]).astype(jnp.int32), axis=1), N_GROUPS - 1)
    tile_valid = (tile_id < tile_end[-1]).astype(jnp.int32)
    return dest, src, tile_group.astype(jnp.int32), tile_valid


def _moe_kernel(src_ref, tg_ref, tv_ref, hp_hbm, rs_ref, wg_ref, wu_ref, wd_ref, o_ref,
                gbuf, sem, x_scr, hid_scr, *, tm):
    j, s = pl.program_id(0), pl.program_id(1)
    valid = tv_ref[j] == 1

    @pl.when(s == 0)
    def _():
        _gather_pipelined(hp_hbm, src_ref, j, pl.num_programs(0), tm, gbuf, sem)
        lo, hi = _unpack_halves(gbuf[j % 2])
        half = lo.shape[1]
        x_scr[:, :half] = lo
        x_scr[:, half:] = hi

    @pl.when(valid & (s < EXP_PER_GROUP))
    def _():
        x = x_scr[...]
        gate = jnp.dot(x, wg_ref[0, 0], preferred_element_type=_F32)
        up = jnp.dot(x, wu_ref[0, 0], preferred_element_type=_F32)
        rec = rs_ref[...]
        cw = rec[:, 0:1]
        for k in range(1, EXP_PER_GROUP):
            cw = jnp.where(s == k, rec[:, k:k + 1], cw)
        hid_scr[s] = ((jax.nn.silu(gate) * up) * cw).astype(_BF)

    @pl.when(valid & (s >= EXP_PER_GROUP))
    def _():
        hid = jnp.concatenate([hid_scr[e] for e in range(EXP_PER_GROUP)], axis=1)
        wd = wd_ref[0]
        o_ref[...] = jnp.dot(hid, wd.reshape(wd.shape[0] * wd.shape[1], wd.shape[2]), preferred_element_type=_F32)

    @pl.when(jnp.logical_not(valid) & (s >= EXP_PER_GROUP))
    def _():
        o_ref[...] = jnp.zeros(o_ref.shape, o_ref.dtype)


def _moe(hp, rs, src, tile_group, tile_valid, w_gate, w_up, w_down, l, tm):
    T, half = hp.shape
    D = 2 * half
    P = src.shape[0]
    L, E, _, F = w_gate.shape
    tn = 1024
    w_down_g = w_down.reshape(L * N_GROUPS, EXP_PER_GROUP, F, D)

    def expert(j, s, src, tg, tv):
        return tg[j] * EXP_PER_GROUP + jnp.minimum(s, EXP_PER_GROUP - 1)

    def col(j, s, src, tg, tv):
        return jnp.maximum(s - EXP_PER_GROUP, 0)

    return pl.pallas_call(
        functools.partial(_moe_kernel, tm=tm),
        grid_spec=pltpu.PrefetchScalarGridSpec(
            num_scalar_prefetch=3,
            grid=(P // tm, EXP_PER_GROUP + D // tn),
            in_specs=[pl.BlockSpec(memory_space=pl.ANY),
                      pl.BlockSpec((tm, LANES), lambda j, s, *p: (j, 0)),
                      pl.BlockSpec((1, 1, D, F), lambda j, s, *p: (l, expert(j, s, *p), 0, 0)),
                      pl.BlockSpec((1, 1, D, F), lambda j, s, *p: (l, expert(j, s, *p), 0, 0)),
                      pl.BlockSpec((1, EXP_PER_GROUP, F, tn), lambda j, s, *p: (l * N_GROUPS + p[1][j], 0, 0, col(j, s, *p)))],
            out_specs=pl.BlockSpec((tm, tn), lambda j, s, *p: (j, col(j, s, *p))),
            scratch_shapes=[pltpu.VMEM((2, tm, half), jnp.uint32), pltpu.SemaphoreType.DMA((2,)),
                            pltpu.VMEM((tm, D), _BF), pltpu.VMEM((EXP_PER_GROUP, tm, F), _BF)]),
        out_shape=jax.ShapeDtypeStruct((P, D), _F32),
        compiler_params=_params(("arbitrary", "arbitrary"), 48),
        name="moe",
    )(src, tile_group, tile_valid, hp, rs, w_gate, w_up, w_down_g)


def _rot_cols(w):
    q = HEAD_ROPE // 4
    a, b, c, d = w[..., :q], w[..., q:2 * q], w[..., 2 * q:3 * q], w[..., 3 * q:]
    return jnp.concatenate([-b, a, -d, c], axis=-1)


def _prep_w_in(w_in, D):
    q, kv = D // 4, D // 8
    o1, o2, o3 = q, q + kv, q + kv + HEAD_ROPE
    cq, ckv, kr, rest = w_in[..., :o1], w_in[..., o1:o2], w_in[..., o2:o3], w_in[..., o3:]
    cols = [cq, rest, ckv, kr, _rot_cols(kr)]
    width = sum(c.shape[-1] for c in cols)
    pad = (-width) % 512
    cols.append(jnp.zeros(w_in.shape[:-1] + (pad,), w_in.dtype))
    return jnp.concatenate(cols, axis=-1).astype(_BF)


def _prep_w_uq(w_uq, n_heads):
    L, r, _ = w_uq.shape
    w = w_uq.reshape(L, r, n_heads, HEAD_V + HEAD_ROPE)
    rope = w[..., HEAD_V:]
    return jnp.concatenate([w[..., :HEAD_V], rope, _rot_cols(rope)], axis=-1).reshape(L, r, n_heads * HEAD_QK).astype(_BF)


def _prep_w_ukv(w_ukv, n_heads):
    L, r, _ = w_ukv.shape
    w = w_ukv.reshape(L, r, n_heads, 2 * HEAD_V)
    return jnp.concatenate([w[..., :HEAD_V].reshape(L, r, -1), w[..., HEAD_V:].reshape(L, r, -1)], axis=-1).astype(_BF)


def _rope_table(n_batch, seq, ctx):
    half = HEAD_ROPE // 2
    inv_freq = ROPE_THETA ** (-jnp.arange(0, half, 2, dtype=_F32) / half)
    pos = jnp.arange(seq)
    ang_r = (pos // GRID_W).astype(_F32)[:, None] * inv_freq
    ang_c = (pos % GRID_W).astype(_F32)[:, None] * inv_freq
    ang = jnp.concatenate([ang_r, ang_r, ang_c, ang_c], axis=-1)
    lat = jnp.concatenate([jnp.cos(ang), jnp.sin(ang)], axis=-1)
    ctx_rows = jnp.concatenate([jnp.ones((n_batch * ctx, HEAD_ROPE), _F32), jnp.zeros((n_batch * ctx, HEAD_ROPE), _F32)], axis=-1)
    return jnp.concatenate([jnp.tile(lat, (n_batch, 1)), ctx_rows], axis=0)


def kernel(x, c, ctx, c_ctx, w_ada, b_ada, w_in, q_norm_g, kv_norm_g, w_uq, w_ukv, w_fourier, w_spatial, b_spatial, w_out, ln1_g, ln1_b, w_router_group, b_router_group, w_router_expert, b_router_expert, w_gate, w_up, w_down, ln2_g, ln2_b):
    B, SEQ, D = x.shape
    CTX = ctx.shape[1]
    L = w_ada.shape[0]
    n_heads = (D // 2) // HEAD_V
    q_rank, kv_rank = D // 4, D // 8
    fd = D // 16
    alpha = (2.0 * L) ** 0.25
    n_seg = B + 1
    assert n_seg <= MOD_ROWS and SEQ % 256 == 0 and (B * CTX) % 256 == 0 and CTX == 256 and (B * SEQ) % CTX == 0
    moe_tm = 512

    f_off, u_off, ckv_off, kr_off = q_rank, 2 * q_rank, D, D + kv_rank

    w_in_p = _prep_w_in(w_in, D)
    w_uq_p = _prep_w_uq(w_uq, n_heads)
    w_ukv_p = _prep_w_ukv(w_ukv, n_heads)
    w_out_b = w_out.astype(_BF)
    w_f_b = w_fourier.astype(_BF)
    w_s_b = w_spatial.astype(_BF)
    w_gate_b = w_gate.astype(_BF)
    w_up_b = w_up.astype(_BF)
    w_down_b = w_down.astype(_BF)
    n_logits = N_GROUPS + N_GROUPS * EXP_PER_GROUP
    w_r = jnp.concatenate([w_router_group, w_router_expert, jnp.zeros((L, D, LANES - n_logits), _F32)], axis=-1).astype(_BF)
    b_r = jnp.concatenate([b_router_group, b_router_expert, jnp.zeros((L, LANES - n_logits), _F32)], axis=-1).reshape(L, 1, LANES)
    cs_tab = _rope_table(B, SEQ, CTX)
    c_lat, s_lat = _dft_mats(SEQ)
    c_ctx_m, s_ctx_m = _dft_mats(CTX)
    c_ch, s_ch = _dft_mats(fd)
    cs_ch = jnp.concatenate([c_ch, s_ch], axis=1)

    c_rows = jnp.concatenate([c, c_ctx[None, :], jnp.zeros((MOD_ROWS - n_seg, D), _F32)], axis=0)
    mod_all = _adaln(c_rows, w_ada, b_ada)

    xt = jnp.concatenate([x.reshape(B * SEQ, D), ctx.reshape(B * CTX, D)], axis=0)
    mod3 = [mod_all[l].reshape(MOD_ROWS * 6, 1, D) for l in range(L)]
    h = _modulate(xt, mod3[0], 1, 0, SEQ, B)

    for l in range(L):
        last = l == L - 1
        y_in = _matmul(h, w_in_p, l, _F32, "in_proj")
        q = _q_proj(y_in, q_norm_g[l], w_uq_p, l, cs_tab, q_rank, math.log2(math.e) / math.sqrt(HEAD_V + HEAD_ROPE))
        k, v1 = _kv_proj(y_in, kv_norm_g[l], w_ukv_p, l, cs_tab, kv_rank, ckv_off, kr_off, n_heads)
        att = _attention(q, k, v1, B, SEQ, CTX, n_heads)
        pc, ps = _dft_channels(y_in, cs_ch, f_off, fd)
        fmix = jnp.concatenate([
            _dft_positions(pc, ps, c_lat, s_lat, w_f_b[l], B, SEQ, 0, fd),
            _dft_positions(pc, ps, c_ctx_m, s_ctx_m, w_f_b[l], B, CTX, B * SEQ, fd)], axis=0)
        cmix = _chunk_mix(y_in, w_s_b[l], b_spatial[l], u_off, q_rank)
        mixed = _out_proj(att, fmix, cmix, w_out_b, l)
        x1, h2p, rec = _resid_ln_route(xt, mixed, mod3[l], 2, ln1_g[l], ln1_b[l], 4, 3, w_r, b_r, l, SEQ, B, alpha)
        dest, src, tile_group, tile_valid = _route_plan(rec, moe_tm)
        ys = _moe(h2p, rec[src], src, tile_group, tile_valid, w_gate_b, w_up_b, w_down_b, l, moe_tm)
        xt, h = _resid_ln_gather(x1, ys, dest, mod3[l], 5, ln2_g[l], ln2_b[l], 1, 0, SEQ, B, alpha,
                                 emit_h=not last, mod_next=mod3[min(l + 1, L - 1)])

    return xt[:B * SEQ].reshape(B, SEQ, D)
```

```python
import functools
import math

import jax
import jax.numpy as jnp
from jax import lax
from jax.experimental import pallas as pl
from jax.experimental.pallas import tpu as pltpu

_BF = jnp.bfloat16
_F32 = jnp.float32

ROPE_THETA = 10000.0
EPS = 1e-6
GRID_W = 64
HEAD_V = 128
HEAD_ROPE = 64
HEAD_QK = 256
F_GROUPS = 4
C_HEADS = 4
CHUNK = 128
N_GROUPS = 4
EXP_PER_GROUP = 4
LANES = 128
MOD_ROWS = 8

_MIB = 1024 * 1024


def _params(sem, vmem_mib):
    return pltpu.CompilerParams(dimension_semantics=sem, vmem_limit_bytes=vmem_mib * _MIB)


def _tile(n, prefs):
    for t in prefs:
        if n % t == 0:
            return t
    raise ValueError(f"no tile in {prefs} divides {n}")


def _adaln_kernel(c_ref, w_ref, b_ref, o_ref):
    s = jax.nn.silu(c_ref[...]).astype(_BF)
    o_ref[0] = jnp.dot(s, w_ref[0].astype(_BF), preferred_element_type=_F32) + b_ref[0]


def _adaln(c_rows, w_ada, b_ada):
    L, D, N = w_ada.shape
    tn = 512
    return pl.pallas_call(
        _adaln_kernel,
        grid=(L, N // tn),
        in_specs=[pl.BlockSpec((MOD_ROWS, D), lambda l, j: (0, 0)),
                  pl.BlockSpec((1, D, tn), lambda l, j: (l, 0, j)),
                  pl.BlockSpec((1, 1, tn), lambda l, j: (l, 0, j))],
        out_specs=pl.BlockSpec((1, MOD_ROWS, tn), lambda l, j: (l, 0, j)),
        out_shape=jax.ShapeDtypeStruct((L, MOD_ROWS, N), _F32),
        compiler_params=_params(("parallel", "parallel"), 40),
        name="adaln",
    )(c_rows, w_ada, b_ada.reshape(L, 1, N))


def _mod_spec(chunk, D, tiles_per_seq, n_batch):
    return pl.BlockSpec((1, 1, D), lambda i, *_: (jnp.minimum(i // tiles_per_seq, n_batch) * 6 + chunk, 0, 0))


def _row_specs(parts, tm, width):
    if len(parts) == 1:
        return [pl.BlockSpec((tm, width), lambda i, *_: (i, 0))]
    n_lat = parts[0].shape[0] // tm
    return [pl.BlockSpec((tm, width), lambda i, *_: (jnp.minimum(i, n_lat - 1), 0)),
            pl.BlockSpec((tm, width), lambda i, *_: (jnp.maximum(i - n_lat, 0), 0))]


def _pick_rows(refs, is_lat):
    return refs[0][...] if len(refs) == 1 else jnp.where(is_lat, refs[0][...], refs[1][...])


def _modulate_kernel(*refs, n_x, n_lat):
    x_refs, (sc_ref, sh_ref, o_ref) = refs[:n_x], refs[n_x:]
    x = _pick_rows(x_refs, pl.program_id(0) < n_lat)
    o_ref[...] = (x * (1.0 + sc_ref[0]) + sh_ref[0]).astype(_BF)


def _modulate(xs, mod, sc_chunk, sh_chunk, seq, n_batch):
    D = xs[0].shape[1]
    T = sum(p.shape[0] for p in xs)
    tm = _tile(seq, (256,))
    tps = seq // tm
    return pl.pallas_call(
        functools.partial(_modulate_kernel, n_x=len(xs), n_lat=xs[0].shape[0] // tm),
        grid=(T // tm,),
        in_specs=_row_specs(xs, tm, D) + [_mod_spec(sc_chunk, D, tps, n_batch), _mod_spec(sh_chunk, D, tps, n_batch)],
        out_specs=pl.BlockSpec((tm, D), lambda i: (i, 0)),
        out_shape=jax.ShapeDtypeStruct((T, D), _BF),
        compiler_params=_params(("parallel",), 32),
        name="modulate",
    )(*xs, mod, mod)


def _pack_halves(hb):
    half = hb.shape[1] // 2
    bits = pltpu.bitcast(hb.astype(_F32), jnp.uint32)
    return (bits[:, half:] & jnp.uint32(0xFFFF0000)) | (bits[:, :half] >> 16)


def _unpack_halves(u):
    lo = pltpu.bitcast(u << 16, _F32).astype(_BF)
    hi = pltpu.bitcast(u & jnp.uint32(0xFFFF0000), _F32).astype(_BF)
    return lo, hi


def _ln_tail(y, g_ref, b_ref, sc_ref, sh_ref, xo_ref, ho_ref):
    mu = jnp.mean(y, axis=-1, keepdims=True)
    d = y - mu
    var = jnp.mean(d * d, axis=-1, keepdims=True)
    xn = d * lax.rsqrt(var + EPS) * g_ref[...] + b_ref[...]
    xo_ref[...] = xn
    if sc_ref is None:
        return None
    hb = (xn * (1.0 + sc_ref[0]) + sh_ref[0]).astype(_BF)
    if ho_ref is not None:
        ho_ref[...] = hb
    return hb


def _resid_ln_route_kernel(*refs, n_x, n_lat, alpha):
    x_refs = refs[:n_x]
    a_ref, gate_ref, g_ref, b_ref, sc_ref, sh_ref, wr_ref, br_ref, xo_ref, hp_ref, r_ref = refs[n_x:]
    x = _pick_rows(x_refs, pl.program_id(0) < n_lat)
    y = alpha * x + gate_ref[0] * a_ref[...]
    hb = _ln_tail(y, g_ref, b_ref, sc_ref, sh_ref, xo_ref, None)
    rec = _route(jnp.dot(hb, wr_ref[0], preferred_element_type=_F32) + br_ref[0])
    half = hb.shape[1] // 2
    hp_ref[:, :half] = _pack_halves(hb)
    hp_ref[:, half:] = pltpu.bitcast(rec, jnp.uint32)
    r_ref[...] = rec


def _resid_ln_route(xs, a, mod, gate_chunk, ln_g, ln_b, sc_chunk, sh_chunk, w_r, b_r, l, seq, n_batch, alpha):
    T, D = a.shape
    tm = _tile(seq, (256,))
    tps = seq // tm
    row = pl.BlockSpec((tm, D), lambda i: (i, 0))
    vec = pl.BlockSpec((1, D), lambda i: (0, 0))
    return pl.pallas_call(
        functools.partial(_resid_ln_route_kernel, n_x=len(xs), n_lat=xs[0].shape[0] // tm, alpha=alpha),
        grid=(T // tm,),
        in_specs=_row_specs(xs, tm, D) + [row, _mod_spec(gate_chunk, D, tps, n_batch), vec, vec,
                                          _mod_spec(sc_chunk, D, tps, n_batch), _mod_spec(sh_chunk, D, tps, n_batch),
                                          pl.BlockSpec((1, D, LANES), lambda i: (l, 0, 0)),
                                          pl.BlockSpec((1, 1, LANES), lambda i: (l, 0, 0))],
        out_specs=[row, pl.BlockSpec((tm, D // 2 + LANES), lambda i: (i, 0)),
                   pl.BlockSpec((tm, LANES), lambda i: (i, 0))],
        out_shape=[jax.ShapeDtypeStruct((T, D), _F32), jax.ShapeDtypeStruct((T, D // 2 + LANES), jnp.uint32),
                   jax.ShapeDtypeStruct((T, LANES), _F32)],
        compiler_params=_params(("parallel",), 40 + 8 * len(xs)),
        name="resid_ln_route",
    )(*xs, a, mod, ln_g.reshape(1, D), ln_b.reshape(1, D), mod, mod, w_r, b_r)


def _gather_tile(src_hbm, idx_ref, tile, tm, buf, sem, wait, rows=None, inline=False):
    slot = tile % 2
    lo, n = rows if rows is not None else (0, tm)

    def _step(r, carry):
        cp = pltpu.make_async_copy(src_hbm.at[pl.ds(idx_ref[tile * tm + r], 1)], buf.at[slot, pl.ds(r, 1)],
                                   sem.at[slot])
        if wait:
            cp.wait()
        else:
            cp.start()
        return carry

    if inline:
        for r in range(n):
            _step(lo + r, 0)
    else:
        lax.fori_loop(lo, lo + n, _step, 0, unroll=8)


def _gather_pipelined(src_hbm, idx_ref, tile, n_tiles, tm, buf, sem):
    @pl.when(tile == 0)
    def _():
        _gather_tile(src_hbm, idx_ref, tile, tm, buf, sem, wait=False)

    _gather_tile(src_hbm, idx_ref, tile, tm, buf, sem, wait=True)

    @pl.when(tile + 1 < n_tiles)
    def _():
        _gather_tile(src_hbm, idx_ref, tile + 1, tm, buf, sem, wait=False)


def _resid_ln_gather_kernel(idx_ref, x_ref, ys_hbm, gate_ref, g_ref, b_ref, sc_ref, sh_ref, *rest, alpha, tm, emit_h):
    if emit_h:
        xo_ref, ho_ref, buf, sem = rest
    else:
        xo_ref, buf, sem = rest
        ho_ref = sc_ref = sh_ref = None
    i = pl.program_id(0)
    _gather_pipelined(ys_hbm, idx_ref, i, pl.num_programs(0), tm, buf, sem)
    y = alpha * x_ref[...] + gate_ref[0] * buf[i % 2]
    _ln_tail(y, g_ref, b_ref, sc_ref, sh_ref, xo_ref, ho_ref)


def _resid_ln_gather(x, ys, dest, mod, gate_chunk, ln_g, ln_b, sc_chunk, sh_chunk, seq, n_batch, alpha, emit_h, mod_next,
                     n_rows):
    T, D = n_rows, x.shape[1]
    tm = _tile(seq, (256,))
    tps = seq // tm
    row = pl.BlockSpec((tm, D), lambda i, idx: (i, 0))
    vec = pl.BlockSpec((1, D), lambda i, idx: (0, 0))
    out_specs = [row, row] if emit_h else [row]
    out_shape = [jax.ShapeDtypeStruct((T, D), _F32)] + ([jax.ShapeDtypeStruct((T, D), _BF)] if emit_h else [])
    res = pl.pallas_call(
        functools.partial(_resid_ln_gather_kernel, alpha=alpha, tm=tm, emit_h=emit_h),
        grid_spec=pltpu.PrefetchScalarGridSpec(
            num_scalar_prefetch=1,
            grid=(T // tm,),
            in_specs=[row, pl.BlockSpec(memory_space=pl.ANY), _mod_spec(gate_chunk, D, tps, n_batch), vec, vec,
                      _mod_spec(sc_chunk, D, tps, n_batch), _mod_spec(sh_chunk, D, tps, n_batch)],
            out_specs=out_specs,
            scratch_shapes=[pltpu.VMEM((2, tm, D), _F32), pltpu.SemaphoreType.DMA((2,))]),
        out_shape=out_shape,
        compiler_params=_params(("arbitrary",), 40),
        name="resid_ln_gather",
    )(dest, x, ys, mod, ln_g.reshape(1, D), ln_b.reshape(1, D), mod_next, mod_next)
    return res if emit_h else (res[0], None)


def _mm_kernel(a_ref, b_ref, o_ref):
    o_ref[...] = jnp.dot(a_ref[...], b_ref[0], preferred_element_type=_F32).astype(o_ref.dtype)


def _matmul(a, b, l, out_dtype, name):
    M, K = a.shape
    N = b.shape[2]
    tm = _tile(M, (1088, 768, 512, 256))
    tn = 512
    return pl.pallas_call(
        _mm_kernel,
        grid=(M // tm, N // tn),
        in_specs=[pl.BlockSpec((tm, K), lambda i, j: (i, 0)), pl.BlockSpec((1, K, tn), lambda i, j: (l, 0, j))],
        out_specs=pl.BlockSpec((tm, tn), lambda i, j: (i, j)),
        out_shape=jax.ShapeDtypeStruct((M, N), out_dtype),
        compiler_params=_params(("parallel", "parallel"), 48),
        name=name,
    )(a, b)


def _mm3_kernel(al_ref, ac_ref, fl_ref, fc_ref, c_ref, b1_ref, b2_ref, b3_ref, o_ref, *, n_lat):
    is_lat = pl.program_id(0) < n_lat
    acc = jnp.dot(_pick_rows((al_ref, ac_ref), is_lat), b1_ref[0], preferred_element_type=_F32)
    acc = acc + jnp.dot(_pick_rows((fl_ref, fc_ref), is_lat), b2_ref[0], preferred_element_type=_F32)
    acc = acc + jnp.dot(c_ref[...], b3_ref[0], preferred_element_type=_F32)
    o_ref[...] = acc


def _out_proj(att, fmix, cmix, w_out, l):
    M = cmix.shape[0]
    k1, k2, k3 = att[0].shape[1], fmix[0].shape[1], cmix.shape[1]
    N = w_out.shape[2]
    assert k2 == k3 and k1 % k2 == 0
    tm = _tile(att[1].shape[0], (512, 256))
    tn = 512
    return pl.pallas_call(
        functools.partial(_mm3_kernel, n_lat=att[0].shape[0] // tm),
        grid=(M // tm, N // tn),
        in_specs=_row_specs(att, tm, k1) + _row_specs(fmix, tm, k2) + [
            pl.BlockSpec((tm, k3), lambda i, j: (i, 0)),
            pl.BlockSpec((1, k1, tn), lambda i, j: (l, 0, j)),
            pl.BlockSpec((1, k2, tn), lambda i, j: (l, k1 // k2, j)),
            pl.BlockSpec((1, k3, tn), lambda i, j: (l, k1 // k2 + 1, j))],
        out_specs=pl.BlockSpec((tm, tn), lambda i, j: (i, j)),
        out_shape=jax.ShapeDtypeStruct((M, N), _F32),
        compiler_params=_params(("parallel", "parallel"), 48),
        name="out_proj",
    )(*att, *fmix, cmix, w_out, w_out, w_out)


def _rms_bf16(x, g):
    return (x * lax.rsqrt(jnp.mean(x * x, axis=-1, keepdims=True) + EPS) * g).astype(_BF)


def _rope_pair(t):
    return t + pltpu.roll(t, HEAD_ROPE, 1)


def _qproj_kernel(cq_ref, g_ref, w_ref, cs_ref, o_ref, *, heads, scale):
    xn = _rms_bf16(cq_ref[...], g_ref[...])
    y = jnp.dot(xn, w_ref[0], preferred_element_type=_F32)
    cs = cs_ref[...]
    for h in range(heads):
        lo = h * HEAD_QK
        o_ref[:, lo:lo + HEAD_V] = (y[:, lo:lo + HEAD_V] * scale).astype(_BF)
        t = y[:, lo + HEAD_V:lo + HEAD_QK] * cs
        o_ref[:, lo + HEAD_V:lo + HEAD_QK] = (_rope_pair(t) * scale).astype(_BF)


def _q_proj(y_in, g_q, w_uq_p, l, cs, q_rank, scale):
    T = y_in.shape[0]
    N = w_uq_p.shape[2]
    tm = _tile(T, (512, 256))
    hb = 4
    tn = hb * HEAD_QK
    return pl.pallas_call(
        functools.partial(_qproj_kernel, heads=hb, scale=scale),
        grid=(T // tm, N // tn),
        in_specs=[pl.BlockSpec((tm, q_rank), lambda i, j: (i, 0)),
                  pl.BlockSpec((1, q_rank), lambda i, j: (0, 0)),
                  pl.BlockSpec((1, q_rank, tn), lambda i, j: (l, 0, j)),
                  pl.BlockSpec((tm, LANES), lambda i, j: (i, 0))],
        out_specs=pl.BlockSpec((tm, tn), lambda i, j: (i, j)),
        out_shape=jax.ShapeDtypeStruct((T, N), _BF),
        compiler_params=_params(("parallel", "parallel"), 32),
        name="q_proj",
    )(y_in, g_q.reshape(1, q_rank), w_uq_p, cs)


def _kvproj_kernel(ckv_ref, g_ref, wk_ref, wv_ref, kr_ref, cs_ref, k_ref, v_ref, *, heads):
    xn = _rms_bf16(ckv_ref[...], g_ref[...])
    kn = jnp.dot(xn, wk_ref[0], preferred_element_type=_F32)
    v = jnp.dot(xn, wv_ref[0], preferred_element_type=_F32)
    t = kr_ref[...] * cs_ref[...]
    lane = lax.broadcasted_iota(jnp.int32, t.shape, 1)
    krf = jnp.where(lane < HEAD_ROPE, _rope_pair(t), 0.0).astype(_BF)
    ones = jnp.ones((t.shape[0], HEAD_V), _BF)
    for h in range(heads):
        k_ref[:, h * HEAD_QK:h * HEAD_QK + HEAD_V] = kn[:, h * HEAD_V:(h + 1) * HEAD_V].astype(_BF)
        k_ref[:, h * HEAD_QK + HEAD_V:(h + 1) * HEAD_QK] = krf
        v_ref[:, h * HEAD_QK:h * HEAD_QK + HEAD_V] = v[:, h * HEAD_V:(h + 1) * HEAD_V].astype(_BF)
        v_ref[:, h * HEAD_QK + HEAD_V:(h + 1) * HEAD_QK] = ones


def _kv_proj(y_in, g_kv, w_ukv_p, l, cs, kv_rank, ckv_off, kr_off, n_heads):
    T = y_in.shape[0]
    tm = _tile(T, (512, 256))
    hb = 4
    nb = n_heads // hb
    return pl.pallas_call(
        functools.partial(_kvproj_kernel, heads=hb),
        grid=(T // tm, nb),
        in_specs=[pl.BlockSpec((tm, kv_rank), lambda i, j: (i, ckv_off // kv_rank)),
                  pl.BlockSpec((1, kv_rank), lambda i, j: (0, 0)),
                  pl.BlockSpec((1, kv_rank, hb * HEAD_V), lambda i, j: (l, 0, j)),
                  pl.BlockSpec((1, kv_rank, hb * HEAD_V), lambda i, j: (l, 0, nb + j)),
                  pl.BlockSpec((tm, LANES), lambda i, j: (i, kr_off // LANES)),
                  pl.BlockSpec((tm, LANES), lambda i, j: (i, 0))],
        out_specs=[pl.BlockSpec((tm, hb * HEAD_QK), lambda i, j: (i, j)),
                   pl.BlockSpec((tm, hb * HEAD_QK), lambda i, j: (i, j))],
        out_shape=[jax.ShapeDtypeStruct((T, n_heads * HEAD_QK), _BF),
                   jax.ShapeDtypeStruct((T, n_heads * HEAD_QK), _BF)],
        compiler_params=_params(("parallel", "parallel"), 32),
        name="kv_proj",
    )(y_in, g_kv.reshape(1, kv_rank), w_ukv_p, w_ukv_p, y_in, cs)


_NT = (((1,), (1,)), ((), ()))


def _softmax_step(q, k, v1, m, acc):
    s = lax.dot_general(q, k, _NT, preferred_element_type=_F32)
    s_max = jnp.max(s, axis=-1, keepdims=True)
    m_new = s_max if m is None else jnp.maximum(m, s_max)
    p = jnp.exp2(s - m_new).astype(_BF)
    pv = jnp.dot(p, v1, preferred_element_type=_F32)
    if acc is not None:
        pv = jnp.exp2(m - m_new) * acc + pv
    return m_new, pv


def _attn_finish(acc, o_ref):
    o_ref[...] = (acc[:, :HEAD_V] / acc[:, HEAD_V:]).astype(_BF)


def _attn_lat_kernel(q_ref, kc_ref, vc_ref, kl_ref, vl_ref, o_ref, *, tk, n_chunks):
    q = q_ref[...]
    m, acc = _softmax_step(q, kc_ref[...], vc_ref[...], None, None)
    for c in range(n_chunks):
        m, acc = _softmax_step(q, kl_ref[c * tk:(c + 1) * tk, :], vl_ref[c * tk:(c + 1) * tk, :], m, acc)
    _attn_finish(acc, o_ref)


def _attn_ctx_kernel(q_ref, kc_ref, vc_ref, o_ref):
    _, acc = _softmax_step(q_ref[...], kc_ref[...], vc_ref[...], None, None)
    _attn_finish(acc, o_ref)


def _attention(q, k, v1, n_batch, seq, ctx, n_heads):
    tq = _tile(seq, (512, 256))
    tk = _tile(seq, (512, 256))
    nlt = seq // tq
    ctx_blk0 = n_batch * seq // ctx
    att_lat = pl.pallas_call(
        functools.partial(_attn_lat_kernel, tk=tk, n_chunks=seq // tk),
        grid=(n_batch, n_heads, nlt),
        in_specs=[pl.BlockSpec((tq, HEAD_QK), lambda b, h, i: (b * nlt + i, h)),
                  pl.BlockSpec((ctx, HEAD_QK), lambda b, h, i: (ctx_blk0 + b, h)),
                  pl.BlockSpec((ctx, HEAD_QK), lambda b, h, i: (ctx_blk0 + b, h)),
                  pl.BlockSpec((seq, HEAD_QK), lambda b, h, i: (b, h)),
                  pl.BlockSpec((seq, HEAD_QK), lambda b, h, i: (b, h))],
        out_specs=pl.BlockSpec((tq, HEAD_V), lambda b, h, i: (b * nlt + i, h)),
        out_shape=jax.ShapeDtypeStruct((n_batch * seq, n_heads * HEAD_V), _BF),
        compiler_params=_params(("parallel", "parallel", "arbitrary"), 48),
        name="attention",
    )(q, k, v1, k, v1)
    att_ctx = pl.pallas_call(
        _attn_ctx_kernel,
        grid=(n_batch, n_heads),
        in_specs=[pl.BlockSpec((ctx, HEAD_QK), lambda b, h: (ctx_blk0 + b, h)),
                  pl.BlockSpec((ctx, HEAD_QK), lambda b, h: (ctx_blk0 + b, h)),
                  pl.BlockSpec((ctx, HEAD_QK), lambda b, h: (ctx_blk0 + b, h))],
        out_specs=pl.BlockSpec((ctx, HEAD_V), lambda b, h: (b, h)),
        out_shape=jax.ShapeDtypeStruct((n_batch * ctx, n_heads * HEAD_V), _BF),
        compiler_params=_params(("parallel", "parallel"), 32),
        name="attention_ctx",
    )(q, k, v1)
    return att_lat, att_ctx


def _dft_mats(n):
    r = 1 << (n.bit_length() // 2)
    j = jnp.arange(n, dtype=jnp.int32)

    def table(k):
        ang = ((k[:, None] * j[None, :]) % n).astype(_F32) * (2.0 * math.pi / n)
        return jnp.cos(ang), jnp.sin(ang)

    ca, sa = table(jnp.arange(n // r, dtype=jnp.int32) * r)
    cb, sb = table(jnp.arange(r, dtype=jnp.int32))
    c = ca[:, None, :] * cb[None, :, :] - sa[:, None, :] * sb[None, :, :]
    s = sa[:, None, :] * cb[None, :, :] + ca[:, None, :] * sb[None, :, :]
    return c.reshape(n, n).astype(_BF), s.reshape(n, n).astype(_BF)


def _dft_ch_kernel(f_ref, cs_ref, pc_ref, ps_ref, *, fd):
    y = jnp.dot(f_ref[...].astype(_BF), cs_ref[...], preferred_element_type=_F32)
    pc_ref[...] = y[:, :fd].astype(_BF)
    ps_ref[...] = y[:, fd:].astype(_BF)


def _dft_channels(y_in, cs_mat, f_off, fd):
    T = y_in.shape[0]
    tm = _tile(T, (512, 256))
    out = jax.ShapeDtypeStruct((T, F_GROUPS * fd), _BF)
    return pl.pallas_call(
        functools.partial(_dft_ch_kernel, fd=fd),
        grid=(T // tm, F_GROUPS),
        in_specs=[pl.BlockSpec((tm, fd), lambda i, g: (i, f_off // fd + g)),
                  pl.BlockSpec((fd, 2 * fd), lambda i, g: (0, 0))],
        out_specs=[pl.BlockSpec((tm, fd), lambda i, g: (i, g)), pl.BlockSpec((tm, fd), lambda i, g: (i, g))],
        out_shape=[out, out],
        compiler_params=_params(("parallel", "parallel"), 32),
        name="dft_channels",
    )(y_in, cs_mat)


def _dft_pos_kernel(c_ref, s_ref, pc_ref, ps_ref, wf_ref, o_ref, *, norm, fd, groups):
    acc = jnp.dot(c_ref[...], pc_ref[...], preferred_element_type=_F32)
    acc = acc - jnp.dot(s_ref[...], ps_ref[...], preferred_element_type=_F32)
    fr = (acc * norm).astype(_BF)
    for g in range(groups):
        o_ref[:, g * fd:(g + 1) * fd] = jnp.dot(fr[:, g * fd:(g + 1) * fd], wf_ref[g],
                                                preferred_element_type=_F32).astype(_BF)


def _dft_positions(pc, ps, cmat, smat, w_f, n_batch, n, row0, fd):
    tm = min(512, n)
    gb = 2
    tn = gb * fd
    blk0 = row0 // n
    return pl.pallas_call(
        functools.partial(_dft_pos_kernel, norm=1.0 / math.sqrt(n * fd), fd=fd, groups=gb),
        grid=(n_batch, F_GROUPS // gb, n // tm),
        in_specs=[pl.BlockSpec((tm, n), lambda b, j, m: (m, 0)),
                  pl.BlockSpec((tm, n), lambda b, j, m: (m, 0)),
                  pl.BlockSpec((n, tn), lambda b, j, m: (blk0 + b, j)),
                  pl.BlockSpec((n, tn), lambda b, j, m: (blk0 + b, j)),
                  pl.BlockSpec((gb, fd, fd), lambda b, j, m: (j, 0, 0))],
        out_specs=pl.BlockSpec((tm, tn), lambda b, j, m: (b * (n // tm) + m, j)),
        out_shape=jax.ShapeDtypeStruct((n_batch * n, F_GROUPS * fd), _BF),
        compiler_params=_params(("parallel", "parallel", "arbitrary"), 48),
        name=f"dft_positions_{n}",
    )(cmat, smat, pc, ps, w_f)


def _chunk_kernel(u_ref, v_ref, ws_ref, bs_ref, o_ref, *, heads, cdim, n_chunks):
    u = jax.nn.gelu(u_ref[...])
    v = jax.nn.gelu(v_ref[...])
    for h in range(heads):
        vh = v[:, h * cdim:(h + 1) * cdim]
        mu = jnp.mean(vh, axis=-1, keepdims=True)
        d = vh - mu
        var = jnp.mean(d * d, axis=-1, keepdims=True)
        vs = (d * lax.rsqrt(var + EPS)).astype(_BF)
        w = ws_ref[h]
        bias = bs_ref[h]
        for c in range(n_chunks):
            rows = slice(c * CHUNK, (c + 1) * CHUNK)
            sv = jnp.dot(w, vs[rows], preferred_element_type=_F32) + bias
            o_ref[rows, h * cdim:(h + 1) * cdim] = (u[rows, h * cdim:(h + 1) * cdim] * sv).astype(_BF)


def _chunk_mix(y_in, w_s, b_s, u_off, width):
    T = y_in.shape[0]
    tm = _tile(T, (512, 256))
    cdim = width // C_HEADS
    return pl.pallas_call(
        functools.partial(_chunk_kernel, heads=C_HEADS, cdim=cdim, n_chunks=tm // CHUNK),
        grid=(T // tm,),
        in_specs=[pl.BlockSpec((tm, width), lambda i: (i, u_off // width)),
                  pl.BlockSpec((tm, width), lambda i: (i, u_off // width + 1)),
                  pl.BlockSpec((C_HEADS, CHUNK, CHUNK), lambda i: (0, 0, 0)),
                  pl.BlockSpec((C_HEADS, CHUNK, 1), lambda i: (0, 0, 0))],
        out_specs=pl.BlockSpec((tm, width), lambda i: (i, 0)),
        out_shape=jax.ShapeDtypeStruct((T, width), _BF),
        compiler_params=_params(("parallel",), 32),
        name="chunk_mix",
    )(y_in, y_in, w_s, b_s.reshape(C_HEADS, CHUNK, 1))


_R_GIDX = N_GROUPS


def _route(lg):
    lane = lax.broadcasted_iota(jnp.int32, lg.shape, 1)

    def col(j):
        return jnp.sum(jnp.where(lane == j, lg, 0.0), axis=-1, keepdims=True)

    g = [col(j) for j in range(N_GROUPS)]
    gmax = functools.reduce(jnp.maximum, g)
    gi = jnp.full(gmax.shape, N_GROUPS - 1, jnp.int32)
    for j in reversed(range(N_GROUPS - 1)):
        gi = jnp.where(g[j] == gmax, j, gi)
    gw = 1.0 / functools.reduce(lambda a, b: a + b, [jnp.exp(x - gmax) for x in g])

    e = []
    for j in range(EXP_PER_GROUP):
        ej = col(N_GROUPS + j)
        for gg in range(1, N_GROUPS):
            ej = jnp.where(gi == gg, col(N_GROUPS + gg * EXP_PER_GROUP + j), ej)
        e.append(ej)
    v1 = functools.reduce(jnp.maximum, e)
    i1 = jnp.full(v1.shape, EXP_PER_GROUP - 1, jnp.int32)
    for j in reversed(range(EXP_PER_GROUP - 1)):
        i1 = jnp.where(e[j] == v1, j, i1)
    e2 = [jnp.where(i1 == j, -jnp.inf, e[j]) for j in range(EXP_PER_GROUP)]
    v2 = functools.reduce(jnp.maximum, e2)
    i2 = jnp.full(v2.shape, EXP_PER_GROUP - 1, jnp.int32)
    for j in reversed(range(EXP_PER_GROUP - 1)):
        i2 = jnp.where(e2[j] == v2, j, i2)
    t = jnp.exp(v2 - v1)
    w1 = (1.0 / (1.0 + t)) * gw
    w2 = (t / (1.0 + t)) * gw

    rec = jnp.where(lane == _R_GIDX, gi.astype(_F32), 0.0)
    for j in range(EXP_PER_GROUP):
        cw = jnp.where(i1 == j, w1, 0.0) + jnp.where(i2 == j, w2, 0.0)
        rec = rec + jnp.where(lane == j, cw, 0.0)
    return rec


def _route_plan(rec, tm):
    T = rec.shape[0]
    n_tiles = T // tm + N_GROUPS
    gidx = rec[:, _R_GIDX].astype(jnp.int32)
    onehot = (gidx[:, None] == jnp.arange(N_GROUPS, dtype=jnp.int32)[None, :]).astype(jnp.int32)
    rank = jnp.sum((jnp.cumsum(onehot, axis=0) - onehot) * onehot, axis=1)
    counts = jnp.sum(onehot, axis=0)
    tiles_g = (counts + tm - 1) // tm
    tile_end = jnp.cumsum(tiles_g)
    start = (tile_end - tiles_g) * tm
    dest = (jnp.sum(onehot * start[None, :], axis=1) + rank).astype(jnp.int32)
    src = jnp.zeros(((n_tiles + 1) * tm,), jnp.int32).at[dest].set(jnp.arange(T, dtype=jnp.int32))
    tile_id = jnp.arange(n_tiles, dtype=jnp.int32)
    tile_group = jnp.minimum(jnp.sum((tile_id[:, None] >= tile_end[None, :]).astype(jnp.int32), axis=1), N_GROUPS - 1)
    return dest, src, tile_group.astype(jnp.int32)


def _moe_kernel(src_ref, tg_ref, hp_hbm, wg_ref, wu_ref, wd_ref, o_ref, gbuf, sem, x_scr, rec_scr, hid_scr, *, tm, half):
    j, s = pl.program_id(0), pl.program_id(1)
    last_j, last_s = pl.num_programs(0) - 1, pl.num_programs(1) - 1
    quarter = tm // EXP_PER_GROUP
    F = wg_ref.shape[3]
    split = (F // 256) * 256
    tail = F - split

    @pl.when(s == 0)
    def _():
        @pl.when(j == 0)
        def _():
            _gather_tile(hp_hbm, src_ref, j, tm, gbuf, sem, wait=False)

        _gather_tile(hp_hbm, src_ref, j, tm, gbuf, sem, wait=True)
        rows = gbuf[j % 2]
        lo, hi = _unpack_halves(rows[:, :half])
        x_scr[:, :half] = lo
        x_scr[:, half:] = hi
        rec_scr[...] = pltpu.bitcast(rows[:, half:], _F32)

    @pl.when(s < EXP_PER_GROUP)
    def _():
        _gather_tile(hp_hbm, src_ref, j + 1, tm, gbuf, sem, wait=False, rows=(s * quarter, quarter), inline=True)
        x = x_scr[...]
        if split and 2 * tail == 256:
            w_tail = jnp.concatenate([wg_ref[0, 0, :, split:], wu_ref[0, 0, :, split:]], axis=1)
            t = jnp.dot(x, w_tail, preferred_element_type=_F32)
            gate = jnp.concatenate([jnp.dot(x, wg_ref[0, 0, :, :split], preferred_element_type=_F32), t[:, :tail]], axis=1)
            up = jnp.concatenate([jnp.dot(x, wu_ref[0, 0, :, :split], preferred_element_type=_F32), t[:, tail:]], axis=1)
        else:
            gate = jnp.dot(x, wg_ref[0, 0], preferred_element_type=_F32)
            up = jnp.dot(x, wu_ref[0, 0], preferred_element_type=_F32)
        rec = rec_scr[...]
        cw = rec[:, 0:1]
        for k in range(1, EXP_PER_GROUP):
            cw = jnp.where(s == k, rec[:, k:k + 1], cw)
        hid_scr[s] = ((jax.nn.silu(gate) * up) * cw).astype(_BF)

    @pl.when(s >= EXP_PER_GROUP)
    def _():
        hid = jnp.concatenate([hid_scr[e] for e in range(EXP_PER_GROUP)], axis=1)
        wd = wd_ref[0]
        o_ref[...] = jnp.dot(hid, wd.reshape(wd.shape[0] * wd.shape[1], wd.shape[2]), preferred_element_type=_F32)

    @pl.when((j == last_j) & (s == last_s))
    def _():
        _gather_tile(hp_hbm, src_ref, j + 1, tm, gbuf, sem, wait=True)


def _moe(hp, src, tile_group, w_gate, w_up, w_down, l, tm):
    T, width = hp.shape
    half = width - LANES
    D = 2 * half
    n_tiles = tile_group.shape[0]
    L, E, _, F = w_gate.shape
    tn = 1024
    w_down_g = w_down.reshape(L * N_GROUPS, EXP_PER_GROUP, F, D)
    assert src.shape[0] == (n_tiles + 1) * tm

    def expert(j, s, src, tg):
        return tg[j] * EXP_PER_GROUP + jnp.minimum(s, EXP_PER_GROUP - 1)

    def col(j, s, src, tg):
        return jnp.maximum(s - EXP_PER_GROUP, 0)

    return pl.pallas_call(
        functools.partial(_moe_kernel, tm=tm, half=half),
        grid_spec=pltpu.PrefetchScalarGridSpec(
            num_scalar_prefetch=2,
            grid=(n_tiles, EXP_PER_GROUP + D // tn),
            in_specs=[pl.BlockSpec(memory_space=pl.ANY),
                      pl.BlockSpec((1, 1, D, F), lambda j, s, *p: (l, expert(j, s, *p), 0, 0)),
                      pl.BlockSpec((1, 1, D, F), lambda j, s, *p: (l, expert(j, s, *p), 0, 0)),
                      pl.BlockSpec((1, EXP_PER_GROUP, F, tn), lambda j, s, *p: (l * N_GROUPS + p[1][j], 0, 0, col(j, s, *p)))],
            out_specs=pl.BlockSpec((tm, tn), lambda j, s, *p: (j, col(j, s, *p))),
            scratch_shapes=[pltpu.VMEM((2, tm, width), jnp.uint32), pltpu.SemaphoreType.DMA((2,)),
                            pltpu.VMEM((tm, D), _BF), pltpu.VMEM((tm, LANES), _F32),
                            pltpu.VMEM((EXP_PER_GROUP, tm, F), _BF)]),
        out_shape=jax.ShapeDtypeStruct((n_tiles * tm, D), _F32),
        compiler_params=_params(("arbitrary", "arbitrary"), 48),
        name="moe",
    )(src, tile_group, hp, w_gate, w_up, w_down_g)


def _rot_cols(w):
    q = HEAD_ROPE // 4
    a, b, c, d = w[..., :q], w[..., q:2 * q], w[..., 2 * q:3 * q], w[..., 3 * q:]
    return jnp.concatenate([-b, a, -d, c], axis=-1)


def _prep_w_in(w_in, D):
    q, kv = D // 4, D // 8
    o1, o2, o3 = q, q + kv, q + kv + HEAD_ROPE
    kr = w_in[..., o2:o3]
    pieces = [w_in[..., :o1], w_in[..., o3:], w_in[..., o1:o2], kr, _rot_cols(kr)]
    width = sum(p.shape[-1] for p in pieces)
    total = width + (-width) % 512
    out, off = None, 0
    for p in pieces:
        cfg = [(0, 0, 0)] * (p.ndim - 1) + [(off, total - off - p.shape[-1], 0)]
        term = lax.pad(p.astype(_BF), jnp.zeros((), _BF), cfg)
        out = term if out is None else out + term
        off += p.shape[-1]
    return out


def _prep_w_uq(w_uq, n_heads):
    L, r, _ = w_uq.shape
    w = w_uq.reshape(L, r, n_heads, HEAD_V + HEAD_ROPE)
    rope = w[..., HEAD_V:]
    return jnp.concatenate([w[..., :HEAD_V], rope, _rot_cols(rope)], axis=-1).reshape(L, r, n_heads * HEAD_QK).astype(_BF)


def _prep_w_ukv(w_ukv, n_heads):
    L, r, _ = w_ukv.shape
    w = w_ukv.reshape(L, r, n_heads, 2 * HEAD_V)
    return jnp.concatenate([w[..., :HEAD_V].reshape(L, r, -1), w[..., HEAD_V:].reshape(L, r, -1)], axis=-1).astype(_BF)


def _rope_table(n_batch, seq, ctx):
    half = HEAD_ROPE // 2
    inv_freq = ROPE_THETA ** (-jnp.arange(0, half, 2, dtype=_F32) / half)
    pos = jnp.arange(seq)
    ang_r = (pos // GRID_W).astype(_F32)[:, None] * inv_freq
    ang_c = (pos % GRID_W).astype(_F32)[:, None] * inv_freq
    ang = jnp.concatenate([ang_r, ang_r, ang_c, ang_c], axis=-1)
    lat = jnp.concatenate([jnp.cos(ang), jnp.sin(ang)], axis=-1)
    ctx_rows = jnp.concatenate([jnp.ones((n_batch * ctx, HEAD_ROPE), _F32), jnp.zeros((n_batch * ctx, HEAD_ROPE), _F32)], axis=-1)
    return jnp.concatenate([jnp.tile(lat, (n_batch, 1)), ctx_rows], axis=0)


def kernel(x, c, ctx, c_ctx, w_ada, b_ada, w_in, q_norm_g, kv_norm_g, w_uq, w_ukv, w_fourier, w_spatial, b_spatial, w_out, ln1_g, ln1_b, w_router_group, b_router_group, w_router_expert, b_router_expert, w_gate, w_up, w_down, ln2_g, ln2_b):
    B, SEQ, D = x.shape
    CTX = ctx.shape[1]
    L = w_ada.shape[0]
    n_heads = (D // 2) // HEAD_V
    q_rank, kv_rank = D // 4, D // 8
    fd = D // 16
    alpha = (2.0 * L) ** 0.25
    n_seg = B + 1
    assert n_seg <= MOD_ROWS and SEQ % 256 == 0 and (B * CTX) % 256 == 0 and CTX == 256 and (B * SEQ) % CTX == 0
    moe_tm = 512

    f_off, u_off, ckv_off, kr_off = q_rank, 2 * q_rank, D, D + kv_rank

    w_in_p = _prep_w_in(w_in, D)
    w_uq_p = _prep_w_uq(w_uq, n_heads)
    w_ukv_p = _prep_w_ukv(w_ukv, n_heads)
    w_out_b = w_out.astype(_BF)
    w_f_b = w_fourier.astype(_BF)
    w_s_b = w_spatial.astype(_BF)
    w_gate_b = w_gate.astype(_BF)
    w_up_b = w_up.astype(_BF)
    w_down_b = w_down.astype(_BF)
    n_logits = N_GROUPS + N_GROUPS * EXP_PER_GROUP
    w_r = jnp.concatenate([w_router_group, w_router_expert, jnp.zeros((L, D, LANES - n_logits), _F32)], axis=-1).astype(_BF)
    b_r = jnp.concatenate([b_router_group, b_router_expert, jnp.zeros((L, LANES - n_logits), _F32)], axis=-1).reshape(L, 1, LANES)
    cs_tab = _rope_table(B, SEQ, CTX)
    c_lat, s_lat = _dft_mats(SEQ)
    c_ctx_m, s_ctx_m = _dft_mats(CTX)
    c_ch, s_ch = _dft_mats(fd)
    cs_ch = jnp.concatenate([c_ch, s_ch], axis=1)

    c_rows = jnp.concatenate([c, c_ctx[None, :], jnp.zeros((MOD_ROWS - n_seg, D), _F32)], axis=0)
    mod_all = _adaln(c_rows, w_ada, b_ada)

    xt = (x.reshape(B * SEQ, D), ctx.reshape(B * CTX, D))
    mod3 = [mod_all[l].reshape(MOD_ROWS * 6, 1, D) for l in range(L)]
    h = _modulate(xt, mod3[0], 1, 0, SEQ, B)

    for l in range(L):
        last = l == L - 1
        y_in = _matmul(h, w_in_p, l, _F32, "in_proj")
        q = _q_proj(y_in, q_norm_g[l], w_uq_p, l, cs_tab, q_rank, math.log2(math.e) / math.sqrt(HEAD_V + HEAD_ROPE))
        k, v1 = _kv_proj(y_in, kv_norm_g[l], w_ukv_p, l, cs_tab, kv_rank, ckv_off, kr_off, n_heads)
        att = _attention(q, k, v1, B, SEQ, CTX, n_heads)
        pc, ps = _dft_channels(y_in, cs_ch, f_off, fd)
        fmix = (_dft_positions(pc, ps, c_lat, s_lat, w_f_b[l], B, SEQ, 0, fd),
                _dft_positions(pc, ps, c_ctx_m, s_ctx_m, w_f_b[l], B, CTX, B * SEQ, fd))
        cmix = _chunk_mix(y_in, w_s_b[l], b_spatial[l], u_off, q_rank)
        mixed = _out_proj(att, fmix, cmix, w_out_b, l)
        x1, h2p, rec = _resid_ln_route(xt, mixed, mod3[l], 2, ln1_g[l], ln1_b[l], 4, 3, w_r, b_r, l, SEQ, B, alpha)
        dest, src, tile_group = _route_plan(rec, moe_tm)
        ys = _moe(h2p, src, tile_group, w_gate_b, w_up_b, w_down_b, l, moe_tm)
        x2, h = _resid_ln_gather(x1, ys, dest, mod3[l], 5, ln2_g[l], ln2_b[l], 1, 0, SEQ, B, alpha,
                                 emit_h=not last, mod_next=mod3[min(l + 1, L - 1)],
                                 n_rows=B * SEQ if last else B * (SEQ + CTX))
        xt = (x2,)

    return xt[0].reshape(B, SEQ, D)
```

```python
import functools
import math

import jax
import jax.numpy as jnp
from jax import lax
from jax.experimental import pallas as pl
from jax.experimental.pallas import tpu as pltpu

_BF = jnp.bfloat16
_F32 = jnp.float32

ROPE_THETA = 10000.0
EPS = 1e-6
GRID_W = 64
HEAD_V = 128
HEAD_ROPE = 64
HEAD_QK = 256
F_GROUPS = 4
C_HEADS = 4
CHUNK = 128
N_GROUPS = 4
EXP_PER_GROUP = 4
LANES = 128
MOD_ROWS = 8

_MIB = 1024 * 1024


def _params(sem, vmem_mib):
    return pltpu.CompilerParams(dimension_semantics=sem, vmem_limit_bytes=vmem_mib * _MIB)


def _tile(n, prefs):
    for t in prefs:
        if n % t == 0:
            return t
    raise ValueError(f"no tile in {prefs} divides {n}")


def _adaln_kernel(c_ref, w_ref, b_ref, o_ref):
    s = jax.nn.silu(c_ref[...]).astype(_BF)
    o_ref[0] = jnp.dot(s, w_ref[0].astype(_BF), preferred_element_type=_F32) + b_ref[0]


def _adaln(c_rows, w_ada, b_ada):
    L, D, N = w_ada.shape
    tn = 512
    return pl.pallas_call(
        _adaln_kernel,
        grid=(L, N // tn),
        in_specs=[pl.BlockSpec((MOD_ROWS, D), lambda l, j: (0, 0)),
                  pl.BlockSpec((1, D, tn), lambda l, j: (l, 0, j)),
                  pl.BlockSpec((1, 1, tn), lambda l, j: (l, 0, j))],
        out_specs=pl.BlockSpec((1, MOD_ROWS, tn), lambda l, j: (l, 0, j)),
        out_shape=jax.ShapeDtypeStruct((L, MOD_ROWS, N), _F32),
        compiler_params=_params(("parallel", "parallel"), 40),
        name="adaln",
    )(c_rows, w_ada, b_ada.reshape(L, 1, N))


def _mod_spec(chunk, D, tiles_per_seq, n_batch):
    return pl.BlockSpec((1, 1, D), lambda i, *_: (jnp.minimum(i // tiles_per_seq, n_batch) * 6 + chunk, 0, 0))


def _row_specs(parts, tm, width):
    if len(parts) == 1:
        return [pl.BlockSpec((tm, width), lambda i, *_: (i, 0))]
    n_lat = parts[0].shape[0] // tm
    return [pl.BlockSpec((tm, width), lambda i, *_: (jnp.minimum(i, n_lat - 1), 0)),
            pl.BlockSpec((tm, width), lambda i, *_: (jnp.maximum(i - n_lat, 0), 0))]


def _pick_rows(refs, is_lat):
    return refs[0][...] if len(refs) == 1 else jnp.where(is_lat, refs[0][...], refs[1][...])


def _modulate_kernel(*refs, n_x, n_lat):
    x_refs, (sc_ref, sh_ref, o_ref) = refs[:n_x], refs[n_x:]
    x = _pick_rows(x_refs, pl.program_id(0) < n_lat)
    o_ref[...] = (x * (1.0 + sc_ref[0]) + sh_ref[0]).astype(_BF)


def _modulate(xs, mod, sc_chunk, sh_chunk, seq, n_batch):
    D = xs[0].shape[1]
    T = sum(p.shape[0] for p in xs)
    tm = _tile(seq, (256,))
    tps = seq // tm
    return pl.pallas_call(
        functools.partial(_modulate_kernel, n_x=len(xs), n_lat=xs[0].shape[0] // tm),
        grid=(T // tm,),
        in_specs=_row_specs(xs, tm, D) + [_mod_spec(sc_chunk, D, tps, n_batch), _mod_spec(sh_chunk, D, tps, n_batch)],
        out_specs=pl.BlockSpec((tm, D), lambda i: (i, 0)),
        out_shape=jax.ShapeDtypeStruct((T, D), _BF),
        compiler_params=_params(("parallel",), 32),
        name="modulate",
    )(*xs, mod, mod)


def _pack_halves(hb):
    half = hb.shape[1] // 2
    bits = pltpu.bitcast(hb.astype(_F32), jnp.uint32)
    return (bits[:, half:] & jnp.uint32(0xFFFF0000)) | (bits[:, :half] >> 16)


def _unpack_halves(u):
    lo = pltpu.bitcast(u << 16, _F32).astype(_BF)
    hi = pltpu.bitcast(u & jnp.uint32(0xFFFF0000), _F32).astype(_BF)
    return lo, hi


def _ln_tail(y, g_ref, b_ref, sc_ref, sh_ref, xo_ref, ho_ref):
    mu = jnp.mean(y, axis=-1, keepdims=True)
    d = y - mu
    var = jnp.mean(d * d, axis=-1, keepdims=True)
    xn = d * lax.rsqrt(var + EPS) * g_ref[...] + b_ref[...]
    xo_ref[...] = xn
    if sc_ref is None:
        return None
    hb = (xn * (1.0 + sc_ref[0]) + sh_ref[0]).astype(_BF)
    if ho_ref is not None:
        ho_ref[...] = hb
    return hb


def _resid_ln_route_kernel(*refs, n_x, n_lat, alpha):
    x_refs = refs[:n_x]
    a_ref, gate_ref, g_ref, b_ref, sc_ref, sh_ref, wr_ref, br_ref, xo_ref, hp_ref, r_ref = refs[n_x:]
    x = _pick_rows(x_refs, pl.program_id(0) < n_lat)
    y = alpha * x + gate_ref[0] * a_ref[...]
    hb = _ln_tail(y, g_ref, b_ref, sc_ref, sh_ref, xo_ref, None)
    rec = _route(jnp.dot(hb, wr_ref[0], preferred_element_type=_F32) + br_ref[0])
    half = hb.shape[1] // 2
    hp_ref[:, :half] = _pack_halves(hb)
    hp_ref[:, half:] = pltpu.bitcast(rec, jnp.uint32)
    r_ref[...] = rec


def _resid_ln_route(xs, a, mod, gate_chunk, ln_g, ln_b, sc_chunk, sh_chunk, w_r, b_r, l, seq, n_batch, alpha):
    T, D = a.shape
    tm = _tile(seq, (256,))
    tps = seq // tm
    row = pl.BlockSpec((tm, D), lambda i: (i, 0))
    vec = pl.BlockSpec((1, D), lambda i: (0, 0))
    return pl.pallas_call(
        functools.partial(_resid_ln_route_kernel, n_x=len(xs), n_lat=xs[0].shape[0] // tm, alpha=alpha),
        grid=(T // tm,),
        in_specs=_row_specs(xs, tm, D) + [row, _mod_spec(gate_chunk, D, tps, n_batch), vec, vec,
                                          _mod_spec(sc_chunk, D, tps, n_batch), _mod_spec(sh_chunk, D, tps, n_batch),
                                          pl.BlockSpec((1, D, LANES), lambda i: (l, 0, 0)),
                                          pl.BlockSpec((1, 1, LANES), lambda i: (l, 0, 0))],
        out_specs=[row, pl.BlockSpec((tm, D // 2 + LANES), lambda i: (i, 0)),
                   pl.BlockSpec((tm, LANES), lambda i: (i, 0))],
        out_shape=[jax.ShapeDtypeStruct((T, D), _F32), jax.ShapeDtypeStruct((T, D // 2 + LANES), jnp.uint32),
                   jax.ShapeDtypeStruct((T, LANES), _F32)],
        compiler_params=_params(("parallel",), 40 + 8 * len(xs)),
        name="resid_ln_route",
    )(*xs, a, mod, ln_g.reshape(1, D), ln_b.reshape(1, D), mod, mod, w_r, b_r)


def _gather_tile(src_hbm, idx_ref, tile, tm, buf, sem, wait, rows=None, inline=False):
    slot = tile % 2
    lo, n = rows if rows is not None else (0, tm)
    def _step(r, carry):
        cp = pltpu.make_async_copy(src_hbm.at[pl.ds(idx_ref[tile * tm + r], 1)], buf.at[slot, pl.ds(r, 1)],
                                   sem.at[slot])
        if wait:
            cp.wait()
        else:
            cp.start()
        return carry

    if inline:
        for r in range(n):
            _step(lo + r, 0)
    else:
        lax.fori_loop(lo, lo + n, _step, 0, unroll=8)


def _gather_pipelined(src_hbm, idx_ref, tile, n_tiles, tm, buf, sem):
    @pl.when(tile == 0)
    def _():
        _gather_tile(src_hbm, idx_ref, tile, tm, buf, sem, wait=False)

    _gather_tile(src_hbm, idx_ref, tile, tm, buf, sem, wait=True)

    @pl.when(tile + 1 < n_tiles)
    def _():
        _gather_tile(src_hbm, idx_ref, tile + 1, tm, buf, sem, wait=False)


def _resid_ln_gather_kernel(idx_ref, x_ref, ys_hbm, gate_ref, g_ref, b_ref, sc_ref, sh_ref, *rest, alpha, tm, emit_h):
    if emit_h:
        xo_ref, ho_ref, buf, sem = rest
    else:
        xo_ref, buf, sem = rest
        ho_ref = sc_ref = sh_ref = None
    i = pl.program_id(0)
    _gather_pipelined(ys_hbm, idx_ref, i, pl.num_programs(0), tm, buf, sem)
    y = alpha * x_ref[...] + gate_ref[0] * buf[i % 2]
    _ln_tail(y, g_ref, b_ref, sc_ref, sh_ref, xo_ref, ho_ref)


def _resid_ln_gather(x, ys, dest, mod, gate_chunk, ln_g, ln_b, sc_chunk, sh_chunk, seq, n_batch, alpha, emit_h, mod_next,
                     n_rows):
    T, D = n_rows, x.shape[1]
    tm = _tile(seq, (256,))
    tps = seq // tm
    row = pl.BlockSpec((tm, D), lambda i, idx: (i, 0))
    vec = pl.BlockSpec((1, D), lambda i, idx: (0, 0))
    out_specs = [row, row] if emit_h else [row]
    out_shape = [jax.ShapeDtypeStruct((T, D), _F32)] + ([jax.ShapeDtypeStruct((T, D), _BF)] if emit_h else [])
    res = pl.pallas_call(
        functools.partial(_resid_ln_gather_kernel, alpha=alpha, tm=tm, emit_h=emit_h),
        grid_spec=pltpu.PrefetchScalarGridSpec(
            num_scalar_prefetch=1,
            grid=(T // tm,),
            in_specs=[row, pl.BlockSpec(memory_space=pl.ANY), _mod_spec(gate_chunk, D, tps, n_batch), vec, vec,
                      _mod_spec(sc_chunk, D, tps, n_batch), _mod_spec(sh_chunk, D, tps, n_batch)],
            out_specs=out_specs,
            scratch_shapes=[pltpu.VMEM((2, tm, D), _F32), pltpu.SemaphoreType.DMA((2,))]),
        out_shape=out_shape,
        compiler_params=_params(("arbitrary",), 40),
        name="resid_ln_gather",
    )(dest, x, ys, mod, ln_g.reshape(1, D), ln_b.reshape(1, D), mod_next, mod_next)
    return res if emit_h else (res[0], None)


def _mm_kernel(a_ref, b_ref, o_ref):
    o_ref[...] = jnp.dot(a_ref[...], b_ref[0], preferred_element_type=_F32).astype(o_ref.dtype)


def _matmul(a, b, l, out_dtype, name):
    M, K = a.shape
    N = b.shape[2]
    tm = _tile(M, (1088, 768, 512, 256))
    tn = 512
    return pl.pallas_call(
        _mm_kernel,
        grid=(M // tm, N // tn),
        in_specs=[pl.BlockSpec((tm, K), lambda i, j: (i, 0)), pl.BlockSpec((1, K, tn), lambda i, j: (l, 0, j))],
        out_specs=pl.BlockSpec((tm, tn), lambda i, j: (i, j)),
        out_shape=jax.ShapeDtypeStruct((M, N), out_dtype),
        compiler_params=_params(("parallel", "parallel"), 48),
        name=name,
    )(a, b)


def _mm3_kernel(al_ref, ac_ref, fl_ref, fc_ref, c_ref, b1_ref, b2_ref, b3_ref, o_ref, *, n_lat):
    is_lat = pl.program_id(1) < n_lat
    acc = jnp.dot(_pick_rows((al_ref, ac_ref), is_lat), b1_ref[0], preferred_element_type=_F32)
    acc = acc + jnp.dot(_pick_rows((fl_ref, fc_ref), is_lat), b2_ref[0], preferred_element_type=_F32)
    acc = acc + jnp.dot(c_ref[...], b3_ref[0], preferred_element_type=_F32)
    o_ref[...] = acc


def _out_proj(att, fmix, cmix, w_out, l):
    M = cmix.shape[0]
    k1, k2, k3 = att[0].shape[1], fmix[0].shape[1], cmix.shape[1]
    N = w_out.shape[2]
    assert k2 == k3 and k1 % k2 == 0
    tm = _tile(att[1].shape[0], (512, 256))
    tn = 1024
    n_lat = att[0].shape[0] // tm

    def rows(parts, width):
        return [pl.BlockSpec((tm, width), lambda j, i: (jnp.minimum(i, n_lat - 1), 0)),
                pl.BlockSpec((tm, width), lambda j, i: (jnp.maximum(i - n_lat, 0), 0))]

    return pl.pallas_call(
        functools.partial(_mm3_kernel, n_lat=n_lat),
        grid=(N // tn, M // tm),
        in_specs=rows(att, k1) + rows(fmix, k2) + [
            pl.BlockSpec((tm, k3), lambda j, i: (i, 0)),
            pl.BlockSpec((1, k1, tn), lambda j, i: (l, 0, j)),
            pl.BlockSpec((1, k2, tn), lambda j, i: (l, k1 // k2, j)),
            pl.BlockSpec((1, k3, tn), lambda j, i: (l, k1 // k2 + 1, j))],
        out_specs=pl.BlockSpec((tm, tn), lambda j, i: (i, j)),
        out_shape=jax.ShapeDtypeStruct((M, N), _F32),
        compiler_params=_params(("parallel", "parallel"), 48),
        name="out_proj",
    )(*att, *fmix, cmix, w_out, w_out, w_out)


def _rms_bf16(x, g):
    return (x * lax.rsqrt(jnp.mean(x * x, axis=-1, keepdims=True) + EPS) * g).astype(_BF)


def _rope_pair(t):
    return t + pltpu.roll(t, HEAD_ROPE, 1)


def _qproj_kernel(cq_ref, g_ref, w_ref, cs_ref, o_ref, *, heads, scale):
    xn = _rms_bf16(cq_ref[...], g_ref[...])
    y = jnp.dot(xn, w_ref[0], preferred_element_type=_F32)
    cs = cs_ref[...]
    for h in range(heads):
        lo = h * HEAD_QK
        o_ref[:, lo:lo + HEAD_V] = (y[:, lo:lo + HEAD_V] * scale).astype(_BF)
        t = y[:, lo + HEAD_V:lo + HEAD_QK] * cs
        o_ref[:, lo + HEAD_V:lo + HEAD_QK] = (_rope_pair(t) * scale).astype(_BF)


def _q_proj(y_in, g_q, w_uq_p, l, cs, q_rank, scale):
    T = y_in.shape[0]
    N = w_uq_p.shape[2]
    tm = _tile(T, (512, 256))
    hb = 4
    tn = hb * HEAD_QK
    return pl.pallas_call(
        functools.partial(_qproj_kernel, heads=hb, scale=scale),
        grid=(T // tm, N // tn),
        in_specs=[pl.BlockSpec((tm, q_rank), lambda i, j: (i, 0)),
                  pl.BlockSpec((1, q_rank), lambda i, j: (0, 0)),
                  pl.BlockSpec((1, q_rank, tn), lambda i, j: (l, 0, j)),
                  pl.BlockSpec((tm, LANES), lambda i, j: (i, 0))],
        out_specs=pl.BlockSpec((tm, tn), lambda i, j: (i, j)),
        out_shape=jax.ShapeDtypeStruct((T, N), _BF),
        compiler_params=_params(("parallel", "parallel"), 32),
        name="q_proj",
    )(y_in, g_q.reshape(1, q_rank), w_uq_p, cs)


def _kvproj_kernel(ckv_ref, g_ref, wk_ref, wv_ref, kr_ref, cs_ref, k_ref, v_ref, *, heads):
    xn = _rms_bf16(ckv_ref[...], g_ref[...])
    kn = jnp.dot(xn, wk_ref[0], preferred_element_type=_F32)
    v = jnp.dot(xn, wv_ref[0], preferred_element_type=_F32)
    t = kr_ref[...] * cs_ref[...]
    lane = lax.broadcasted_iota(jnp.int32, t.shape, 1)
    krf = jnp.where(lane < HEAD_ROPE, _rope_pair(t), 0.0).astype(_BF)
    ones = jnp.ones((t.shape[0], HEAD_V), _BF)
    for h in range(heads):
        k_ref[:, h * HEAD_QK:h * HEAD_QK + HEAD_V] = kn[:, h * HEAD_V:(h + 1) * HEAD_V].astype(_BF)
        k_ref[:, h * HEAD_QK + HEAD_V:(h + 1) * HEAD_QK] = krf
        v_ref[:, h * HEAD_QK:h * HEAD_QK + HEAD_V] = v[:, h * HEAD_V:(h + 1) * HEAD_V].astype(_BF)
        v_ref[:, h * HEAD_QK + HEAD_V:(h + 1) * HEAD_QK] = ones


def _kv_proj(y_in, g_kv, w_ukv_p, l, cs, kv_rank, ckv_off, kr_off, n_heads):
    T = y_in.shape[0]
    tm = _tile(T, (512, 256))
    hb = 4
    nb = n_heads // hb
    return pl.pallas_call(
        functools.partial(_kvproj_kernel, heads=hb),
        grid=(T // tm, nb),
        in_specs=[pl.BlockSpec((tm, kv_rank), lambda i, j: (i, ckv_off // kv_rank)),
                  pl.BlockSpec((1, kv_rank), lambda i, j: (0, 0)),
                  pl.BlockSpec((1, kv_rank, hb * HEAD_V), lambda i, j: (l, 0, j)),
                  pl.BlockSpec((1, kv_rank, hb * HEAD_V), lambda i, j: (l, 0, nb + j)),
                  pl.BlockSpec((tm, LANES), lambda i, j: (i, kr_off // LANES)),
                  pl.BlockSpec((tm, LANES), lambda i, j: (i, 0))],
        out_specs=[pl.BlockSpec((tm, hb * HEAD_QK), lambda i, j: (i, j)),
                   pl.BlockSpec((tm, hb * HEAD_QK), lambda i, j: (i, j))],
        out_shape=[jax.ShapeDtypeStruct((T, n_heads * HEAD_QK), _BF),
                   jax.ShapeDtypeStruct((T, n_heads * HEAD_QK), _BF)],
        compiler_params=_params(("parallel", "parallel"), 32),
        name="kv_proj",
    )(y_in, g_kv.reshape(1, kv_rank), w_ukv_p, w_ukv_p, y_in, cs)


_NT = (((1,), (1,)), ((), ()))


def _softmax_step(q, k, v1, m, acc):
    s = lax.dot_general(q, k, _NT, preferred_element_type=_F32)
    s_max = jnp.max(s, axis=-1, keepdims=True)
    m_new = s_max if m is None else jnp.maximum(m, s_max)
    p = jnp.exp2((s - m_new).astype(_BF))
    pv = jnp.dot(p, v1, preferred_element_type=_F32)
    if acc is not None:
        pv = jnp.exp2(m - m_new) * acc + pv
    return m_new, pv


def _attn_finish(acc, o_ref):
    o_ref[...] = (acc[:, :HEAD_V] / acc[:, HEAD_V:]).astype(_BF)


def _attn_lat_kernel(q_ref, kc_ref, vc_ref, kl_ref, vl_ref, o_ref, *, tk, n_chunks):
    q = q_ref[...]
    m, acc = _softmax_step(q, kc_ref[...], vc_ref[...], None, None)
    for c in range(n_chunks):
        m, acc = _softmax_step(q, kl_ref[c * tk:(c + 1) * tk, :], vl_ref[c * tk:(c + 1) * tk, :], m, acc)
    _attn_finish(acc, o_ref)


def _attn_ctx_kernel(q_ref, kc_ref, vc_ref, o_ref):
    _, acc = _softmax_step(q_ref[...], kc_ref[...], vc_ref[...], None, None)
    _attn_finish(acc, o_ref)


def _attention(q, k, v1, n_batch, seq, ctx, n_heads):
    tq = _tile(seq, (1024, 512, 256))
    tk = _tile(seq, (512, 256))
    nlt = seq // tq
    ctx_blk0 = n_batch * seq // ctx
    att_lat = pl.pallas_call(
        functools.partial(_attn_lat_kernel, tk=tk, n_chunks=seq // tk),
        grid=(n_batch, n_heads, nlt),
        in_specs=[pl.BlockSpec((tq, HEAD_QK), lambda b, h, i: (b * nlt + i, h)),
                  pl.BlockSpec((ctx, HEAD_QK), lambda b, h, i: (ctx_blk0 + b, h)),
                  pl.BlockSpec((ctx, HEAD_QK), lambda b, h, i: (ctx_blk0 + b, h)),
                  pl.BlockSpec((seq, HEAD_QK), lambda b, h, i: (b, h)),
                  pl.BlockSpec((seq, HEAD_QK), lambda b, h, i: (b, h))],
        out_specs=pl.BlockSpec((tq, HEAD_V), lambda b, h, i: (b * nlt + i, h)),
        out_shape=jax.ShapeDtypeStruct((n_batch * seq, n_heads * HEAD_V), _BF),
        compiler_params=_params(("parallel", "parallel", "arbitrary"), 48),
        name="attention",
    )(q, k, v1, k, v1)
    att_ctx = pl.pallas_call(
        _attn_ctx_kernel,
        grid=(n_batch, n_heads),
        in_specs=[pl.BlockSpec((ctx, HEAD_QK), lambda b, h: (ctx_blk0 + b, h)),
                  pl.BlockSpec((ctx, HEAD_QK), lambda b, h: (ctx_blk0 + b, h)),
                  pl.BlockSpec((ctx, HEAD_QK), lambda b, h: (ctx_blk0 + b, h))],
        out_specs=pl.BlockSpec((ctx, HEAD_V), lambda b, h: (b, h)),
        out_shape=jax.ShapeDtypeStruct((n_batch * ctx, n_heads * HEAD_V), _BF),
        compiler_params=_params(("parallel", "parallel"), 32),
        name="attention_ctx",
    )(q, k, v1)
    return att_lat, att_ctx


def _dft_mats(n):
    r = 1 << (n.bit_length() // 2)
    j = jnp.arange(n, dtype=jnp.int32)

    def table(k):
        ang = ((k[:, None] * j[None, :]) % n).astype(_F32) * (2.0 * math.pi / n)
        return jnp.cos(ang), jnp.sin(ang)

    ca, sa = table(jnp.arange(n // r, dtype=jnp.int32) * r)
    cb, sb = table(jnp.arange(r, dtype=jnp.int32))
    c = ca[:, None, :] * cb[None, :, :] - sa[:, None, :] * sb[None, :, :]
    s = sa[:, None, :] * cb[None, :, :] + ca[:, None, :] * sb[None, :, :]
    return c.reshape(n, n).astype(_BF), s.reshape(n, n).astype(_BF)


def _dft_ch_kernel(f_ref, cs_ref, pc_ref, ps_ref, *, fd):
    y = jnp.dot(f_ref[...].astype(_BF), cs_ref[...], preferred_element_type=_F32)
    pc_ref[...] = y[:, :fd].astype(_BF)
    ps_ref[...] = y[:, fd:].astype(_BF)


def _dft_channels(y_in, cs_mat, f_off, fd):
    T = y_in.shape[0]
    tm = _tile(T, (512, 256))
    out = jax.ShapeDtypeStruct((T, F_GROUPS * fd), _BF)
    return pl.pallas_call(
        functools.partial(_dft_ch_kernel, fd=fd),
        grid=(T // tm, F_GROUPS),
        in_specs=[pl.BlockSpec((tm, fd), lambda i, g: (i, f_off // fd + g)),
                  pl.BlockSpec((fd, 2 * fd), lambda i, g: (0, 0))],
        out_specs=[pl.BlockSpec((tm, fd), lambda i, g: (i, g)), pl.BlockSpec((tm, fd), lambda i, g: (i, g))],
        out_shape=[out, out],
        compiler_params=_params(("parallel", "parallel"), 32),
        name="dft_channels",
    )(y_in, cs_mat)


def _dft_pos_kernel(c_ref, s_ref, pc_ref, ps_ref, wf_ref, o_ref, *, norm, fd, groups):
    acc = jnp.dot(c_ref[...], pc_ref[...], preferred_element_type=_F32)
    acc = acc - jnp.dot(s_ref[...], ps_ref[...], preferred_element_type=_F32)
    fr = (acc * norm).astype(_BF)
    for g in range(groups):
        o_ref[:, g * fd:(g + 1) * fd] = jnp.dot(fr[:, g * fd:(g + 1) * fd], wf_ref[g],
                                                preferred_element_type=_F32).astype(_BF)


def _dft_positions(pc, ps, cmat, smat, w_f, n_batch, n, row0, fd):
    tm = min(512, n)
    gb = 2
    tn = gb * fd
    blk0 = row0 // n
    return pl.pallas_call(
        functools.partial(_dft_pos_kernel, norm=1.0 / math.sqrt(n * fd), fd=fd, groups=gb),
        grid=(n_batch, F_GROUPS // gb, n // tm),
        in_specs=[pl.BlockSpec((tm, n), lambda b, j, m: (m, 0)),
                  pl.BlockSpec((tm, n), lambda b, j, m: (m, 0)),
                  pl.BlockSpec((n, tn), lambda b, j, m: (blk0 + b, j)),
                  pl.BlockSpec((n, tn), lambda b, j, m: (blk0 + b, j)),
                  pl.BlockSpec((gb, fd, fd), lambda b, j, m: (j, 0, 0))],
        out_specs=pl.BlockSpec((tm, tn), lambda b, j, m: (b * (n // tm) + m, j)),
        out_shape=jax.ShapeDtypeStruct((n_batch * n, F_GROUPS * fd), _BF),
        compiler_params=_params(("parallel", "parallel", "arbitrary"), 48),
        name=f"dft_positions_{n}",
    )(cmat, smat, pc, ps, w_f)


def _chunk_kernel(u_ref, v_ref, ws_ref, bs_ref, o_ref, *, heads, cdim, n_chunks):
    u = jax.nn.gelu(u_ref[...])
    v = jax.nn.gelu(v_ref[...])
    for h in range(heads):
        vh = v[:, h * cdim:(h + 1) * cdim]
        mu = jnp.mean(vh, axis=-1, keepdims=True)
        d = vh - mu
        var = jnp.mean(d * d, axis=-1, keepdims=True)
        vs = (d * lax.rsqrt(var + EPS)).astype(_BF)
        w = ws_ref[h]
        bias = bs_ref[h]
        for c in range(n_chunks):
            rows = slice(c * CHUNK, (c + 1) * CHUNK)
            sv = jnp.dot(w, vs[rows], preferred_element_type=_F32) + bias
            o_ref[rows, h * cdim:(h + 1) * cdim] = (u[rows, h * cdim:(h + 1) * cdim] * sv).astype(_BF)


def _chunk_mix(y_in, w_s, b_s, u_off, width):
    T = y_in.shape[0]
    tm = _tile(T, (512, 256))
    cdim = width // C_HEADS
    return pl.pallas_call(
        functools.partial(_chunk_kernel, heads=C_HEADS, cdim=cdim, n_chunks=tm // CHUNK),
        grid=(T // tm,),
        in_specs=[pl.BlockSpec((tm, width), lambda i: (i, u_off // width)),
                  pl.BlockSpec((tm, width), lambda i: (i, u_off // width + 1)),
                  pl.BlockSpec((C_HEADS, CHUNK, CHUNK), lambda i: (0, 0, 0)),
                  pl.BlockSpec((C_HEADS, CHUNK, 1), lambda i: (0, 0, 0))],
        out_specs=pl.BlockSpec((tm, width), lambda i: (i, 0)),
        out_shape=jax.ShapeDtypeStruct((T, width), _BF),
        compiler_params=_params(("parallel",), 32),
        name="chunk_mix",
    )(y_in, y_in, w_s, b_s.reshape(C_HEADS, CHUNK, 1))


_R_GIDX = N_GROUPS


def _route(lg):
    lane = lax.broadcasted_iota(jnp.int32, lg.shape, 1)

    def col(j):
        return jnp.sum(jnp.where(lane == j, lg, 0.0), axis=-1, keepdims=True)

    g = [col(j) for j in range(N_GROUPS)]
    gmax = functools.reduce(jnp.maximum, g)
    gi = jnp.full(gmax.shape, N_GROUPS - 1, jnp.int32)
    for j in reversed(range(N_GROUPS - 1)):
        gi = jnp.where(g[j] == gmax, j, gi)
    gw = 1.0 / functools.reduce(lambda a, b: a + b, [jnp.exp(x - gmax) for x in g])

    e = []
    for j in range(EXP_PER_GROUP):
        ej = col(N_GROUPS + j)
        for gg in range(1, N_GROUPS):
            ej = jnp.where(gi == gg, col(N_GROUPS + gg * EXP_PER_GROUP + j), ej)
        e.append(ej)
    v1 = functools.reduce(jnp.maximum, e)
    i1 = jnp.full(v1.shape, EXP_PER_GROUP - 1, jnp.int32)
    for j in reversed(range(EXP_PER_GROUP - 1)):
        i1 = jnp.where(e[j] == v1, j, i1)
    e2 = [jnp.where(i1 == j, -jnp.inf, e[j]) for j in range(EXP_PER_GROUP)]
    v2 = functools.reduce(jnp.maximum, e2)
    i2 = jnp.full(v2.shape, EXP_PER_GROUP - 1, jnp.int32)
    for j in reversed(range(EXP_PER_GROUP - 1)):
        i2 = jnp.where(e2[j] == v2, j, i2)
    t = jnp.exp(v2 - v1)
    w1 = (1.0 / (1.0 + t)) * gw
    w2 = (t / (1.0 + t)) * gw

    rec = jnp.where(lane == _R_GIDX, gi.astype(_F32), 0.0)
    for j in range(EXP_PER_GROUP):
        cw = jnp.where(i1 == j, w1, 0.0) + jnp.where(i2 == j, w2, 0.0)
        rec = rec + jnp.where(lane == j, cw, 0.0)
    return rec


def _route_plan(rec, tm):
    T = rec.shape[0]
    n_tiles = T // tm + N_GROUPS
    gidx = rec[:, _R_GIDX].astype(jnp.int32)
    onehot = (gidx[:, None] == jnp.arange(N_GROUPS, dtype=jnp.int32)[None, :]).astype(jnp.int32)
    rank = jnp.sum((jnp.cumsum(onehot, axis=0) - onehot) * onehot, axis=1)
    counts = jnp.sum(onehot, axis=0)
    tiles_g = (counts + tm - 1) // tm
    tile_end = jnp.cumsum(tiles_g)
    start = (tile_end - tiles_g) * tm
    dest = (jnp.sum(onehot * start[None, :], axis=1) + rank).astype(jnp.int32)
    src = jnp.zeros(((n_tiles + 1) * tm,), jnp.int32).at[dest].set(jnp.arange(T, dtype=jnp.int32))
    tile_id = jnp.arange(n_tiles, dtype=jnp.int32)
    tile_group = jnp.minimum(jnp.sum((tile_id[:, None] >= tile_end[None, :]).astype(jnp.int32), axis=1), N_GROUPS - 1)
    tile_used = (tile_id < tile_end[-1]).astype(jnp.int32)
    return dest, src, tile_group.astype(jnp.int32), tile_used


def _moe_kernel(src_ref, tg_ref, tv_ref, hp_hbm, wg_ref, wu_ref, wd_ref, o_ref, gbuf, sem, x_scr, rec_scr, hid_scr, *, tm, half):
    j, s = pl.program_id(0), pl.program_id(1)
    last_j, last_s = pl.num_programs(0) - 1, pl.num_programs(1) - 1
    used = tv_ref[j] == 1
    fetched = (j == 0) | (tv_ref[jnp.maximum(j - 1, 0)] == 1)
    quarter = tm // EXP_PER_GROUP
    F = wg_ref.shape[3]
    split = (F // 256) * 256
    tail = F - split

    @pl.when((s == 0) & fetched)
    def _():
        @pl.when(j == 0)
        def _():
            _gather_tile(hp_hbm, src_ref, j, tm, gbuf, sem, wait=False)

        _gather_tile(hp_hbm, src_ref, j, tm, gbuf, sem, wait=True)
        rows = gbuf[j % 2]
        lo, hi = _unpack_halves(rows[:, :half])
        x_scr[:, :half] = lo
        x_scr[:, half:] = hi
        rec_scr[...] = pltpu.bitcast(rows[:, half:], _F32)

    @pl.when(used & (s < EXP_PER_GROUP))
    def _():
        _gather_tile(hp_hbm, src_ref, j + 1, tm, gbuf, sem, wait=False, rows=(s * quarter, quarter), inline=True)
        x = x_scr[...]
        if split and 2 * tail == 256:
            w_tail = jnp.concatenate([wg_ref[0, 0, :, split:], wu_ref[0, 0, :, split:]], axis=1)
            t = jnp.dot(x, w_tail, preferred_element_type=_F32)
            gate = jnp.concatenate([jnp.dot(x, wg_ref[0, 0, :, :split], preferred_element_type=_F32), t[:, :tail]], axis=1)
            up = jnp.concatenate([jnp.dot(x, wu_ref[0, 0, :, :split], preferred_element_type=_F32), t[:, tail:]], axis=1)
        else:
            gate = jnp.dot(x, wg_ref[0, 0], preferred_element_type=_F32)
            up = jnp.dot(x, wu_ref[0, 0], preferred_element_type=_F32)
        rec = rec_scr[...]
        cw = rec[:, 0:1]
        for k in range(1, EXP_PER_GROUP):
            cw = jnp.where(s == k, rec[:, k:k + 1], cw)
        hid_scr[s] = ((jax.nn.silu(gate) * up) * cw).astype(_BF)

    @pl.when(used & (s >= EXP_PER_GROUP))
    def _():
        hid = jnp.concatenate([hid_scr[e] for e in range(EXP_PER_GROUP)], axis=1)
        wd = wd_ref[0]
        o_ref[...] = jnp.dot(hid, wd.reshape(wd.shape[0] * wd.shape[1], wd.shape[2]), preferred_element_type=_F32)

    @pl.when(jnp.logical_not(used) & (s >= EXP_PER_GROUP))
    def _():
        o_ref[...] = jnp.zeros(o_ref.shape, o_ref.dtype)

    @pl.when(used & (j == last_j) & (s == last_s))
    def _():
        _gather_tile(hp_hbm, src_ref, j + 1, tm, gbuf, sem, wait=True)


def _moe(hp, src, tile_group, tile_used, w_gate, w_up, w_down, l, tm):
    T, width = hp.shape
    half = width - LANES
    D = 2 * half
    n_tiles = tile_group.shape[0]
    L, E, _, F = w_gate.shape
    tn = 1024
    w_down_g = w_down.reshape(L * N_GROUPS, EXP_PER_GROUP, F, D)
    assert src.shape[0] == (n_tiles + 1) * tm

    def expert(j, s, src, tg, tv):
        return tg[j] * EXP_PER_GROUP + jnp.minimum(s, EXP_PER_GROUP - 1)

    def col(j, s, src, tg, tv):
        return jnp.maximum(s - EXP_PER_GROUP, 0)

    return pl.pallas_call(
        functools.partial(_moe_kernel, tm=tm, half=half),
        grid_spec=pltpu.PrefetchScalarGridSpec(
            num_scalar_prefetch=3,
            grid=(n_tiles, EXP_PER_GROUP + D // tn),
            in_specs=[pl.BlockSpec(memory_space=pl.ANY),
                      pl.BlockSpec((1, 1, D, F), lambda j, s, *p: (l, expert(j, s, *p), 0, 0)),
                      pl.BlockSpec((1, 1, D, F), lambda j, s, *p: (l, expert(j, s, *p), 0, 0)),
                      pl.BlockSpec((1, EXP_PER_GROUP, F, tn), lambda j, s, *p: (l * N_GROUPS + p[1][j], 0, 0, col(j, s, *p)))],
            out_specs=pl.BlockSpec((tm, tn), lambda j, s, *p: (j, col(j, s, *p))),
            scratch_shapes=[pltpu.VMEM((2, tm, width), jnp.uint32), pltpu.SemaphoreType.DMA((2,)),
                            pltpu.VMEM((tm, D), _BF), pltpu.VMEM((tm, LANES), _F32),
                            pltpu.VMEM((EXP_PER_GROUP, tm, F), _BF)]),
        out_shape=jax.ShapeDtypeStruct((n_tiles * tm, D), _F32),
        compiler_params=_params(("arbitrary", "arbitrary"), 48),
        name="moe",
    )(src, tile_group, tile_used, hp, w_gate, w_up, w_down_g)


def _rot_cols(w):
    q = HEAD_ROPE // 4
    a, b, c, d = w[..., :q], w[..., q:2 * q], w[..., 2 * q:3 * q], w[..., 3 * q:]
    return jnp.concatenate([-b, a, -d, c], axis=-1)


def _prep_w_in(w_in, D):
    q, kv = D // 4, D // 8
    o1, o2, o3 = q, q + kv, q + kv + HEAD_ROPE
    kr = w_in[..., o2:o3]
    pieces = [w_in[..., :o1], w_in[..., o3:], w_in[..., o1:o2], kr, _rot_cols(kr)]
    width = sum(p.shape[-1] for p in pieces)
    total = width + (-width) % 512
    out, off = None, 0
    for p in pieces:
        cfg = [(0, 0, 0)] * (p.ndim - 1) + [(off, total - off - p.shape[-1], 0)]
        term = lax.pad(p, jnp.zeros((), p.dtype), cfg)
        out = term if out is None else out + term
        off += p.shape[-1]
    return out.astype(_BF)


def _prep_w_uq(w_uq, n_heads):
    L, r, _ = w_uq.shape
    w = w_uq.reshape(L, r, n_heads, HEAD_V + HEAD_ROPE)
    rope = w[..., HEAD_V:]
    return jnp.concatenate([w[..., :HEAD_V], rope, _rot_cols(rope)], axis=-1).reshape(L, r, n_heads * HEAD_QK).astype(_BF)


def _prep_w_ukv(w_ukv, n_heads):
    L, r, _ = w_ukv.shape
    w = w_ukv.reshape(L, r, n_heads, 2 * HEAD_V)
    return jnp.concatenate([w[..., :HEAD_V].reshape(L, r, -1), w[..., HEAD_V:].reshape(L, r, -1)], axis=-1).astype(_BF)


def _rope_table(n_batch, seq, ctx):
    half = HEAD_ROPE // 2
    inv_freq = ROPE_THETA ** (-jnp.arange(0, half, 2, dtype=_F32) / half)
    pos = jnp.arange(seq)
    ang_r = (pos // GRID_W).astype(_F32)[:, None] * inv_freq
    ang_c = (pos % GRID_W).astype(_F32)[:, None] * inv_freq
    ang = jnp.concatenate([ang_r, ang_r, ang_c, ang_c], axis=-1)
    lat = jnp.concatenate([jnp.cos(ang), jnp.sin(ang)], axis=-1)
    ctx_rows = jnp.concatenate([jnp.ones((n_batch * ctx, HEAD_ROPE), _F32), jnp.zeros((n_batch * ctx, HEAD_ROPE), _F32)], axis=-1)
    return jnp.concatenate([jnp.tile(lat, (n_batch, 1)), ctx_rows], axis=0)


def kernel(x, c, ctx, c_ctx, w_ada, b_ada, w_in, q_norm_g, kv_norm_g, w_uq, w_ukv, w_fourier, w_spatial, b_spatial, w_out, ln1_g, ln1_b, w_router_group, b_router_group, w_router_expert, b_router_expert, w_gate, w_up, w_down, ln2_g, ln2_b):
    B, SEQ, D = x.shape
    CTX = ctx.shape[1]
    L = w_ada.shape[0]
    n_heads = (D // 2) // HEAD_V
    q_rank, kv_rank = D // 4, D // 8
    fd = D // 16
    alpha = (2.0 * L) ** 0.25
    n_seg = B + 1
    assert n_seg <= MOD_ROWS and SEQ % 256 == 0 and (B * CTX) % 256 == 0 and CTX == 256 and (B * SEQ) % CTX == 0
    moe_tm = 512

    f_off, u_off, ckv_off, kr_off = q_rank, 2 * q_rank, D, D + kv_rank

    w_in_p = _prep_w_in(w_in, D)
    w_uq_p = _prep_w_uq(w_uq, n_heads)
    w_ukv_p = _prep_w_ukv(w_ukv, n_heads)
    w_out_b = w_out.astype(_BF)
    w_f_b = w_fourier.astype(_BF)
    w_s_b = w_spatial.astype(_BF)
    w_gate_b = w_gate.astype(_BF)
    w_up_b = w_up.astype(_BF)
    w_down_b = w_down.astype(_BF)
    n_logits = N_GROUPS + N_GROUPS * EXP_PER_GROUP
    w_r = jnp.concatenate([w_router_group, w_router_expert, jnp.zeros((L, D, LANES - n_logits), _F32)], axis=-1).astype(_BF)
    b_r = jnp.concatenate([b_router_group, b_router_expert, jnp.zeros((L, LANES - n_logits), _F32)], axis=-1).reshape(L, 1, LANES)
    cs_tab = _rope_table(B, SEQ, CTX)
    c_lat, s_lat = _dft_mats(SEQ)
    c_ctx_m, s_ctx_m = _dft_mats(CTX)
    c_ch, s_ch = _dft_mats(fd)
    cs_ch = jnp.concatenate([c_ch, s_ch], axis=1)

    c_rows = jnp.concatenate([c, c_ctx[None, :], jnp.zeros((MOD_ROWS - n_seg, D), _F32)], axis=0)
    mod_all = _adaln(c_rows, w_ada, b_ada)

    xt = (x.reshape(B * SEQ, D), ctx.reshape(B * CTX, D))
    mod3 = [mod_all[l].reshape(MOD_ROWS * 6, 1, D) for l in range(L)]
    h = _modulate(xt, mod3[0], 1, 0, SEQ, B)

    for l in range(L):
        last = l == L - 1
        y_in = _matmul(h, w_in_p, l, _F32, "in_proj")
        q = _q_proj(y_in, q_norm_g[l], w_uq_p, l, cs_tab, q_rank, math.log2(math.e) / math.sqrt(HEAD_V + HEAD_ROPE))
        k, v1 = _kv_proj(y_in, kv_norm_g[l], w_ukv_p, l, cs_tab, kv_rank, ckv_off, kr_off, n_heads)
        att = _attention(q, k, v1, B, SEQ, CTX, n_heads)
        pc, ps = _dft_channels(y_in, cs_ch, f_off, fd)
        fmix = (_dft_positions(pc, ps, c_lat, s_lat, w_f_b[l], B, SEQ, 0, fd),
                _dft_positions(pc, ps, c_ctx_m, s_ctx_m, w_f_b[l], B, CTX, B * SEQ, fd))
        cmix = _chunk_mix(y_in, w_s_b[l], b_spatial[l], u_off, q_rank)
        mixed = _out_proj(att, fmix, cmix, w_out_b, l)
        x1, h2p, rec = _resid_ln_route(xt, mixed, mod3[l], 2, ln1_g[l], ln1_b[l], 4, 3, w_r, b_r, l, SEQ, B, alpha)
        dest, src, tile_group, tile_used = _route_plan(rec, moe_tm)
        ys = _moe(h2p, src, tile_group, tile_used, w_gate_b, w_up_b, w_down_b, l, moe_tm)
        x2, h = _resid_ln_gather(x1, ys, dest, mod3[l], 5, ln2_g[l], ln2_b[l], 1, 0, SEQ, B, alpha,
                                 emit_h=not last, mod_next=mod3[min(l + 1, L - 1)],
                                 n_rows=B * SEQ if last else B * (SEQ + CTX))
        xt = (x2,)

    return xt[0].reshape(B, SEQ, D)
```

```python
import functools
import math

import jax
import jax.numpy as jnp
from jax import lax
from jax.experimental import pallas as pl
from jax.experimental.pallas import tpu as pltpu

_BF = jnp.bfloat16
_F32 = jnp.float32

ROPE_THETA = 10000.0
EPS = 1e-6
GRID_W = 64
HEAD_V = 128
HEAD_ROPE = 64
HEAD_QK = 256
F_GROUPS = 4
C_HEADS = 4
CHUNK = 128
N_GROUPS = 4
EXP_PER_GROUP = 4
LANES = 128
MOD_ROWS = 8
MOE_TN = 1024

_MIB = 1024 * 1024


def _params(sem, vmem_mib):
    return pltpu.CompilerParams(dimension_semantics=sem, vmem_limit_bytes=vmem_mib * _MIB)


def _tile(n, prefs):
    for t in prefs:
        if n % t == 0:
            return t
    raise ValueError(f"no tile in {prefs} divides {n}")


def _adaln_kernel(c_ref, w_ref, b_ref, o_ref):
    s = jax.nn.silu(c_ref[...]).astype(_BF)
    o_ref[0] = jnp.dot(s, w_ref[0].astype(_BF), preferred_element_type=_F32) + b_ref[0]


def _adaln(c_rows, w_ada, b_ada):
    L, D, N = w_ada.shape
    tn = 512
    return pl.pallas_call(
        _adaln_kernel,
        grid=(L, N // tn),
        in_specs=[pl.BlockSpec((MOD_ROWS, D), lambda l, j: (0, 0)),
                  pl.BlockSpec((1, D, tn), lambda l, j: (l, 0, j)),
                  pl.BlockSpec((1, 1, tn), lambda l, j: (l, 0, j))],
        out_specs=pl.BlockSpec((1, MOD_ROWS, tn), lambda l, j: (l, 0, j)),
        out_shape=jax.ShapeDtypeStruct((L, MOD_ROWS, N), _F32),
        compiler_params=_params(("parallel", "parallel"), 40),
        name="adaln",
    )(c_rows, w_ada, b_ada.reshape(L, 1, N))


def _mod_spec(chunk, D, tiles_per_seq, n_batch):
    return pl.BlockSpec((1, 1, D), lambda i, *_: (jnp.minimum(i // tiles_per_seq, n_batch) * 6 + chunk, 0, 0))


def _row_specs(parts, tm, width):
    if len(parts) == 1:
        return [pl.BlockSpec((tm, width), lambda i, *_: (i, 0))]
    n_lat = parts[0].shape[0] // tm
    return [pl.BlockSpec((tm, width), lambda i, *_: (jnp.minimum(i, n_lat - 1), 0)),
            pl.BlockSpec((tm, width), lambda i, *_: (jnp.maximum(i - n_lat, 0), 0))]


def _pick_rows(refs, is_lat):
    return refs[0][...] if len(refs) == 1 else jnp.where(is_lat, refs[0][...], refs[1][...])


def _modulate_kernel(*refs, n_x, n_lat):
    x_refs, (sc_ref, sh_ref, o_ref) = refs[:n_x], refs[n_x:]
    x = _pick_rows(x_refs, pl.program_id(0) < n_lat)
    o_ref[...] = (x * (1.0 + sc_ref[0]) + sh_ref[0]).astype(_BF)


def _modulate(xs, mod, sc_chunk, sh_chunk, seq, n_batch):
    D = xs[0].shape[1]
    T = sum(p.shape[0] for p in xs)
    tm = _tile(seq, (256,))
    tps = seq // tm
    return pl.pallas_call(
        functools.partial(_modulate_kernel, n_x=len(xs), n_lat=xs[0].shape[0] // tm),
        grid=(T // tm,),
        in_specs=_row_specs(xs, tm, D) + [_mod_spec(sc_chunk, D, tps, n_batch), _mod_spec(sh_chunk, D, tps, n_batch)],
        out_specs=pl.BlockSpec((tm, D), lambda i: (i, 0)),
        out_shape=jax.ShapeDtypeStruct((T, D), _BF),
        compiler_params=_params(("parallel",), 32),
        name="modulate",
    )(*xs, mod, mod)


def _pack_halves(hb):
    half = hb.shape[1] // 2
    bits = pltpu.bitcast(hb.astype(_F32), jnp.uint32)
    return (bits[:, half:] & jnp.uint32(0xFFFF0000)) | (bits[:, :half] >> 16)


def _unpack_halves(u):
    lo = pltpu.bitcast(u << 16, _F32).astype(_BF)
    hi = pltpu.bitcast(u & jnp.uint32(0xFFFF0000), _F32).astype(_BF)
    return lo, hi


def _ln_tail(y, g_ref, b_ref, sc_ref, sh_ref, xo_ref, ho_ref):
    mu = jnp.mean(y, axis=-1, keepdims=True)
    d = y - mu
    var = jnp.mean(d * d, axis=-1, keepdims=True)
    xn = d * lax.rsqrt(var + EPS) * g_ref[...] + b_ref[...]
    xo_ref[...] = xn
    if sc_ref is None:
        return None
    hb = (xn * (1.0 + sc_ref[0]) + sh_ref[0]).astype(_BF)
    if ho_ref is not None:
        ho_ref[...] = hb
    return hb


def _resid_ln_route_kernel(*refs, n_x, n_lat, alpha):
    x_refs = refs[:n_x]
    a_ref, gate_ref, g_ref, b_ref, sc_ref, sh_ref, wr_ref, br_ref, xo_ref, hp_ref, r_ref = refs[n_x:]
    x = _pick_rows(x_refs, pl.program_id(0) < n_lat)
    y = alpha * x + gate_ref[0] * a_ref[...].astype(_F32)
    hb = _ln_tail(y, g_ref, b_ref, sc_ref, sh_ref, xo_ref, None)
    rec = _route(jnp.dot(hb, wr_ref[0], preferred_element_type=_F32) + br_ref[0])
    half = hb.shape[1] // 2
    hp_ref[:, :half] = _pack_halves(hb)
    hp_ref[:, half:] = pltpu.bitcast(rec, jnp.uint32)
    r_ref[...] = rec


def _resid_ln_route(xs, a, mod, gate_chunk, ln_g, ln_b, sc_chunk, sh_chunk, w_r, b_r, l, seq, n_batch, alpha):
    T, D = a.shape
    tm = _tile(seq, (256,))
    tps = seq // tm
    row = pl.BlockSpec((tm, D), lambda i: (i, 0))
    vec = pl.BlockSpec((1, D), lambda i: (0, 0))
    return pl.pallas_call(
        functools.partial(_resid_ln_route_kernel, n_x=len(xs), n_lat=xs[0].shape[0] // tm, alpha=alpha),
        grid=(T // tm,),
        in_specs=_row_specs(xs, tm, D) + [row, _mod_spec(gate_chunk, D, tps, n_batch), vec, vec,
                                          _mod_spec(sc_chunk, D, tps, n_batch), _mod_spec(sh_chunk, D, tps, n_batch),
                                          pl.BlockSpec((1, D, LANES), lambda i: (l, 0, 0)),
                                          pl.BlockSpec((1, 1, LANES), lambda i: (l, 0, 0))],
        out_specs=[row, pl.BlockSpec((tm, D // 2 + LANES), lambda i: (i, 0)),
                   pl.BlockSpec((tm, LANES), lambda i: (i, 0))],
        out_shape=[jax.ShapeDtypeStruct((T, D), _F32), jax.ShapeDtypeStruct((T, D // 2 + LANES), jnp.uint32),
                   jax.ShapeDtypeStruct((T, LANES), _F32)],
        compiler_params=_params(("parallel",), 40 + 8 * len(xs)),
        name="resid_ln_route",
    )(*xs, a, mod, ln_g.reshape(1, D), ln_b.reshape(1, D), mod, mod, w_r, b_r)


def _gather_tile(src_hbm, idx_ref, tile, tm, buf, sem, wait, rows=None, inline=False):
    slot = tile % 2
    lo, n = rows if rows is not None else (0, tm)
    def _step(r, carry):
        cp = pltpu.make_async_copy(src_hbm.at[pl.ds(idx_ref[tile * tm + r], 1)], buf.at[slot, pl.ds(r, 1)],
                                   sem.at[slot])
        if wait:
            cp.wait()
        else:
            cp.start()
        return carry

    if inline:
        for r in range(n):
            _step(lo + r, 0)
    else:
        lax.fori_loop(lo, lo + n, _step, 0, unroll=8)


def _gather_pipelined(src_hbm, idx_ref, tile, n_tiles, tm, buf, sem):
    @pl.when(tile == 0)
    def _():
        _gather_tile(src_hbm, idx_ref, tile, tm, buf, sem, wait=False)

    _gather_tile(src_hbm, idx_ref, tile, tm, buf, sem, wait=True)

    @pl.when(tile + 1 < n_tiles)
    def _():
        _gather_tile(src_hbm, idx_ref, tile + 1, tm, buf, sem, wait=False)


def _resid_ln_gather_kernel(idx_ref, x_ref, ys_hbm, gate_ref, g_ref, b_ref, sc_ref, sh_ref, *rest, alpha, tm, emit_h):
    if emit_h:
        xo_ref, ho_ref, buf, sem = rest
    else:
        xo_ref, buf, sem = rest
        ho_ref = sc_ref = sh_ref = None
    i = pl.program_id(0)
    _gather_pipelined(ys_hbm, idx_ref, i, pl.num_programs(0), tm, buf, sem)
    rows = buf[i % 2]
    pieces = []
    for c in range(rows.shape[1] // (MOE_TN // 2)):
        pieces.extend(_unpack_halves(rows[:, c * (MOE_TN // 2):(c + 1) * (MOE_TN // 2)]))
    a = jnp.concatenate(pieces, axis=1).astype(_F32)
    y = alpha * x_ref[...] + gate_ref[0] * a
    _ln_tail(y, g_ref, b_ref, sc_ref, sh_ref, xo_ref, ho_ref)


def _resid_ln_gather(x, ys, dest, mod, gate_chunk, ln_g, ln_b, sc_chunk, sh_chunk, seq, n_batch, alpha, emit_h, mod_next,
                     n_rows):
    T, D = n_rows, x.shape[1]
    tm = _tile(seq, (256,))
    tps = seq // tm
    row = pl.BlockSpec((tm, D), lambda i, idx: (i, 0))
    vec = pl.BlockSpec((1, D), lambda i, idx: (0, 0))
    out_specs = [row, row] if emit_h else [row]
    out_shape = [jax.ShapeDtypeStruct((T, D), _F32)] + ([jax.ShapeDtypeStruct((T, D), _BF)] if emit_h else [])
    res = pl.pallas_call(
        functools.partial(_resid_ln_gather_kernel, alpha=alpha, tm=tm, emit_h=emit_h),
        grid_spec=pltpu.PrefetchScalarGridSpec(
            num_scalar_prefetch=1,
            grid=(T // tm,),
            in_specs=[row, pl.BlockSpec(memory_space=pl.ANY), _mod_spec(gate_chunk, D, tps, n_batch), vec, vec,
                      _mod_spec(sc_chunk, D, tps, n_batch), _mod_spec(sh_chunk, D, tps, n_batch)],
            out_specs=out_specs,
            scratch_shapes=[pltpu.VMEM((2, tm, D // 2), jnp.uint32), pltpu.SemaphoreType.DMA((2,))]),
        out_shape=out_shape,
        compiler_params=_params(("arbitrary",), 40),
        name="resid_ln_gather",
    )(dest, x, ys, mod, ln_g.reshape(1, D), ln_b.reshape(1, D), mod_next, mod_next)
    return res if emit_h else (res[0], None)


def _mm_kernel(a_ref, b_ref, o_ref):
    o_ref[...] = jnp.dot(a_ref[...], b_ref[0], preferred_element_type=_F32).astype(o_ref.dtype)


def _matmul(a, b, l, out_dtype, name):
    M, K = a.shape
    N = b.shape[2]
    tm = _tile(M, (1088, 768, 512, 256))
    tn = 512
    return pl.pallas_call(
        _mm_kernel,
        grid=(M // tm, N // tn),
        in_specs=[pl.BlockSpec((tm, K), lambda i, j: (i, 0)), pl.BlockSpec((1, K, tn), lambda i, j: (l, 0, j))],
        out_specs=pl.BlockSpec((tm, tn), lambda i, j: (i, j)),
        out_shape=jax.ShapeDtypeStruct((M, N), out_dtype),
        compiler_params=_params(("parallel", "parallel"), 48),
        name=name,
    )(a, b)


def _mm3_kernel(al_ref, ac_ref, fl_ref, fc_ref, c_ref, b1_ref, b2_ref, b3_ref, o_ref, *, n_lat):
    is_lat = pl.program_id(1) < n_lat
    acc = jnp.dot(_pick_rows((al_ref, ac_ref), is_lat), b1_ref[0], preferred_element_type=_F32)
    acc = acc + jnp.dot(_pick_rows((fl_ref, fc_ref), is_lat), b2_ref[0], preferred_element_type=_F32)
    acc = acc + jnp.dot(c_ref[...], b3_ref[0], preferred_element_type=_F32)
    o_ref[...] = acc.astype(o_ref.dtype)


def _out_proj(att, fmix, cmix, w_out, l):
    M = cmix.shape[0]
    k1, k2, k3 = att[0].shape[1], fmix[0].shape[1], cmix.shape[1]
    N = w_out.shape[2]
    assert k2 == k3 and k1 % k2 == 0
    tm = _tile(att[1].shape[0], (512, 256))
    tn = 1024
    n_lat = att[0].shape[0] // tm

    def rows(parts, width):
        return [pl.BlockSpec((tm, width), lambda j, i: (jnp.minimum(i, n_lat - 1), 0)),
                pl.BlockSpec((tm, width), lambda j, i: (jnp.maximum(i - n_lat, 0), 0))]

    return pl.pallas_call(
        functools.partial(_mm3_kernel, n_lat=n_lat),
        grid=(N // tn, M // tm),
        in_specs=rows(att, k1) + rows(fmix, k2) + [
            pl.BlockSpec((tm, k3), lambda j, i: (i, 0)),
            pl.BlockSpec((1, k1, tn), lambda j, i: (l, 0, j)),
            pl.BlockSpec((1, k2, tn), lambda j, i: (l, k1 // k2, j)),
            pl.BlockSpec((1, k3, tn), lambda j, i: (l, k1 // k2 + 1, j))],
        out_specs=pl.BlockSpec((tm, tn), lambda j, i: (i, j)),
        out_shape=jax.ShapeDtypeStruct((M, N), _BF),
        compiler_params=_params(("parallel", "parallel"), 48),
        name="out_proj",
    )(*att, *fmix, cmix, w_out, w_out, w_out)


def _rms_bf16(x, g):
    return (x * lax.rsqrt(jnp.mean(x * x, axis=-1, keepdims=True) + EPS) * g).astype(_BF)


def _rope_pair(t):
    return t + pltpu.roll(t, HEAD_ROPE, 1)


def _qproj_kernel(cq_ref, g_ref, w_ref, cs_ref, o_ref, *, heads, scale):
    xn = _rms_bf16(cq_ref[...].astype(_F32), g_ref[...])
    y = jnp.dot(xn, w_ref[0], preferred_element_type=_F32)
    cs = cs_ref[...]
    for h in range(heads):
        lo = h * HEAD_QK
        o_ref[:, lo:lo + HEAD_V] = (y[:, lo:lo + HEAD_V] * scale).astype(_BF)
        t = y[:, lo + HEAD_V:lo + HEAD_QK] * cs
        o_ref[:, lo + HEAD_V:lo + HEAD_QK] = (_rope_pair(t) * scale).astype(_BF)


def _q_proj(y_in, g_q, w_uq_p, l, cs, q_rank, scale):
    T = y_in.shape[0]
    N = w_uq_p.shape[2]
    tm = _tile(T, (512, 256))
    hb = 4
    tn = hb * HEAD_QK
    return pl.pallas_call(
        functools.partial(_qproj_kernel, heads=hb, scale=scale),
        grid=(T // tm, N // tn),
        in_specs=[pl.BlockSpec((tm, q_rank), lambda i, j: (i, 0)),
                  pl.BlockSpec((1, q_rank), lambda i, j: (0, 0)),
                  pl.BlockSpec((1, q_rank, tn), lambda i, j: (l, 0, j)),
                  pl.BlockSpec((tm, LANES), lambda i, j: (i, 0))],
        out_specs=pl.BlockSpec((tm, tn), lambda i, j: (i, j)),
        out_shape=jax.ShapeDtypeStruct((T, N), _BF),
        compiler_params=_params(("parallel", "parallel"), 32),
        name="q_proj",
    )(y_in, g_q.reshape(1, q_rank), w_uq_p, cs)


def _kvproj_kernel(ckv_ref, g_ref, wk_ref, wv_ref, kr_ref, cs_ref, k_ref, v_ref, *, heads):
    xn = _rms_bf16(ckv_ref[...].astype(_F32), g_ref[...])
    kn = jnp.dot(xn, wk_ref[0], preferred_element_type=_F32)
    v = jnp.dot(xn, wv_ref[0], preferred_element_type=_F32)
    t = kr_ref[...].astype(_F32) * cs_ref[...]
    lane = lax.broadcasted_iota(jnp.int32, t.shape, 1)
    krf = jnp.where(lane < HEAD_ROPE, _rope_pair(t), 0.0).astype(_BF)
    ones = jnp.ones((t.shape[0], HEAD_V), _BF)
    for h in range(heads):
        k_ref[:, h * HEAD_QK:h * HEAD_QK + HEAD_V] = kn[:, h * HEAD_V:(h + 1) * HEAD_V].astype(_BF)
        k_ref[:, h * HEAD_QK + HEAD_V:(h + 1) * HEAD_QK] = krf
        v_ref[:, h * HEAD_QK:h * HEAD_QK + HEAD_V] = v[:, h * HEAD_V:(h + 1) * HEAD_V].astype(_BF)
        v_ref[:, h * HEAD_QK + HEAD_V:(h + 1) * HEAD_QK] = ones


def _kv_proj(y_in, g_kv, w_ukv_p, l, cs, kv_rank, ckv_off, kr_off, n_heads):
    T = y_in.shape[0]
    tm = _tile(T, (512, 256))
    hb = 4
    nb = n_heads // hb
    return pl.pallas_call(
        functools.partial(_kvproj_kernel, heads=hb),
        grid=(T // tm, nb),
        in_specs=[pl.BlockSpec((tm, kv_rank), lambda i, j: (i, ckv_off // kv_rank)),
                  pl.BlockSpec((1, kv_rank), lambda i, j: (0, 0)),
                  pl.BlockSpec((1, kv_rank, hb * HEAD_V), lambda i, j: (l, 0, j)),
                  pl.BlockSpec((1, kv_rank, hb * HEAD_V), lambda i, j: (l, 0, nb + j)),
                  pl.BlockSpec((tm, LANES), lambda i, j: (i, kr_off // LANES)),
                  pl.BlockSpec((tm, LANES), lambda i, j: (i, 0))],
        out_specs=[pl.BlockSpec((tm, hb * HEAD_QK), lambda i, j: (i, j)),
                   pl.BlockSpec((tm, hb * HEAD_QK), lambda i, j: (i, j))],
        out_shape=[jax.ShapeDtypeStruct((T, n_heads * HEAD_QK), _BF),
                   jax.ShapeDtypeStruct((T, n_heads * HEAD_QK), _BF)],
        compiler_params=_params(("parallel", "parallel"), 32),
        name="kv_proj",
    )(y_in, g_kv.reshape(1, kv_rank), w_ukv_p, w_ukv_p, y_in, cs)


_NT = (((1,), (1,)), ((), ()))


def _softmax_step(q, k, v1, m, acc):
    s = lax.dot_general(q, k, _NT, preferred_element_type=_F32)
    s_max = jnp.max(s, axis=-1, keepdims=True)
    m_new = s_max if m is None else jnp.maximum(m, s_max)
    p = jnp.exp2((s - m_new).astype(_BF))
    pv = jnp.dot(p, v1, preferred_element_type=_F32)
    if acc is not None:
        pv = jnp.exp2(m - m_new) * acc + pv
    return m_new, pv


def _attn_finish(acc, o_ref):
    o_ref[...] = (acc[:, :HEAD_V] / acc[:, HEAD_V:]).astype(_BF)


def _attn_lat_kernel(q_ref, kc_ref, vc_ref, kl_ref, vl_ref, o_ref, *, tk, n_chunks):
    q = q_ref[...]
    m, acc = _softmax_step(q, kc_ref[...], vc_ref[...], None, None)
    for c in range(n_chunks):
        m, acc = _softmax_step(q, kl_ref[c * tk:(c + 1) * tk, :], vl_ref[c * tk:(c + 1) * tk, :], m, acc)
    _attn_finish(acc, o_ref)


def _attn_ctx_kernel(q_ref, kc_ref, vc_ref, o_ref):
    _, acc = _softmax_step(q_ref[...], kc_ref[...], vc_ref[...], None, None)
    _attn_finish(acc, o_ref)


def _attention(q, k, v1, n_batch, seq, ctx, n_heads):
    tq = _tile(seq, (1024, 512, 256))
    tk = _tile(seq, (512, 256))
    nlt = seq // tq
    ctx_blk0 = n_batch * seq // ctx
    att_lat = pl.pallas_call(
        functools.partial(_attn_lat_kernel, tk=tk, n_chunks=seq // tk),
        grid=(n_batch, n_heads, nlt),
        in_specs=[pl.BlockSpec((tq, HEAD_QK), lambda b, h, i: (b * nlt + i, h)),
                  pl.BlockSpec((ctx, HEAD_QK), lambda b, h, i: (ctx_blk0 + b, h)),
                  pl.BlockSpec((ctx, HEAD_QK), lambda b, h, i: (ctx_blk0 + b, h)),
                  pl.BlockSpec((seq, HEAD_QK), lambda b, h, i: (b, h)),
                  pl.BlockSpec((seq, HEAD_QK), lambda b, h, i: (b, h))],
        out_specs=pl.BlockSpec((tq, HEAD_V), lambda b, h, i: (b * nlt + i, h)),
        out_shape=jax.ShapeDtypeStruct((n_batch * seq, n_heads * HEAD_V), _BF),
        compiler_params=_params(("parallel", "parallel", "arbitrary"), 48),
        name="attention",
    )(q, k, v1, k, v1)
    att_ctx = pl.pallas_call(
        _attn_ctx_kernel,
        grid=(n_batch, n_heads),
        in_specs=[pl.BlockSpec((ctx, HEAD_QK), lambda b, h: (ctx_blk0 + b, h)),
                  pl.BlockSpec((ctx, HEAD_QK), lambda b, h: (ctx_blk0 + b, h)),
                  pl.BlockSpec((ctx, HEAD_QK), lambda b, h: (ctx_blk0 + b, h))],
        out_specs=pl.BlockSpec((ctx, HEAD_V), lambda b, h: (b, h)),
        out_shape=jax.ShapeDtypeStruct((n_batch * ctx, n_heads * HEAD_V), _BF),
        compiler_params=_params(("parallel", "parallel"), 32),
        name="attention_ctx",
    )(q, k, v1)
    return att_lat, att_ctx


def _dft_mats(n):
    r = 1 << (n.bit_length() // 2)
    j = jnp.arange(n, dtype=jnp.int32)

    def table(k):
        ang = ((k[:, None] * j[None, :]) % n).astype(_F32) * (2.0 * math.pi / n)
        return jnp.cos(ang), jnp.sin(ang)

    ca, sa = table(jnp.arange(n // r, dtype=jnp.int32) * r)
    cb, sb = table(jnp.arange(r, dtype=jnp.int32))
    c = ca[:, None, :] * cb[None, :, :] - sa[:, None, :] * sb[None, :, :]
    s = sa[:, None, :] * cb[None, :, :] + ca[:, None, :] * sb[None, :, :]
    return c.reshape(n, n).astype(_BF), s.reshape(n, n).astype(_BF)


def _dft_ch_kernel(f_ref, cs_ref, pc_ref, ps_ref, *, fd):
    y = jnp.dot(f_ref[...], cs_ref[...], preferred_element_type=_F32)
    pc_ref[...] = y[:, :fd].astype(_BF)
    ps_ref[...] = y[:, fd:].astype(_BF)


def _dft_channels(y_in, cs_mat, f_off, fd):
    T = y_in.shape[0]
    tm = _tile(T, (512, 256))
    out = jax.ShapeDtypeStruct((T, F_GROUPS * fd), _BF)
    return pl.pallas_call(
        functools.partial(_dft_ch_kernel, fd=fd),
        grid=(T // tm, F_GROUPS),
        in_specs=[pl.BlockSpec((tm, fd), lambda i, g: (i, f_off // fd + g)),
                  pl.BlockSpec((fd, 2 * fd), lambda i, g: (0, 0))],
        out_specs=[pl.BlockSpec((tm, fd), lambda i, g: (i, g)), pl.BlockSpec((tm, fd), lambda i, g: (i, g))],
        out_shape=[out, out],
        compiler_params=_params(("parallel", "parallel"), 32),
        name="dft_channels",
    )(y_in, cs_mat)


def _dft_pos_kernel(c_ref, s_ref, pc_ref, ps_ref, wf_ref, o_ref, *, norm, fd, groups):
    acc = jnp.dot(c_ref[...], pc_ref[...], preferred_element_type=_F32)
    acc = acc - jnp.dot(s_ref[...], ps_ref[...], preferred_element_type=_F32)
    fr = (acc * norm).astype(_BF)
    for g in range(groups):
        o_ref[:, g * fd:(g + 1) * fd] = jnp.dot(fr[:, g * fd:(g + 1) * fd], wf_ref[g],
                                                preferred_element_type=_F32).astype(_BF)


def _dft_positions(pc, ps, cmat, smat, w_f, n_batch, n, row0, fd):
    tm = min(512, n)
    gb = 2
    tn = gb * fd
    blk0 = row0 // n
    return pl.pallas_call(
        functools.partial(_dft_pos_kernel, norm=1.0 / math.sqrt(n * fd), fd=fd, groups=gb),
        grid=(n_batch, F_GROUPS // gb, n // tm),
        in_specs=[pl.BlockSpec((tm, n), lambda b, j, m: (m, 0)),
                  pl.BlockSpec((tm, n), lambda b, j, m: (m, 0)),
                  pl.BlockSpec((n, tn), lambda b, j, m: (blk0 + b, j)),
                  pl.BlockSpec((n, tn), lambda b, j, m: (blk0 + b, j)),
                  pl.BlockSpec((gb, fd, fd), lambda b, j, m: (j, 0, 0))],
        out_specs=pl.BlockSpec((tm, tn), lambda b, j, m: (b * (n // tm) + m, j)),
        out_shape=jax.ShapeDtypeStruct((n_batch * n, F_GROUPS * fd), _BF),
        compiler_params=_params(("parallel", "parallel", "arbitrary"), 48),
        name=f"dft_positions_{n}",
    )(cmat, smat, pc, ps, w_f)


def _chunk_kernel(u_ref, v_ref, ws_ref, bs_ref, o_ref, *, heads, cdim, n_chunks):
    u = jax.nn.gelu(u_ref[...].astype(_F32))
    v = jax.nn.gelu(v_ref[...].astype(_F32))
    for h in range(heads):
        vh = v[:, h * cdim:(h + 1) * cdim]
        mu = jnp.mean(vh, axis=-1, keepdims=True)
        d = vh - mu
        var = jnp.mean(d * d, axis=-1, keepdims=True)
        vs = (d * lax.rsqrt(var + EPS)).astype(_BF)
        w = ws_ref[h]
        bias = bs_ref[h]
        for c in range(n_chunks):
            rows = slice(c * CHUNK, (c + 1) * CHUNK)
            sv = jnp.dot(w, vs[rows], preferred_element_type=_F32) + bias
            o_ref[rows, h * cdim:(h + 1) * cdim] = (u[rows, h * cdim:(h + 1) * cdim] * sv).astype(_BF)


def _chunk_mix(y_in, w_s, b_s, u_off, width):
    T = y_in.shape[0]
    tm = _tile(T, (512, 256))
    cdim = width // C_HEADS
    return pl.pallas_call(
        functools.partial(_chunk_kernel, heads=C_HEADS, cdim=cdim, n_chunks=tm // CHUNK),
        grid=(T // tm,),
        in_specs=[pl.BlockSpec((tm, width), lambda i: (i, u_off // width)),
                  pl.BlockSpec((tm, width), lambda i: (i, u_off // width + 1)),
                  pl.BlockSpec((C_HEADS, CHUNK, CHUNK), lambda i: (0, 0, 0)),
                  pl.BlockSpec((C_HEADS, CHUNK, 1), lambda i: (0, 0, 0))],
        out_specs=pl.BlockSpec((tm, width), lambda i: (i, 0)),
        out_shape=jax.ShapeDtypeStruct((T, width), _BF),
        compiler_params=_params(("parallel",), 32),
        name="chunk_mix",
    )(y_in, y_in, w_s, b_s.reshape(C_HEADS, CHUNK, 1))


_R_GIDX = N_GROUPS


def _route(lg):
    lane = lax.broadcasted_iota(jnp.int32, lg.shape, 1)

    def col(j):
        return jnp.sum(jnp.where(lane == j, lg, 0.0), axis=-1, keepdims=True)

    g = [col(j) for j in range(N_GROUPS)]
    gmax = functools.reduce(jnp.maximum, g)
    gi = jnp.full(gmax.shape, N_GROUPS - 1, jnp.int32)
    for j in reversed(range(N_GROUPS - 1)):
        gi = jnp.where(g[j] == gmax, j, gi)
    gw = 1.0 / functools.reduce(lambda a, b: a + b, [jnp.exp(x - gmax) for x in g])

    e = []
    for j in range(EXP_PER_GROUP):
        ej = col(N_GROUPS + j)
        for gg in range(1, N_GROUPS):
            ej = jnp.where(gi == gg, col(N_GROUPS + gg * EXP_PER_GROUP + j), ej)
        e.append(ej)
    v1 = functools.reduce(jnp.maximum, e)
    i1 = jnp.full(v1.shape, EXP_PER_GROUP - 1, jnp.int32)
    for j in reversed(range(EXP_PER_GROUP - 1)):
        i1 = jnp.where(e[j] == v1, j, i1)
    e2 = [jnp.where(i1 == j, -jnp.inf, e[j]) for j in range(EXP_PER_GROUP)]
    v2 = functools.reduce(jnp.maximum, e2)
    i2 = jnp.full(v2.shape, EXP_PER_GROUP - 1, jnp.int32)
    for j in reversed(range(EXP_PER_GROUP - 1)):
        i2 = jnp.where(e2[j] == v2, j, i2)
    t = jnp.exp(v2 - v1)
    w1 = (1.0 / (1.0 + t)) * gw
    w2 = (t / (1.0 + t)) * gw

    rec = jnp.where(lane == _R_GIDX, gi.astype(_F32), 0.0)
    for j in range(EXP_PER_GROUP):
        cw = jnp.where(i1 == j, w1, 0.0) + jnp.where(i2 == j, w2, 0.0)
        rec = rec + jnp.where(lane == j, cw, 0.0)
    return rec


def _route_plan(rec, tm):
    T = rec.shape[0]
    n_tiles = T // tm + N_GROUPS
    gidx = rec[:, _R_GIDX].astype(jnp.int32)
    onehot = (gidx[:, None] == jnp.arange(N_GROUPS, dtype=jnp.int32)[None, :]).astype(jnp.int32)
    rank = jnp.sum((jnp.cumsum(onehot, axis=0) - onehot) * onehot, axis=1)
    counts = jnp.sum(onehot, axis=0)
    tiles_g = (counts + tm - 1) // tm
    tile_end = jnp.cumsum(tiles_g)
    start = (tile_end - tiles_g) * tm
    dest = (jnp.sum(onehot * start[None, :], axis=1) + rank).astype(jnp.int32)
    src = jnp.zeros(((n_tiles + 1) * tm,), jnp.int32).at[dest].set(jnp.arange(T, dtype=jnp.int32))
    tile_id = jnp.arange(n_tiles, dtype=jnp.int32)
    tile_group = jnp.minimum(jnp.sum((tile_id[:, None] >= tile_end[None, :]).astype(jnp.int32), axis=1), N_GROUPS - 1)
    tile_used = (tile_id < tile_end[-1]).astype(jnp.int32)
    return dest, src, tile_group.astype(jnp.int32), tile_used


def _moe_kernel(src_ref, tg_ref, tv_ref, hp_hbm, wg_ref, wu_ref, wd_ref, o_ref, gbuf, sem, x_scr, rec_scr, hid_scr, *, tm, half):
    j, s = pl.program_id(0), pl.program_id(1)
    last_j, last_s = pl.num_programs(0) - 1, pl.num_programs(1) - 1
    used = tv_ref[j] == 1
    fetched = (j == 0) | (tv_ref[jnp.maximum(j - 1, 0)] == 1)
    quarter = tm // EXP_PER_GROUP
    F = wg_ref.shape[3]
    split = (F // 256) * 256
    tail = F - split

    @pl.when((s == 0) & fetched)
    def _():
        @pl.when(j == 0)
        def _():
            _gather_tile(hp_hbm, src_ref, j, tm, gbuf, sem, wait=False)

        _gather_tile(hp_hbm, src_ref, j, tm, gbuf, sem, wait=True)
        rows = gbuf[j % 2]
        lo, hi = _unpack_halves(rows[:, :half])
        x_scr[:, :half] = lo
        x_scr[:, half:] = hi
        rec_scr[...] = pltpu.bitcast(rows[:, half:], _F32)

    @pl.when(used & (s < EXP_PER_GROUP))
    def _():
        _gather_tile(hp_hbm, src_ref, j + 1, tm, gbuf, sem, wait=False, rows=(s * quarter, quarter), inline=True)
        x = x_scr[...]
        if split and 2 * tail == 256:
            w_tail = jnp.concatenate([wg_ref[0, 0, :, split:], wu_ref[0, 0, :, split:]], axis=1)
            t = jnp.dot(x, w_tail, preferred_element_type=_F32)
            gate = jnp.concatenate([jnp.dot(x, wg_ref[0, 0, :, :split], preferred_element_type=_F32), t[:, :tail]], axis=1)
            up = jnp.concatenate([jnp.dot(x, wu_ref[0, 0, :, :split], preferred_element_type=_F32), t[:, tail:]], axis=1)
        else:
            gate = jnp.dot(x, wg_ref[0, 0], preferred_element_type=_F32)
            up = jnp.dot(x, wu_ref[0, 0], preferred_element_type=_F32)
        rec = rec_scr[...]
        cw = rec[:, 0:1]
        for k in range(1, EXP_PER_GROUP):
            cw = jnp.where(s == k, rec[:, k:k + 1], cw)
        hid_scr[s] = ((jax.nn.silu(gate) * up) * cw).astype(_BF)

    @pl.when(used & (s >= EXP_PER_GROUP))
    def _():
        hid = jnp.concatenate([hid_scr[e] for e in range(EXP_PER_GROUP)], axis=1)
        wd = wd_ref[0]
        y = jnp.dot(hid, wd.reshape(wd.shape[0] * wd.shape[1], wd.shape[2]), preferred_element_type=_F32)
        o_ref[...] = _pack_halves(y.astype(_BF))

    @pl.when(jnp.logical_not(used) & (s >= EXP_PER_GROUP))
    def _():
        o_ref[...] = jnp.zeros(o_ref.shape, o_ref.dtype)

    @pl.when(used & (j == last_j) & (s == last_s))
    def _():
        _gather_tile(hp_hbm, src_ref, j + 1, tm, gbuf, sem, wait=True)


def _moe(hp, src, tile_group, tile_used, w_gate, w_up, w_down, l, tm):
    T, width = hp.shape
    half = width - LANES
    D = 2 * half
    n_tiles = tile_group.shape[0]
    L, E, _, F = w_gate.shape
    tn = MOE_TN
    w_down_g = w_down.reshape(L * N_GROUPS, EXP_PER_GROUP, F, D)
    assert src.shape[0] == (n_tiles + 1) * tm

    def expert(j, s, src, tg, tv):
        return tg[j] * EXP_PER_GROUP + jnp.minimum(s, EXP_PER_GROUP - 1)

    def col(j, s, src, tg, tv):
        return jnp.maximum(s - EXP_PER_GROUP, 0)

    return pl.pallas_call(
        functools.partial(_moe_kernel, tm=tm, half=half),
        grid_spec=pltpu.PrefetchScalarGridSpec(
            num_scalar_prefetch=3,
            grid=(n_tiles, EXP_PER_GROUP + D // tn),
            in_specs=[pl.BlockSpec(memory_space=pl.ANY),
                      pl.BlockSpec((1, 1, D, F), lambda j, s, *p: (l, expert(j, s, *p), 0, 0)),
                      pl.BlockSpec((1, 1, D, F), lambda j, s, *p: (l, expert(j, s, *p), 0, 0)),
                      pl.BlockSpec((1, EXP_PER_GROUP, F, tn), lambda j, s, *p: (l * N_GROUPS + p[1][j], 0, 0, col(j, s, *p)))],
            out_specs=pl.BlockSpec((tm, tn // 2), lambda j, s, *p: (j, col(j, s, *p))),
            scratch_shapes=[pltpu.VMEM((2, tm, width), jnp.uint32), pltpu.SemaphoreType.DMA((2,)),
                            pltpu.VMEM((tm, D), _BF), pltpu.VMEM((tm, LANES), _F32),
                            pltpu.VMEM((EXP_PER_GROUP, tm, F), _BF)]),
        out_shape=jax.ShapeDtypeStruct((n_tiles * tm, D // 2), jnp.uint32),
        compiler_params=_params(("arbitrary", "arbitrary"), 48),
        name="moe",
    )(src, tile_group, tile_used, hp, w_gate, w_up, w_down_g)


def _rot_cols(w):
    q = HEAD_ROPE // 4
    a, b, c, d = w[..., :q], w[..., q:2 * q], w[..., 2 * q:3 * q], w[..., 3 * q:]
    return jnp.concatenate([-b, a, -d, c], axis=-1)


def _prep_w_in(w_in, D):
    q, kv = D // 4, D // 8
    o1, o2, o3 = q, q + kv, q + kv + HEAD_ROPE
    kr = w_in[..., o2:o3]
    pieces = [w_in[..., :o1], w_in[..., o3:], w_in[..., o1:o2], kr, _rot_cols(kr)]
    width = sum(p.shape[-1] for p in pieces)
    total = width + (-width) % 512
    out, off = None, 0
    for p in pieces:
        cfg = [(0, 0, 0)] * (p.ndim - 1) + [(off, total - off - p.shape[-1], 0)]
        term = lax.pad(p, jnp.zeros((), p.dtype), cfg)
        out = term if out is None else out + term
        off += p.shape[-1]
    return out.astype(_BF)


def _prep_w_uq(w_uq, n_heads):
    L, r, _ = w_uq.shape
    w = w_uq.reshape(L, r, n_heads, HEAD_V + HEAD_ROPE)
    rope = w[..., HEAD_V:]
    return jnp.concatenate([w[..., :HEAD_V], rope, _rot_cols(rope)], axis=-1).reshape(L, r, n_heads * HEAD_QK).astype(_BF)


def _prep_w_ukv(w_ukv, n_heads):
    L, r, _ = w_ukv.shape
    w = w_ukv.reshape(L, r, n_heads, 2 * HEAD_V)
    return jnp.concatenate([w[..., :HEAD_V].reshape(L, r, -1), w[..., HEAD_V:].reshape(L, r, -1)], axis=-1).astype(_BF)


def _rope_table(n_batch, seq, ctx):
    half = HEAD_ROPE // 2
    inv_freq = ROPE_THETA ** (-jnp.arange(0, half, 2, dtype=_F32) / half)
    pos = jnp.arange(seq)
    ang_r = (pos // GRID_W).astype(_F32)[:, None] * inv_freq
    ang_c = (pos % GRID_W).astype(_F32)[:, None] * inv_freq
    ang = jnp.concatenate([ang_r, ang_r, ang_c, ang_c], axis=-1)
    lat = jnp.concatenate([jnp.cos(ang), jnp.sin(ang)], axis=-1)
    ctx_rows = jnp.concatenate([jnp.ones((n_batch * ctx, HEAD_ROPE), _F32), jnp.zeros((n_batch * ctx, HEAD_ROPE), _F32)], axis=-1)
    return jnp.concatenate([jnp.tile(lat, (n_batch, 1)), ctx_rows], axis=0)


def kernel(x, c, ctx, c_ctx, w_ada, b_ada, w_in, q_norm_g, kv_norm_g, w_uq, w_ukv, w_fourier, w_spatial, b_spatial, w_out, ln1_g, ln1_b, w_router_group, b_router_group, w_router_expert, b_router_expert, w_gate, w_up, w_down, ln2_g, ln2_b):
    B, SEQ, D = x.shape
    CTX = ctx.shape[1]
    L = w_ada.shape[0]
    n_heads = (D // 2) // HEAD_V
    q_rank, kv_rank = D // 4, D // 8
    fd = D // 16
    alpha = (2.0 * L) ** 0.25
    n_seg = B + 1
    assert n_seg <= MOD_ROWS and SEQ % 256 == 0 and (B * CTX) % 256 == 0 and CTX == 256 and (B * SEQ) % CTX == 0
    moe_tm = 512

    f_off, u_off, ckv_off, kr_off = q_rank, 2 * q_rank, D, D + kv_rank

    w_in_p = _prep_w_in(w_in, D)
    w_uq_p = _prep_w_uq(w_uq, n_heads)
    w_ukv_p = _prep_w_ukv(w_ukv, n_heads)
    w_out_b = w_out.astype(_BF)
    w_f_b = w_fourier.astype(_BF)
    w_s_b = w_spatial.astype(_BF)
    w_gate_b = w_gate.astype(_BF)
    w_up_b = w_up.astype(_BF)
    w_down_b = w_down.astype(_BF)
    n_logits = N_GROUPS + N_GROUPS * EXP_PER_GROUP
    w_r = jnp.concatenate([w_router_group, w_router_expert, jnp.zeros((L, D, LANES - n_logits), _F32)], axis=-1).astype(_BF)
    b_r = jnp.concatenate([b_router_group, b_router_expert, jnp.zeros((L, LANES - n_logits), _F32)], axis=-1).reshape(L, 1, LANES)
    cs_tab = _rope_table(B, SEQ, CTX)
    c_lat, s_lat = _dft_mats(SEQ)
    c_ctx_m, s_ctx_m = _dft_mats(CTX)
    c_ch, s_ch = _dft_mats(fd)
    cs_ch = jnp.concatenate([c_ch, s_ch], axis=1)

    c_rows = jnp.concatenate([c, c_ctx[None, :], jnp.zeros((MOD_ROWS - n_seg, D), _F32)], axis=0)
    mod_all = _adaln(c_rows, w_ada, b_ada)

    xt = (x.reshape(B * SEQ, D), ctx.reshape(B * CTX, D))
    mod3 = [mod_all[l].reshape(MOD_ROWS * 6, 1, D) for l in range(L)]
    h = _modulate(xt, mod3[0], 1, 0, SEQ, B)

    for l in range(L):
        last = l == L - 1
        y_in = _matmul(h, w_in_p, l, _BF, "in_proj")
        q = _q_proj(y_in, q_norm_g[l], w_uq_p, l, cs_tab, q_rank, math.log2(math.e) / math.sqrt(HEAD_V + HEAD_ROPE))
        k, v1 = _kv_proj(y_in, kv_norm_g[l], w_ukv_p, l, cs_tab, kv_rank, ckv_off, kr_off, n_heads)
        att = _attention(q, k, v1, B, SEQ, CTX, n_heads)
        pc, ps = _dft_channels(y_in, cs_ch, f_off, fd)
        fmix = (_dft_positions(pc, ps, c_lat, s_lat, w_f_b[l], B, SEQ, 0, fd),
                _dft_positions(pc, ps, c_ctx_m, s_ctx_m, w_f_b[l], B, CTX, B * SEQ, fd))
        cmix = _chunk_mix(y_in, w_s_b[l], b_spatial[l], u_off, q_rank)
        mixed = _out_proj(att, fmix, cmix, w_out_b, l)
        x1, h2p, rec = _resid_ln_route(xt, mixed, mod3[l], 2, ln1_g[l], ln1_b[l], 4, 3, w_r, b_r, l, SEQ, B, alpha)
        dest, src, tile_group, tile_used = _route_plan(rec, moe_tm)
        ys = _moe(h2p, src, tile_group, tile_used, w_gate_b, w_up_b, w_down_b, l, moe_tm)
        x2, h = _resid_ln_gather(x1, ys, dest, mod3[l], 5, ln2_g[l], ln2_b[l], 1, 0, SEQ, B, alpha,
                                 emit_h=not last, mod_next=mod3[min(l + 1, L - 1)],
                                 n_rows=B * SEQ if last else B * (SEQ + CTX))
        xt = (x2,)

    return xt[0].reshape(B, SEQ, D)
```

```python
import functools
import math

import jax
import jax.numpy as jnp
from jax import lax
from jax.experimental import pallas as pl
from jax.experimental.pallas import tpu as pltpu

_BF = jnp.bfloat16
_F32 = jnp.float32

ROPE_THETA = 10000.0
EPS = 1e-6
GRID_W = 64
HEAD_V = 128
HEAD_ROPE = 64
HEAD_QK = 256
F_GROUPS = 4
C_HEADS = 4
CHUNK = 128
N_GROUPS = 4
EXP_PER_GROUP = 4
LANES = 128
MOD_ROWS = 8
MOE_TN = 1024

_MIB = 1024 * 1024


def _params(sem, vmem_mib):
    return pltpu.CompilerParams(dimension_semantics=sem, vmem_limit_bytes=vmem_mib * _MIB)


def _tile(n, prefs):
    for t in prefs:
        if n % t == 0:
            return t
    raise ValueError(f"no tile in {prefs} divides {n}")


def _adaln_kernel(c_ref, w_ref, b_ref, o_ref):
    s = jax.nn.silu(c_ref[...]).astype(_BF)
    o_ref[0] = jnp.dot(s, w_ref[0].astype(_BF), preferred_element_type=_F32) + b_ref[0]


def _adaln(c_rows, w_ada, b_ada):
    L, D, N = w_ada.shape
    tn = 512
    return pl.pallas_call(
        _adaln_kernel,
        grid=(L, N // tn),
        in_specs=[pl.BlockSpec((MOD_ROWS, D), lambda l, j: (0, 0)),
                  pl.BlockSpec((1, D, tn), lambda l, j: (l, 0, j)),
                  pl.BlockSpec((1, 1, tn), lambda l, j: (l, 0, j))],
        out_specs=pl.BlockSpec((1, MOD_ROWS, tn), lambda l, j: (l, 0, j)),
        out_shape=jax.ShapeDtypeStruct((L, MOD_ROWS, N), _F32),
        compiler_params=_params(("parallel", "parallel"), 40),
        name="adaln",
    )(c_rows, w_ada, b_ada.reshape(L, 1, N))


def _mod_spec(chunk, D, tiles_per_seq, n_batch):
    return pl.BlockSpec((1, 1, D), lambda i, *_: (jnp.minimum(i // tiles_per_seq, n_batch) * 6 + chunk, 0, 0))


def _row_specs(parts, tm, width):
    if len(parts) == 1:
        return [pl.BlockSpec((tm, width), lambda i, *_: (i, 0))]
    n_lat = parts[0].shape[0] // tm
    return [pl.BlockSpec((tm, width), lambda i, *_: (jnp.minimum(i, n_lat - 1), 0)),
            pl.BlockSpec((tm, width), lambda i, *_: (jnp.maximum(i - n_lat, 0), 0))]


def _pick_rows(refs, is_lat):
    return refs[0][...] if len(refs) == 1 else jnp.where(is_lat, refs[0][...], refs[1][...])


def _modulate_kernel(*refs, n_x, n_lat):
    x_refs, (sc_ref, sh_ref, o_ref) = refs[:n_x], refs[n_x:]
    x = _pick_rows(x_refs, pl.program_id(0) < n_lat)
    o_ref[...] = (x * (1.0 + sc_ref[0]) + sh_ref[0]).astype(_BF)


def _modulate(xs, mod, sc_chunk, sh_chunk, seq, n_batch):
    D = xs[0].shape[1]
    T = sum(p.shape[0] for p in xs)
    tm = _tile(seq, (256,))
    tps = seq // tm
    return pl.pallas_call(
        functools.partial(_modulate_kernel, n_x=len(xs), n_lat=xs[0].shape[0] // tm),
        grid=(T // tm,),
        in_specs=_row_specs(xs, tm, D) + [_mod_spec(sc_chunk, D, tps, n_batch), _mod_spec(sh_chunk, D, tps, n_batch)],
        out_specs=pl.BlockSpec((tm, D), lambda i: (i, 0)),
        out_shape=jax.ShapeDtypeStruct((T, D), _BF),
        compiler_params=_params(("parallel",), 32),
        name="modulate",
    )(*xs, mod, mod)


def _pack_halves(hb):
    half = hb.shape[1] // 2
    bits = pltpu.bitcast(hb.astype(_F32), jnp.uint32)
    return (bits[:, half:] & jnp.uint32(0xFFFF0000)) | (bits[:, :half] >> 16)


def _unpack_halves(u):
    lo = pltpu.bitcast(u << 16, _F32).astype(_BF)
    hi = pltpu.bitcast(u & jnp.uint32(0xFFFF0000), _F32).astype(_BF)
    return lo, hi


def _ln_tail(y, g_ref, b_ref, sc_ref, sh_ref, xo_ref, ho_ref):
    mu = jnp.mean(y, axis=-1, keepdims=True)
    d = y - mu
    var = jnp.mean(d * d, axis=-1, keepdims=True)
    xn = d * lax.rsqrt(var + EPS) * g_ref[...] + b_ref[...]
    xo_ref[...] = xn
    if sc_ref is None:
        return None
    hb = (xn * (1.0 + sc_ref[0]) + sh_ref[0]).astype(_BF)
    if ho_ref is not None:
        ho_ref[...] = hb
    return hb


def _resid_ln_route_kernel(*refs, n_x, n_lat, alpha):
    x_refs = refs[:n_x]
    a_ref, gate_ref, g_ref, b_ref, sc_ref, sh_ref, wr_ref, br_ref, xo_ref, hp_ref, r_ref = refs[n_x:]
    x = _pick_rows(x_refs, pl.program_id(0) < n_lat)
    y = alpha * x + gate_ref[0] * a_ref[...].astype(_F32)
    hb = _ln_tail(y, g_ref, b_ref, sc_ref, sh_ref, xo_ref, None)
    rec = _route(jnp.dot(hb, wr_ref[0], preferred_element_type=_F32) + br_ref[0])
    half = hb.shape[1] // 2
    hp_ref[:, :half] = _pack_halves(hb)
    hp_ref[:, half:] = pltpu.bitcast(rec, jnp.uint32)
    r_ref[...] = rec


def _resid_ln_route(xs, a, mod, gate_chunk, ln_g, ln_b, sc_chunk, sh_chunk, w_r, b_r, l, seq, n_batch, alpha):
    T, D = a.shape
    tm = _tile(seq, (256,))
    tps = seq // tm
    row = pl.BlockSpec((tm, D), lambda i: (i, 0))
    vec = pl.BlockSpec((1, D), lambda i: (0, 0))
    return pl.pallas_call(
        functools.partial(_resid_ln_route_kernel, n_x=len(xs), n_lat=xs[0].shape[0] // tm, alpha=alpha),
        grid=(T // tm,),
        in_specs=_row_specs(xs, tm, D) + [row, _mod_spec(gate_chunk, D, tps, n_batch), vec, vec,
                                          _mod_spec(sc_chunk, D, tps, n_batch), _mod_spec(sh_chunk, D, tps, n_batch),
                                          pl.BlockSpec((1, D, LANES), lambda i: (l, 0, 0)),
                                          pl.BlockSpec((1, 1, LANES), lambda i: (l, 0, 0))],
        out_specs=[row, pl.BlockSpec((tm, D // 2 + LANES), lambda i: (i, 0)),
                   pl.BlockSpec((tm, LANES), lambda i: (i, 0))],
        out_shape=[jax.ShapeDtypeStruct((T, D), _F32), jax.ShapeDtypeStruct((T, D // 2 + LANES), jnp.uint32),
                   jax.ShapeDtypeStruct((T, LANES), _F32)],
        compiler_params=_params(("parallel",), 40 + 8 * len(xs)),
        name="resid_ln_route",
    )(*xs, a, mod, ln_g.reshape(1, D), ln_b.reshape(1, D), mod, mod, w_r, b_r)


def _gather_tile(src_hbm, idx_ref, tile, tm, buf, sem, wait, rows=None, inline=False):
    slot = tile % 2
    lo, n = rows if rows is not None else (0, tm)
    def _step(r, carry):
        cp = pltpu.make_async_copy(src_hbm.at[pl.ds(idx_ref[tile * tm + r], 1)], buf.at[slot, pl.ds(r, 1)],
                                   sem.at[slot])
        if wait:
            cp.wait()
        else:
            cp.start()
        return carry

    if inline:
        for r in range(n):
            _step(lo + r, 0)
    else:
        lax.fori_loop(lo, lo + n, _step, 0, unroll=8)


def _gather_pipelined(src_hbm, idx_ref, tile, n_tiles, tm, buf, sem):
    @pl.when(tile == 0)
    def _():
        _gather_tile(src_hbm, idx_ref, tile, tm, buf, sem, wait=False)

    _gather_tile(src_hbm, idx_ref, tile, tm, buf, sem, wait=True)

    @pl.when(tile + 1 < n_tiles)
    def _():
        _gather_tile(src_hbm, idx_ref, tile + 1, tm, buf, sem, wait=False)


def _resid_ln_gather_kernel(idx_ref, x_ref, ys_hbm, gate_ref, g_ref, b_ref, sc_ref, sh_ref, *rest, alpha, tm, emit_h):
    if emit_h:
        xo_ref, ho_ref, buf, sem = rest
    else:
        xo_ref, buf, sem = rest
        ho_ref = sc_ref = sh_ref = None
    i = pl.program_id(0)
    _gather_pipelined(ys_hbm, idx_ref, i, pl.num_programs(0), tm, buf, sem)
    rows = buf[i % 2]
    pieces = []
    for c in range(rows.shape[1] // (MOE_TN // 2)):
        pieces.extend(_unpack_halves(rows[:, c * (MOE_TN // 2):(c + 1) * (MOE_TN // 2)]))
    a = jnp.concatenate(pieces, axis=1).astype(_F32)
    y = alpha * x_ref[...] + gate_ref[0] * a
    _ln_tail(y, g_ref, b_ref, sc_ref, sh_ref, xo_ref, ho_ref)


def _resid_ln_gather(x, ys, dest, mod, gate_chunk, ln_g, ln_b, sc_chunk, sh_chunk, seq, n_batch, alpha, emit_h, mod_next,
                     n_rows):
    T, D = n_rows, x.shape[1]
    tm = _tile(seq, (256,))
    tps = seq // tm
    row = pl.BlockSpec((tm, D), lambda i, idx: (i, 0))
    vec = pl.BlockSpec((1, D), lambda i, idx: (0, 0))
    out_specs = [row, row] if emit_h else [row]
    out_shape = [jax.ShapeDtypeStruct((T, D), _F32)] + ([jax.ShapeDtypeStruct((T, D), _BF)] if emit_h else [])
    res = pl.pallas_call(
        functools.partial(_resid_ln_gather_kernel, alpha=alpha, tm=tm, emit_h=emit_h),
        grid_spec=pltpu.PrefetchScalarGridSpec(
            num_scalar_prefetch=1,
            grid=(T // tm,),
            in_specs=[row, pl.BlockSpec(memory_space=pl.ANY), _mod_spec(gate_chunk, D, tps, n_batch), vec, vec,
                      _mod_spec(sc_chunk, D, tps, n_batch), _mod_spec(sh_chunk, D, tps, n_batch)],
            out_specs=out_specs,
            scratch_shapes=[pltpu.VMEM((2, tm, D // 2), jnp.uint32), pltpu.SemaphoreType.DMA((2,))]),
        out_shape=out_shape,
        compiler_params=_params(("arbitrary",), 40),
        name="resid_ln_gather",
    )(dest, x, ys, mod, ln_g.reshape(1, D), ln_b.reshape(1, D), mod_next, mod_next)
    return res if emit_h else (res[0], None)


def _mm_kernel(a_ref, b_ref, o_ref):
    o_ref[...] = jnp.dot(a_ref[...], b_ref[0], preferred_element_type=_F32).astype(o_ref.dtype)


def _matmul(a, b, l, out_dtype, name):
    M, K = a.shape
    N = b.shape[2]
    tm = _tile(M, (1088, 768, 512, 256))
    tn = 512
    return pl.pallas_call(
        _mm_kernel,
        grid=(M // tm, N // tn),
        in_specs=[pl.BlockSpec((tm, K), lambda i, j: (i, 0)), pl.BlockSpec((1, K, tn), lambda i, j: (l, 0, j))],
        out_specs=pl.BlockSpec((tm, tn), lambda i, j: (i, j)),
        out_shape=jax.ShapeDtypeStruct((M, N), out_dtype),
        compiler_params=_params(("parallel", "parallel"), 48),
        name=name,
    )(a, b)


def _mm3_kernel(al_ref, ac_ref, fl_ref, fc_ref, c_ref, b1_ref, b2_ref, b3_ref, o_ref, *, n_lat):
    is_lat = pl.program_id(1) < n_lat
    acc = jnp.dot(_pick_rows((al_ref, ac_ref), is_lat), b1_ref[0], preferred_element_type=_F32)
    acc = acc + jnp.dot(_pick_rows((fl_ref, fc_ref), is_lat), b2_ref[0], preferred_element_type=_F32)
    acc = acc + jnp.dot(c_ref[...], b3_ref[0], preferred_element_type=_F32)
    o_ref[...] = acc.astype(o_ref.dtype)


def _out_proj(att, fmix, cmix, w_out, l):
    M = cmix.shape[0]
    k1, k2, k3 = att[0].shape[1], fmix[0].shape[1], cmix.shape[1]
    N = w_out.shape[2]
    assert k2 == k3 and k1 % k2 == 0
    tm = _tile(att[1].shape[0], (512, 256))
    tn = 1024
    n_lat = att[0].shape[0] // tm

    def rows(parts, width):
        return [pl.BlockSpec((tm, width), lambda j, i: (jnp.minimum(i, n_lat - 1), 0)),
                pl.BlockSpec((tm, width), lambda j, i: (jnp.maximum(i - n_lat, 0), 0))]

    return pl.pallas_call(
        functools.partial(_mm3_kernel, n_lat=n_lat),
        grid=(N // tn, M // tm),
        in_specs=rows(att, k1) + rows(fmix, k2) + [
            pl.BlockSpec((tm, k3), lambda j, i: (i, 0)),
            pl.BlockSpec((1, k1, tn), lambda j, i: (l, 0, j)),
            pl.BlockSpec((1, k2, tn), lambda j, i: (l, k1 // k2, j)),
            pl.BlockSpec((1, k3, tn), lambda j, i: (l, k1 // k2 + 1, j))],
        out_specs=pl.BlockSpec((tm, tn), lambda j, i: (i, j)),
        out_shape=jax.ShapeDtypeStruct((M, N), _BF),
        compiler_params=_params(("parallel", "parallel"), 48),
        name="out_proj",
    )(*att, *fmix, cmix, w_out, w_out, w_out)


def _rms_bf16(x, g):
    return (x * lax.rsqrt(jnp.mean(x * x, axis=-1, keepdims=True) + EPS) * g).astype(_BF)


def _rope_pair(t):
    return t + pltpu.roll(t, HEAD_ROPE, 1)


def _qproj_kernel(cq_ref, g_ref, w_ref, cs_ref, o_ref, *, heads, scale):
    xn = _rms_bf16(cq_ref[...].astype(_F32), g_ref[...])
    y = jnp.dot(xn, w_ref[0], preferred_element_type=_F32)
    cs = cs_ref[...]
    for h in range(heads):
        lo = h * HEAD_QK
        o_ref[:, lo:lo + HEAD_V] = (y[:, lo:lo + HEAD_V] * scale).astype(_BF)
        t = y[:, lo + HEAD_V:lo + HEAD_QK] * cs
        o_ref[:, lo + HEAD_V:lo + HEAD_QK] = (_rope_pair(t) * scale).astype(_BF)


def _q_proj(y_in, g_q, w_uq_p, l, cs, q_rank, scale):
    T = y_in.shape[0]
    N = w_uq_p.shape[2]
    tm = _tile(T, (512, 256))
    hb = 4
    tn = hb * HEAD_QK
    return pl.pallas_call(
        functools.partial(_qproj_kernel, heads=hb, scale=scale),
        grid=(T // tm, N // tn),
        in_specs=[pl.BlockSpec((tm, q_rank), lambda i, j: (i, 0)),
                  pl.BlockSpec((1, q_rank), lambda i, j: (0, 0)),
                  pl.BlockSpec((1, q_rank, tn), lambda i, j: (l, 0, j)),
                  pl.BlockSpec((tm, LANES), lambda i, j: (i, 0))],
        out_specs=pl.BlockSpec((tm, tn), lambda i, j: (i, j)),
        out_shape=jax.ShapeDtypeStruct((T, N), _BF),
        compiler_params=_params(("parallel", "parallel"), 32),
        name="q_proj",
    )(y_in, g_q.reshape(1, q_rank), w_uq_p, cs)


def _kvproj_kernel(ckv_ref, g_ref, wk_ref, wv_ref, kr_ref, cs_ref, k_ref, v_ref, *, heads):
    xn = _rms_bf16(ckv_ref[...].astype(_F32), g_ref[...])
    kn = jnp.dot(xn, wk_ref[0], preferred_element_type=_F32)
    v = jnp.dot(xn, wv_ref[0], preferred_element_type=_F32)
    t = kr_ref[...].astype(_F32) * cs_ref[...]
    lane = lax.broadcasted_iota(jnp.int32, t.shape, 1)
    krf = jnp.where(lane < HEAD_ROPE, _rope_pair(t), 0.0).astype(_BF)
    ones = jnp.ones((t.shape[0], HEAD_V), _BF)
    for h in range(heads):
        k_ref[:, h * HEAD_QK:h * HEAD_QK + HEAD_V] = kn[:, h * HEAD_V:(h + 1) * HEAD_V].astype(_BF)
        k_ref[:, h * HEAD_QK + HEAD_V:(h + 1) * HEAD_QK] = krf
        v_ref[:, h * HEAD_QK:h * HEAD_QK + HEAD_V] = v[:, h * HEAD_V:(h + 1) * HEAD_V].astype(_BF)
        v_ref[:, h * HEAD_QK + HEAD_V:(h + 1) * HEAD_QK] = ones


def _kv_proj(y_in, g_kv, w_ukv_p, l, cs, kv_rank, ckv_off, kr_off, n_heads):
    T = y_in.shape[0]
    tm = _tile(T, (512, 256))
    hb = 4
    nb = n_heads // hb
    return pl.pallas_call(
        functools.partial(_kvproj_kernel, heads=hb),
        grid=(T // tm, nb),
        in_specs=[pl.BlockSpec((tm, kv_rank), lambda i, j: (i, ckv_off // kv_rank)),
                  pl.BlockSpec((1, kv_rank), lambda i, j: (0, 0)),
                  pl.BlockSpec((1, kv_rank, hb * HEAD_V), lambda i, j: (l, 0, j)),
                  pl.BlockSpec((1, kv_rank, hb * HEAD_V), lambda i, j: (l, 0, nb + j)),
                  pl.BlockSpec((tm, LANES), lambda i, j: (i, kr_off // LANES)),
                  pl.BlockSpec((tm, LANES), lambda i, j: (i, 0))],
        out_specs=[pl.BlockSpec((tm, hb * HEAD_QK), lambda i, j: (i, j)),
                   pl.BlockSpec((tm, hb * HEAD_QK), lambda i, j: (i, j))],
        out_shape=[jax.ShapeDtypeStruct((T, n_heads * HEAD_QK), _BF),
                   jax.ShapeDtypeStruct((T, n_heads * HEAD_QK), _BF)],
        compiler_params=_params(("parallel", "parallel"), 32),
        name="kv_proj",
    )(y_in, g_kv.reshape(1, kv_rank), w_ukv_p, w_ukv_p, y_in, cs)


_NT = (((1,), (1,)), ((), ()))


def _softmax_step(q, k, v1, m, acc):
    s = lax.dot_general(q, k, _NT, preferred_element_type=_F32)
    s_max = jnp.max(s, axis=-1, keepdims=True)
    m_new = s_max if m is None else jnp.maximum(m, s_max)
    p = jnp.exp2((s - m_new).astype(_BF))
    pv = jnp.dot(p, v1, preferred_element_type=_F32)
    if acc is not None:
        pv = jnp.exp2(m - m_new) * acc + pv
    return m_new, pv


def _attn_finish(acc, o_ref):
    o_ref[...] = (acc[:, :HEAD_V] / acc[:, HEAD_V:]).astype(_BF)


def _attn_lat_kernel(q_ref, kc_ref, vc_ref, kl_ref, vl_ref, o_ref, *, tk, n_chunks):
    q = q_ref[...]
    m, acc = _softmax_step(q, kc_ref[...], vc_ref[...], None, None)
    for c in range(n_chunks):
        m, acc = _softmax_step(q, kl_ref[c * tk:(c + 1) * tk, :], vl_ref[c * tk:(c + 1) * tk, :], m, acc)
    _attn_finish(acc, o_ref)


def _attn_ctx_kernel(q_ref, kc_ref, vc_ref, o_ref):
    _, acc = _softmax_step(q_ref[...], kc_ref[...], vc_ref[...], None, None)
    _attn_finish(acc, o_ref)


def _attention(q, k, v1, n_batch, seq, ctx, n_heads):
    tq = _tile(seq, (1024, 512, 256))
    tk = _tile(seq, (256,))
    nlt = seq // tq
    ctx_blk0 = n_batch * seq // ctx
    att_lat = pl.pallas_call(
        functools.partial(_attn_lat_kernel, tk=tk, n_chunks=seq // tk),
        grid=(n_batch, n_heads, nlt),
        in_specs=[pl.BlockSpec((tq, HEAD_QK), lambda b, h, i: (b * nlt + i, h)),
                  pl.BlockSpec((ctx, HEAD_QK), lambda b, h, i: (ctx_blk0 + b, h)),
                  pl.BlockSpec((ctx, HEAD_QK), lambda b, h, i: (ctx_blk0 + b, h)),
                  pl.BlockSpec((seq, HEAD_QK), lambda b, h, i: (b, h)),
                  pl.BlockSpec((seq, HEAD_QK), lambda b, h, i: (b, h))],
        out_specs=pl.BlockSpec((tq, HEAD_V), lambda b, h, i: (b * nlt + i, h)),
        out_shape=jax.ShapeDtypeStruct((n_batch * seq, n_heads * HEAD_V), _BF),
        compiler_params=_params(("parallel", "parallel", "arbitrary"), 48),
        name="attention",
    )(q, k, v1, k, v1)
    att_ctx = pl.pallas_call(
        _attn_ctx_kernel,
        grid=(n_batch, n_heads),
        in_specs=[pl.BlockSpec((ctx, HEAD_QK), lambda b, h: (ctx_blk0 + b, h)),
                  pl.BlockSpec((ctx, HEAD_QK), lambda b, h: (ctx_blk0 + b, h)),
                  pl.BlockSpec((ctx, HEAD_QK), lambda b, h: (ctx_blk0 + b, h))],
        out_specs=pl.BlockSpec((ctx, HEAD_V), lambda b, h: (b, h)),
        out_shape=jax.ShapeDtypeStruct((n_batch * ctx, n_heads * HEAD_V), _BF),
        compiler_params=_params(("parallel", "parallel"), 32),
        name="attention_ctx",
    )(q, k, v1)
    return att_lat, att_ctx


def _dft_mats(n):
    r = 1 << (n.bit_length() // 2)
    j = jnp.arange(n, dtype=jnp.int32)

    def table(k):
        ang = ((k[:, None] * j[None, :]) % n).astype(_F32) * (2.0 * math.pi / n)
        return jnp.cos(ang), jnp.sin(ang)

    ca, sa = table(jnp.arange(n // r, dtype=jnp.int32) * r)
    cb, sb = table(jnp.arange(r, dtype=jnp.int32))
    c = ca[:, None, :] * cb[None, :, :] - sa[:, None, :] * sb[None, :, :]
    s = sa[:, None, :] * cb[None, :, :] + ca[:, None, :] * sb[None, :, :]
    return c.reshape(n, n).astype(_BF), s.reshape(n, n).astype(_BF)


def _dft_ch_kernel(f_ref, cs_ref, pc_ref, ps_ref, *, fd):
    y = jnp.dot(f_ref[...], cs_ref[...], preferred_element_type=_F32)
    pc_ref[...] = y[:, :fd].astype(_BF)
    ps_ref[...] = y[:, fd:].astype(_BF)


def _dft_channels(y_in, cs_mat, f_off, fd):
    T = y_in.shape[0]
    tm = _tile(T, (512, 256))
    out = jax.ShapeDtypeStruct((T, F_GROUPS * fd), _BF)
    return pl.pallas_call(
        functools.partial(_dft_ch_kernel, fd=fd),
        grid=(T // tm, F_GROUPS),
        in_specs=[pl.BlockSpec((tm, fd), lambda i, g: (i, f_off // fd + g)),
                  pl.BlockSpec((fd, 2 * fd), lambda i, g: (0, 0))],
        out_specs=[pl.BlockSpec((tm, fd), lambda i, g: (i, g)), pl.BlockSpec((tm, fd), lambda i, g: (i, g))],
        out_shape=[out, out],
        compiler_params=_params(("parallel", "parallel"), 32),
        name="dft_channels",
    )(y_in, cs_mat)


def _dft_pos_kernel(c_ref, s_ref, pc_ref, ps_ref, wf_ref, o_ref, *, norm, fd, groups):
    acc = jnp.dot(c_ref[...], pc_ref[...], preferred_element_type=_F32)
    acc = acc - jnp.dot(s_ref[...], ps_ref[...], preferred_element_type=_F32)
    fr = (acc * norm).astype(_BF)
    for g in range(groups):
        o_ref[:, g * fd:(g + 1) * fd] = jnp.dot(fr[:, g * fd:(g + 1) * fd], wf_ref[g],
                                                preferred_element_type=_F32).astype(_BF)


def _dft_positions(pc, ps, cmat, smat, w_f, n_batch, n, row0, fd):
    tm = min(512, n)
    gb = 2
    tn = gb * fd
    blk0 = row0 // n
    return pl.pallas_call(
        functools.partial(_dft_pos_kernel, norm=1.0 / math.sqrt(n * fd), fd=fd, groups=gb),
        grid=(n_batch, F_GROUPS // gb, n // tm),
        in_specs=[pl.BlockSpec((tm, n), lambda b, j, m: (m, 0)),
                  pl.BlockSpec((tm, n), lambda b, j, m: (m, 0)),
                  pl.BlockSpec((n, tn), lambda b, j, m: (blk0 + b, j)),
                  pl.BlockSpec((n, tn), lambda b, j, m: (blk0 + b, j)),
                  pl.BlockSpec((gb, fd, fd), lambda b, j, m: (j, 0, 0))],
        out_specs=pl.BlockSpec((tm, tn), lambda b, j, m: (b * (n // tm) + m, j)),
        out_shape=jax.ShapeDtypeStruct((n_batch * n, F_GROUPS * fd), _BF),
        compiler_params=_params(("parallel", "parallel", "arbitrary"), 48),
        name=f"dft_positions_{n}",
    )(cmat, smat, pc, ps, w_f)


def _chunk_kernel(u_ref, v_ref, ws_ref, bs_ref, o_ref, *, heads, cdim, n_chunks):
    u = jax.nn.gelu(u_ref[...].astype(_F32))
    v = jax.nn.gelu(v_ref[...].astype(_F32))
    for h in range(heads):
        vh = v[:, h * cdim:(h + 1) * cdim]
        mu = jnp.mean(vh, axis=-1, keepdims=True)
        d = vh - mu
        var = jnp.mean(d * d, axis=-1, keepdims=True)
        vs = (d * lax.rsqrt(var + EPS)).astype(_BF)
        w = ws_ref[h]
        bias = bs_ref[h]
        for c in range(n_chunks):
            rows = slice(c * CHUNK, (c + 1) * CHUNK)
            sv = jnp.dot(w, vs[rows], preferred_element_type=_F32) + bias
            o_ref[rows, h * cdim:(h + 1) * cdim] = (u[rows, h * cdim:(h + 1) * cdim] * sv).astype(_BF)


def _chunk_mix(y_in, w_s, b_s, u_off, width):
    T = y_in.shape[0]
    tm = _tile(T, (512, 256))
    cdim = width // C_HEADS
    return pl.pallas_call(
        functools.partial(_chunk_kernel, heads=C_HEADS, cdim=cdim, n_chunks=tm // CHUNK),
        grid=(T // tm,),
        in_specs=[pl.BlockSpec((tm, width), lambda i: (i, u_off // width)),
                  pl.BlockSpec((tm, width), lambda i: (i, u_off // width + 1)),
                  pl.BlockSpec((C_HEADS, CHUNK, CHUNK), lambda i: (0, 0, 0)),
                  pl.BlockSpec((C_HEADS, CHUNK, 1), lambda i: (0, 0, 0))],
        out_specs=pl.BlockSpec((tm, width), lambda i: (i, 0)),
        out_shape=jax.ShapeDtypeStruct((T, width), _BF),
        compiler_params=_params(("parallel",), 32),
        name="chunk_mix",
    )(y_in, y_in, w_s, b_s.reshape(C_HEADS, CHUNK, 1))


_R_GIDX = N_GROUPS


def _route(lg):
    lane = lax.broadcasted_iota(jnp.int32, lg.shape, 1)

    def col(j):
        return jnp.sum(jnp.where(lane == j, lg, 0.0), axis=-1, keepdims=True)

    g = [col(j) for j in range(N_GROUPS)]
    gmax = functools.reduce(jnp.maximum, g)
    gi = jnp.full(gmax.shape, N_GROUPS - 1, jnp.int32)
    for j in reversed(range(N_GROUPS - 1)):
        gi = jnp.where(g[j] == gmax, j, gi)
    gw = 1.0 / functools.reduce(lambda a, b: a + b, [jnp.exp(x - gmax) for x in g])

    e = []
    for j in range(EXP_PER_GROUP):
        ej = col(N_GROUPS + j)
        for gg in range(1, N_GROUPS):
            ej = jnp.where(gi == gg, col(N_GROUPS + gg * EXP_PER_GROUP + j), ej)
        e.append(ej)
    v1 = functools.reduce(jnp.maximum, e)
    i1 = jnp.full(v1.shape, EXP_PER_GROUP - 1, jnp.int32)
    for j in reversed(range(EXP_PER_GROUP - 1)):
        i1 = jnp.where(e[j] == v1, j, i1)
    e2 = [jnp.where(i1 == j, -jnp.inf, e[j]) for j in range(EXP_PER_GROUP)]
    v2 = functools.reduce(jnp.maximum, e2)
    i2 = jnp.full(v2.shape, EXP_PER_GROUP - 1, jnp.int32)
    for j in reversed(range(EXP_PER_GROUP - 1)):
        i2 = jnp.where(e2[j] == v2, j, i2)
    t = jnp.exp(v2 - v1)
    w1 = (1.0 / (1.0 + t)) * gw
    w2 = (t / (1.0 + t)) * gw

    rec = jnp.where(lane == _R_GIDX, gi.astype(_F32), 0.0)
    for j in range(EXP_PER_GROUP):
        cw = jnp.where(i1 == j, w1, 0.0) + jnp.where(i2 == j, w2, 0.0)
        rec = rec + jnp.where(lane == j, cw, 0.0)
    return rec


def _route_plan(rec, tm):
    T = rec.shape[0]
    n_tiles = T // tm + N_GROUPS
    gidx = rec[:, _R_GIDX].astype(jnp.int32)
    onehot = (gidx[:, None] == jnp.arange(N_GROUPS, dtype=jnp.int32)[None, :]).astype(jnp.int32)
    rank = jnp.sum((jnp.cumsum(onehot, axis=0) - onehot) * onehot, axis=1)
    counts = jnp.sum(onehot, axis=0)
    tiles_g = (counts + tm - 1) // tm
    tile_end = jnp.cumsum(tiles_g)
    start = (tile_end - tiles_g) * tm
    dest = (jnp.sum(onehot * start[None, :], axis=1) + rank).astype(jnp.int32)
    src = jnp.zeros(((n_tiles + 1) * tm,), jnp.int32).at[dest].set(jnp.arange(T, dtype=jnp.int32))
    tile_id = jnp.arange(n_tiles, dtype=jnp.int32)
    tile_group = jnp.minimum(jnp.sum((tile_id[:, None] >= tile_end[None, :]).astype(jnp.int32), axis=1), N_GROUPS - 1)
    tile_used = (tile_id < tile_end[-1]).astype(jnp.int32)
    return dest, src, tile_group.astype(jnp.int32), tile_used


def _moe_kernel(src_ref, tg_ref, tv_ref, hp_hbm, wg_ref, wu_ref, wd_ref, o_ref, gbuf, sem, x_scr, rec_scr, hid_scr, *, tm, half):
    j, s = pl.program_id(0), pl.program_id(1)
    last_j, last_s = pl.num_programs(0) - 1, pl.num_programs(1) - 1
    used = tv_ref[j] == 1
    fetched = (j == 0) | (tv_ref[jnp.maximum(j - 1, 0)] == 1)
    quarter = tm // EXP_PER_GROUP
    F = wg_ref.shape[3]
    split = (F // 256) * 256
    tail = F - split

    @pl.when((s == 0) & fetched)
    def _():
        @pl.when(j == 0)
        def _():
            _gather_tile(hp_hbm, src_ref, j, tm, gbuf, sem, wait=False)

        _gather_tile(hp_hbm, src_ref, j, tm, gbuf, sem, wait=True)
        rows = gbuf[j % 2]
        lo, hi = _unpack_halves(rows[:, :half])
        x_scr[:, :half] = lo
        x_scr[:, half:] = hi
        rec_scr[...] = pltpu.bitcast(rows[:, half:], _F32)

    @pl.when(used & (s < EXP_PER_GROUP))
    def _():
        _gather_tile(hp_hbm, src_ref, j + 1, tm, gbuf, sem, wait=False, rows=(s * quarter, quarter), inline=True)
        x = x_scr[...]
        if split and 2 * tail == 256:
            w_tail = jnp.concatenate([wg_ref[0, 0, :, split:], wu_ref[0, 0, :, split:]], axis=1)
            t = jnp.dot(x, w_tail, preferred_element_type=_F32)
            gate = jnp.concatenate([jnp.dot(x, wg_ref[0, 0, :, :split], preferred_element_type=_F32), t[:, :tail]], axis=1)
            up = jnp.concatenate([jnp.dot(x, wu_ref[0, 0, :, :split], preferred_element_type=_F32), t[:, tail:]], axis=1)
        else:
            gate = jnp.dot(x, wg_ref[0, 0], preferred_element_type=_F32)
            up = jnp.dot(x, wu_ref[0, 0], preferred_element_type=_F32)
        rec = rec_scr[...]
        cw = rec[:, 0:1]
        for k in range(1, EXP_PER_GROUP):
            cw = jnp.where(s == k, rec[:, k:k + 1], cw)
        hid_scr[s] = ((jax.nn.silu(gate) * up) * cw).astype(_BF)

    @pl.when(used & (s >= EXP_PER_GROUP))
    def _():
        hid = jnp.concatenate([hid_scr[e] for e in range(EXP_PER_GROUP)], axis=1)
        wd = wd_ref[0]
        y = jnp.dot(hid, wd.reshape(wd.shape[0] * wd.shape[1], wd.shape[2]), preferred_element_type=_F32)
        o_ref[...] = _pack_halves(y.astype(_BF))

    @pl.when(jnp.logical_not(used) & (s >= EXP_PER_GROUP))
    def _():
        o_ref[...] = jnp.zeros(o_ref.shape, o_ref.dtype)

    @pl.when(used & (j == last_j) & (s == last_s))
    def _():
        _gather_tile(hp_hbm, src_ref, j + 1, tm, gbuf, sem, wait=True)


def _moe(hp, src, tile_group, tile_used, w_gate, w_up, w_down, l, tm):
    T, width = hp.shape
    half = width - LANES
    D = 2 * half
    n_tiles = tile_group.shape[0]
    L, E, _, F = w_gate.shape
    tn = MOE_TN
    w_down_g = w_down.reshape(L * N_GROUPS, EXP_PER_GROUP, F, D)
    assert src.shape[0] == (n_tiles + 1) * tm

    def expert(j, s, src, tg, tv):
        return tg[j] * EXP_PER_GROUP + jnp.minimum(s, EXP_PER_GROUP - 1)

    def col(j, s, src, tg, tv):
        return jnp.maximum(s - EXP_PER_GROUP, 0)

    return pl.pallas_call(
        functools.partial(_moe_kernel, tm=tm, half=half),
        grid_spec=pltpu.PrefetchScalarGridSpec(
            num_scalar_prefetch=3,
            grid=(n_tiles, EXP_PER_GROUP + D // tn),
            in_specs=[pl.BlockSpec(memory_space=pl.ANY),
                      pl.BlockSpec((1, 1, D, F), lambda j, s, *p: (l, expert(j, s, *p), 0, 0)),
                      pl.BlockSpec((1, 1, D, F), lambda j, s, *p: (l, expert(j, s, *p), 0, 0)),
                      pl.BlockSpec((1, EXP_PER_GROUP, F, tn), lambda j, s, *p: (l * N_GROUPS + p[1][j], 0, 0, col(j, s, *p)))],
            out_specs=pl.BlockSpec((tm, tn // 2), lambda j, s, *p: (j, col(j, s, *p))),
            scratch_shapes=[pltpu.VMEM((2, tm, width), jnp.uint32), pltpu.SemaphoreType.DMA((2,)),
                            pltpu.VMEM((tm, D), _BF), pltpu.VMEM((tm, LANES), _F32),
                            pltpu.VMEM((EXP_PER_GROUP, tm, F), _BF)]),
        out_shape=jax.ShapeDtypeStruct((n_tiles * tm, D // 2), jnp.uint32),
        compiler_params=_params(("arbitrary", "arbitrary"), 48),
        name="moe",
    )(src, tile_group, tile_used, hp, w_gate, w_up, w_down_g)


def _rot_cols(w):
    q = HEAD_ROPE // 4
    a, b, c, d = w[..., :q], w[..., q:2 * q], w[..., 2 * q:3 * q], w[..., 3 * q:]
    return jnp.concatenate([-b, a, -d, c], axis=-1)


def _w_in_relayout_kernel(w_ref, tail_ref, o_ref, *, o1, o2, o3):
    w = w_ref[0]
    n_in = w.shape[1]
    rest = n_in - o3
    o_ref[0, :, :o1] = w[:, :o1].astype(_BF)
    o_ref[0, :, o1:o1 + rest] = w[:, o3:].astype(_BF)
    o_ref[0, :, o1 + rest:o1 + rest + (o2 - o1)] = w[:, o1:o2].astype(_BF)
    used = o1 + rest + (o2 - o1)
    o_ref[0, :, used:used + LANES] = tail_ref[0]
    o_ref[0, :, used + LANES:] = jnp.zeros((w.shape[0], o_ref.shape[2] - used - LANES), _BF)


def _prep_w_in(w_in, D):
    L, _, n_in = w_in.shape
    q, kv = D // 4, D // 8
    o1, o2, o3 = q, q + kv, q + kv + HEAD_ROPE
    kr = w_in[..., o2:o3]
    tail = jnp.concatenate([kr, _rot_cols(kr)], axis=-1).astype(_BF)
    width = n_in + HEAD_ROPE
    total = width + (-width) % 512
    tk = 256
    return pl.pallas_call(
        functools.partial(_w_in_relayout_kernel, o1=o1, o2=o2, o3=o3),
        grid=(L, D // tk),
        in_specs=[pl.BlockSpec((1, tk, n_in), lambda l, i: (l, i, 0)),
                  pl.BlockSpec((1, tk, LANES), lambda l, i: (l, i, 0))],
        out_specs=pl.BlockSpec((1, tk, total), lambda l, i: (l, i, 0)),
        out_shape=jax.ShapeDtypeStruct((L, D, total), _BF),
        compiler_params=_params(("parallel", "parallel"), 40),
        name="w_in_relayout",
    )(w_in, tail)


def _prep_w_uq(w_uq, n_heads):
    L, r, _ = w_uq.shape
    w = w_uq.reshape(L, r, n_heads, HEAD_V + HEAD_ROPE)
    rope = w[..., HEAD_V:]
    return jnp.concatenate([w[..., :HEAD_V], rope, _rot_cols(rope)], axis=-1).reshape(L, r, n_heads * HEAD_QK).astype(_BF)


def _prep_w_ukv(w_ukv, n_heads):
    L, r, _ = w_ukv.shape
    w = w_ukv.reshape(L, r, n_heads, 2 * HEAD_V)
    return jnp.concatenate([w[..., :HEAD_V].reshape(L, r, -1), w[..., HEAD_V:].reshape(L, r, -1)], axis=-1).astype(_BF)


def _rope_table(n_batch, seq, ctx):
    half = HEAD_ROPE // 2
    inv_freq = ROPE_THETA ** (-jnp.arange(0, half, 2, dtype=_F32) / half)
    pos = jnp.arange(seq)
    ang_r = (pos // GRID_W).astype(_F32)[:, None] * inv_freq
    ang_c = (pos % GRID_W).astype(_F32)[:, None] * inv_freq
    ang = jnp.concatenate([ang_r, ang_r, ang_c, ang_c], axis=-1)
    lat = jnp.concatenate([jnp.cos(ang), jnp.sin(ang)], axis=-1)
    ctx_rows = jnp.concatenate([jnp.ones((n_batch * ctx, HEAD_ROPE), _F32), jnp.zeros((n_batch * ctx, HEAD_ROPE), _F32)], axis=-1)
    return jnp.concatenate([jnp.tile(lat, (n_batch, 1)), ctx_rows], axis=0)


def kernel(x, c, ctx, c_ctx, w_ada, b_ada, w_in, q_norm_g, kv_norm_g, w_uq, w_ukv, w_fourier, w_spatial, b_spatial, w_out, ln1_g, ln1_b, w_router_group, b_router_group, w_router_expert, b_router_expert, w_gate, w_up, w_down, ln2_g, ln2_b):
    B, SEQ, D = x.shape
    CTX = ctx.shape[1]
    L = w_ada.shape[0]
    n_heads = (D // 2) // HEAD_V
    q_rank, kv_rank = D // 4, D // 8
    fd = D // 16
    alpha = (2.0 * L) ** 0.25
    n_seg = B + 1
    assert n_seg <= MOD_ROWS and SEQ % 256 == 0 and (B * CTX) % 256 == 0 and CTX == 256 and (B * SEQ) % CTX == 0
    moe_tm = 512

    f_off, u_off, ckv_off, kr_off = q_rank, 2 * q_rank, D, D + kv_rank

    w_in_p = _prep_w_in(w_in, D)
    w_uq_p = _prep_w_uq(w_uq, n_heads)
    w_ukv_p = _prep_w_ukv(w_ukv, n_heads)
    w_out_b = w_out.astype(_BF)
    w_f_b = w_fourier.astype(_BF)
    w_s_b = w_spatial.astype(_BF)
    w_gate_b = w_gate.astype(_BF)
    w_up_b = w_up.astype(_BF)
    w_down_b = w_down.astype(_BF)
    n_logits = N_GROUPS + N_GROUPS * EXP_PER_GROUP
    w_r = jnp.concatenate([w_router_group, w_router_expert, jnp.zeros((L, D, LANES - n_logits), _F32)], axis=-1).astype(_BF)
    b_r = jnp.concatenate([b_router_group, b_router_expert, jnp.zeros((L, LANES - n_logits), _F32)], axis=-1).reshape(L, 1, LANES)
    cs_tab = _rope_table(B, SEQ, CTX)
    c_lat, s_lat = _dft_mats(SEQ)
    c_ctx_m, s_ctx_m = _dft_mats(CTX)
    c_ch, s_ch = _dft_mats(fd)
    cs_ch = jnp.concatenate([c_ch, s_ch], axis=1)

    c_rows = jnp.concatenate([c, c_ctx[None, :], jnp.zeros((MOD_ROWS - n_seg, D), _F32)], axis=0)
    mod_all = _adaln(c_rows, w_ada, b_ada)

    xt = (x.reshape(B * SEQ, D), ctx.reshape(B * CTX, D))
    mod3 = [mod_all[l].reshape(MOD_ROWS * 6, 1, D) for l in range(L)]
    h = _modulate(xt, mod3[0], 1, 0, SEQ, B)

    for l in range(L):
        last = l == L - 1
        y_in = _matmul(h, w_in_p, l, _BF, "in_proj")
        q = _q_proj(y_in, q_norm_g[l], w_uq_p, l, cs_tab, q_rank, math.log2(math.e) / math.sqrt(HEAD_V + HEAD_ROPE))
        k, v1 = _kv_proj(y_in, kv_norm_g[l], w_ukv_p, l, cs_tab, kv_rank, ckv_off, kr_off, n_heads)
        att = _attention(q, k, v1, B, SEQ, CTX, n_heads)
        pc, ps = _dft_channels(y_in, cs_ch, f_off, fd)
        fmix = (_dft_positions(pc, ps, c_lat, s_lat, w_f_b[l], B, SEQ, 0, fd),
                _dft_positions(pc, ps, c_ctx_m, s_ctx_m, w_f_b[l], B, CTX, B * SEQ, fd))
        cmix = _chunk_mix(y_in, w_s_b[l], b_spatial[l], u_off, q_rank)
        mixed = _out_proj(att, fmix, cmix, w_out_b, l)
        x1, h2p, rec = _resid_ln_route(xt, mixed, mod3[l], 2, ln1_g[l], ln1_b[l], 4, 3, w_r, b_r, l, SEQ, B, alpha)
        dest, src, tile_group, tile_used = _route_plan(rec, moe_tm)
        ys = _moe(h2p, src, tile_group, tile_used, w_gate_b, w_up_b, w_down_b, l, moe_tm)
        x2, h = _resid_ln_gather(x1, ys, dest, mod3[l], 5, ln2_g[l], ln2_b[l], 1, 0, SEQ, B, alpha,
                                 emit_h=not last, mod_next=mod3[min(l + 1, L - 1)],
                                 n_rows=B * SEQ if last else B * (SEQ + CTX))
        xt = (x2,)

    return xt[0].reshape(B, SEQ, D)
```

```python
import functools
import math

import jax
import jax.numpy as jnp
from jax import lax
from jax.experimental import pallas as pl
from jax.experimental.pallas import tpu as pltpu

_BF = jnp.bfloat16
_F32 = jnp.float32

ROPE_THETA = 10000.0
EPS = 1e-6
GRID_W = 64
HEAD_V = 128
HEAD_ROPE = 64
HEAD_QK = 256
F_GROUPS = 4
C_HEADS = 4
CHUNK = 128
N_GROUPS = 4
EXP_PER_GROUP = 4
LANES = 128
MOD_ROWS = 8
MOE_TN = 2048

_MIB = 1024 * 1024


def _params(sem, vmem_mib):
    return pltpu.CompilerParams(dimension_semantics=sem, vmem_limit_bytes=vmem_mib * _MIB)


def _tile(n, prefs):
    for t in prefs:
        if n % t == 0:
            return t
    raise ValueError(f"no tile in {prefs} divides {n}")


def _adaln_kernel(c_ref, w_ref, b_ref, o_ref):
    s = jax.nn.silu(c_ref[...]).astype(_BF)
    o_ref[0] = jnp.dot(s, w_ref[0].astype(_BF), preferred_element_type=_F32) + b_ref[0]


def _adaln(c_rows, w_ada, b_ada):
    L, D, N = w_ada.shape
    tn = 512
    return pl.pallas_call(
        _adaln_kernel,
        grid=(L, N // tn),
        in_specs=[pl.BlockSpec((MOD_ROWS, D), lambda l, j: (0, 0)),
                  pl.BlockSpec((1, D, tn), lambda l, j: (l, 0, j)),
                  pl.BlockSpec((1, 1, tn), lambda l, j: (l, 0, j))],
        out_specs=pl.BlockSpec((1, MOD_ROWS, tn), lambda l, j: (l, 0, j)),
        out_shape=jax.ShapeDtypeStruct((L, MOD_ROWS, N), _F32),
        compiler_params=_params(("parallel", "parallel"), 40),
        name="adaln",
    )(c_rows, w_ada, b_ada.reshape(L, 1, N))


def _mod_spec(chunk, D, tiles_per_seq, n_batch):
    return pl.BlockSpec((1, 1, D), lambda i, *_: (jnp.minimum(i // tiles_per_seq, n_batch) * 6 + chunk, 0, 0))


def _row_specs(parts, tm, width):
    if len(parts) == 1:
        return [pl.BlockSpec((tm, width), lambda i, *_: (i, 0))]
    n_lat = parts[0].shape[0] // tm
    return [pl.BlockSpec((tm, width), lambda i, *_: (jnp.minimum(i, n_lat - 1), 0)),
            pl.BlockSpec((tm, width), lambda i, *_: (jnp.maximum(i - n_lat, 0), 0))]


def _pick_rows(refs, is_lat):
    return refs[0][...] if len(refs) == 1 else jnp.where(is_lat, refs[0][...], refs[1][...])


def _modulate_kernel(*refs, n_x, n_lat):
    x_refs, (sc_ref, sh_ref, o_ref) = refs[:n_x], refs[n_x:]
    x = _pick_rows(x_refs, pl.program_id(0) < n_lat)
    o_ref[...] = (x * (1.0 + sc_ref[0]) + sh_ref[0]).astype(_BF)


def _modulate(xs, mod, sc_chunk, sh_chunk, seq, n_batch):
    D = xs[0].shape[1]
    T = sum(p.shape[0] for p in xs)
    tm = _tile(seq, (256,))
    tps = seq // tm
    return pl.pallas_call(
        functools.partial(_modulate_kernel, n_x=len(xs), n_lat=xs[0].shape[0] // tm),
        grid=(T // tm,),
        in_specs=_row_specs(xs, tm, D) + [_mod_spec(sc_chunk, D, tps, n_batch), _mod_spec(sh_chunk, D, tps, n_batch)],
        out_specs=pl.BlockSpec((tm, D), lambda i: (i, 0)),
        out_shape=jax.ShapeDtypeStruct((T, D), _BF),
        compiler_params=_params(("parallel",), 32),
        name="modulate",
    )(*xs, mod, mod)


def _pack_halves(hb):
    half = hb.shape[1] // 2
    bits = pltpu.bitcast(hb.astype(_F32), jnp.uint32)
    return (bits[:, half:] & jnp.uint32(0xFFFF0000)) | (bits[:, :half] >> 16)


def _unpack_halves(u):
    lo = pltpu.bitcast(u << 16, _F32).astype(_BF)
    hi = pltpu.bitcast(u & jnp.uint32(0xFFFF0000), _F32).astype(_BF)
    return lo, hi


def _ln_tail(y, g_ref, b_ref, sc_ref, sh_ref, xo_ref, ho_ref):
    mu = jnp.mean(y, axis=-1, keepdims=True)
    d = y - mu
    var = jnp.mean(d * d, axis=-1, keepdims=True)
    xn = d * lax.rsqrt(var + EPS) * g_ref[...] + b_ref[...]
    xo_ref[...] = xn
    if sc_ref is None:
        return None
    hb = (xn * (1.0 + sc_ref[0]) + sh_ref[0]).astype(_BF)
    if ho_ref is not None:
        ho_ref[...] = hb
    return hb


def _resid_ln_route_kernel(*refs, n_x, n_lat, alpha):
    x_refs = refs[:n_x]
    a_ref, gate_ref, g_ref, b_ref, sc_ref, sh_ref, wr_ref, br_ref, xo_ref, hp_ref, r_ref = refs[n_x:]
    x = _pick_rows(x_refs, pl.program_id(0) < n_lat)
    y = alpha * x + gate_ref[0] * a_ref[...].astype(_F32)
    hb = _ln_tail(y, g_ref, b_ref, sc_ref, sh_ref, xo_ref, None)
    rec = _route(jnp.dot(hb, wr_ref[0], preferred_element_type=_F32) + br_ref[0])
    half = hb.shape[1] // 2
    hp_ref[:, :half] = _pack_halves(hb)
    hp_ref[:, half:] = pltpu.bitcast(rec, jnp.uint32)
    r_ref[...] = rec


def _resid_ln_route(xs, a, mod, gate_chunk, ln_g, ln_b, sc_chunk, sh_chunk, w_r, b_r, l, seq, n_batch, alpha):
    T, D = a.shape
    tm = _tile(seq, (256,))
    tps = seq // tm
    row = pl.BlockSpec((tm, D), lambda i: (i, 0))
    vec = pl.BlockSpec((1, D), lambda i: (0, 0))
    return pl.pallas_call(
        functools.partial(_resid_ln_route_kernel, n_x=len(xs), n_lat=xs[0].shape[0] // tm, alpha=alpha),
        grid=(T // tm,),
        in_specs=_row_specs(xs, tm, D) + [row, _mod_spec(gate_chunk, D, tps, n_batch), vec, vec,
                                          _mod_spec(sc_chunk, D, tps, n_batch), _mod_spec(sh_chunk, D, tps, n_batch),
                                          pl.BlockSpec((1, D, LANES), lambda i: (l, 0, 0)),
                                          pl.BlockSpec((1, 1, LANES), lambda i: (l, 0, 0))],
        out_specs=[row, pl.BlockSpec((tm, D // 2 + LANES), lambda i: (i, 0)),
                   pl.BlockSpec((tm, LANES), lambda i: (i, 0))],
        out_shape=[jax.ShapeDtypeStruct((T, D), _F32), jax.ShapeDtypeStruct((T, D // 2 + LANES), jnp.uint32),
                   jax.ShapeDtypeStruct((T, LANES), _F32)],
        compiler_params=_params(("parallel",), 40 + 8 * len(xs)),
        name="resid_ln_route",
    )(*xs, a, mod, ln_g.reshape(1, D), ln_b.reshape(1, D), mod, mod, w_r, b_r)


def _gather_tile(src_hbm, idx_ref, tile, tm, buf, sem, wait, rows=None, inline=False):
    slot = tile % 2
    lo, n = rows if rows is not None else (0, tm)
    def _step(r, carry):
        cp = pltpu.make_async_copy(src_hbm.at[pl.ds(idx_ref[tile * tm + r], 1)], buf.at[slot, pl.ds(r, 1)],
                                   sem.at[slot])
        if wait:
            cp.wait()
        else:
            cp.start()
        return carry

    if inline:
        for r in range(n):
            _step(lo + r, 0)
    else:
        lax.fori_loop(lo, lo + n, _step, 0, unroll=8)


def _gather_pipelined(src_hbm, idx_ref, tile, n_tiles, tm, buf, sem):
    @pl.when(tile == 0)
    def _():
        _gather_tile(src_hbm, idx_ref, tile, tm, buf, sem, wait=False)

    _gather_tile(src_hbm, idx_ref, tile, tm, buf, sem, wait=True)

    @pl.when(tile + 1 < n_tiles)
    def _():
        _gather_tile(src_hbm, idx_ref, tile + 1, tm, buf, sem, wait=False)


def _resid_ln_gather_kernel(idx_ref, x_ref, ys_hbm, gate_ref, g_ref, b_ref, sc_ref, sh_ref, *rest, alpha, tm, emit_h):
    if emit_h:
        xo_ref, ho_ref, buf, sem = rest
    else:
        xo_ref, buf, sem = rest
        ho_ref = sc_ref = sh_ref = None
    i = pl.program_id(0)
    _gather_pipelined(ys_hbm, idx_ref, i, pl.num_programs(0), tm, buf, sem)
    rows = buf[i % 2]
    pieces = []
    for c in range(rows.shape[1] // (MOE_TN // 2)):
        pieces.extend(_unpack_halves(rows[:, c * (MOE_TN // 2):(c + 1) * (MOE_TN // 2)]))
    a = jnp.concatenate(pieces, axis=1).astype(_F32)
    y = alpha * x_ref[...] + gate_ref[0] * a
    _ln_tail(y, g_ref, b_ref, sc_ref, sh_ref, xo_ref, ho_ref)


def _resid_ln_gather(x, ys, dest, mod, gate_chunk, ln_g, ln_b, sc_chunk, sh_chunk, seq, n_batch, alpha, emit_h, mod_next,
                     n_rows):
    T, D = n_rows, x.shape[1]
    tm = _tile(seq, (256,))
    tps = seq // tm
    row = pl.BlockSpec((tm, D), lambda i, idx: (i, 0))
    vec = pl.BlockSpec((1, D), lambda i, idx: (0, 0))
    out_specs = [row, row] if emit_h else [row]
    out_shape = [jax.ShapeDtypeStruct((T, D), _F32)] + ([jax.ShapeDtypeStruct((T, D), _BF)] if emit_h else [])
    res = pl.pallas_call(
        functools.partial(_resid_ln_gather_kernel, alpha=alpha, tm=tm, emit_h=emit_h),
        grid_spec=pltpu.PrefetchScalarGridSpec(
            num_scalar_prefetch=1,
            grid=(T // tm,),
            in_specs=[row, pl.BlockSpec(memory_space=pl.ANY), _mod_spec(gate_chunk, D, tps, n_batch), vec, vec,
                      _mod_spec(sc_chunk, D, tps, n_batch), _mod_spec(sh_chunk, D, tps, n_batch)],
            out_specs=out_specs,
            scratch_shapes=[pltpu.VMEM((2, tm, D // 2), jnp.uint32), pltpu.SemaphoreType.DMA((2,))]),
        out_shape=out_shape,
        compiler_params=_params(("arbitrary",), 40),
        name="resid_ln_gather",
    )(dest, x, ys, mod, ln_g.reshape(1, D), ln_b.reshape(1, D), mod_next, mod_next)
    return res if emit_h else (res[0], None)


def _mm_kernel(a_ref, b_ref, o_ref):
    o_ref[...] = jnp.dot(a_ref[...], b_ref[0], preferred_element_type=_F32).astype(o_ref.dtype)


def _matmul(a, b, l, out_dtype, name):
    M, K = a.shape
    N = b.shape[2]
    tm = _tile(M, (1088, 768, 512, 256))
    tn = 512
    return pl.pallas_call(
        _mm_kernel,
        grid=(M // tm, N // tn),
        in_specs=[pl.BlockSpec((tm, K), lambda i, j: (i, 0)), pl.BlockSpec((1, K, tn), lambda i, j: (l, 0, j))],
        out_specs=pl.BlockSpec((tm, tn), lambda i, j: (i, j)),
        out_shape=jax.ShapeDtypeStruct((M, N), out_dtype),
        compiler_params=_params(("parallel", "parallel"), 48),
        name=name,
    )(a, b)


def _mm3_kernel(al_ref, ac_ref, fl_ref, fc_ref, c_ref, b1_ref, b2_ref, b3_ref, o_ref, *, n_lat):
    is_lat = pl.program_id(1) < n_lat
    acc = jnp.dot(_pick_rows((al_ref, ac_ref), is_lat), b1_ref[0], preferred_element_type=_F32)
    acc = acc + jnp.dot(_pick_rows((fl_ref, fc_ref), is_lat), b2_ref[0], preferred_element_type=_F32)
    acc = acc + jnp.dot(c_ref[...], b3_ref[0], preferred_element_type=_F32)
    o_ref[...] = acc.astype(o_ref.dtype)


def _out_proj(att, fmix, cmix, w_out, l):
    M = cmix.shape[0]
    k1, k2, k3 = att[0].shape[1], fmix[0].shape[1], cmix.shape[1]
    N = w_out.shape[2]
    assert k2 == k3 and k1 % k2 == 0
    tm = _tile(att[1].shape[0], (512, 256))
    tn = 1024
    n_lat = att[0].shape[0] // tm

    def rows(parts, width):
        return [pl.BlockSpec((tm, width), lambda j, i: (jnp.minimum(i, n_lat - 1), 0)),
                pl.BlockSpec((tm, width), lambda j, i: (jnp.maximum(i - n_lat, 0), 0))]

    return pl.pallas_call(
        functools.partial(_mm3_kernel, n_lat=n_lat),
        grid=(N // tn, M // tm),
        in_specs=rows(att, k1) + rows(fmix, k2) + [
            pl.BlockSpec((tm, k3), lambda j, i: (i, 0)),
            pl.BlockSpec((1, k1, tn), lambda j, i: (l, 0, j)),
            pl.BlockSpec((1, k2, tn), lambda j, i: (l, k1 // k2, j)),
            pl.BlockSpec((1, k3, tn), lambda j, i: (l, k1 // k2 + 1, j))],
        out_specs=pl.BlockSpec((tm, tn), lambda j, i: (i, j)),
        out_shape=jax.ShapeDtypeStruct((M, N), _BF),
        compiler_params=_params(("parallel", "parallel"), 48),
        name="out_proj",
    )(*att, *fmix, cmix, w_out, w_out, w_out)


def _rms_bf16(x, g):
    return (x * lax.rsqrt(jnp.mean(x * x, axis=-1, keepdims=True) + EPS) * g).astype(_BF)


def _rope_pair(t):
    return t + pltpu.roll(t, HEAD_ROPE, 1)


def _qproj_kernel(cq_ref, g_ref, w_ref, cs_ref, o_ref, *, heads, scale):
    xn = _rms_bf16(cq_ref[...].astype(_F32), g_ref[...])
    y = jnp.dot(xn, w_ref[0], preferred_element_type=_F32)
    cs = cs_ref[...]
    for h in range(heads):
        lo = h * HEAD_QK
        o_ref[:, lo:lo + HEAD_V] = (y[:, lo:lo + HEAD_V] * scale).astype(_BF)
        t = y[:, lo + HEAD_V:lo + HEAD_QK] * cs
        o_ref[:, lo + HEAD_V:lo + HEAD_QK] = (_rope_pair(t) * scale).astype(_BF)


def _q_proj(y_in, g_q, w_uq_p, l, cs, q_rank, scale):
    T = y_in.shape[0]
    N = w_uq_p.shape[2]
    tm = _tile(T, (1088, 768, 512, 256))
    hb = 4
    tn = hb * HEAD_QK
    return pl.pallas_call(
        functools.partial(_qproj_kernel, heads=hb, scale=scale),
        grid=(T // tm, N // tn),
        in_specs=[pl.BlockSpec((tm, q_rank), lambda i, j: (i, 0)),
                  pl.BlockSpec((1, q_rank), lambda i, j: (0, 0)),
                  pl.BlockSpec((1, q_rank, tn), lambda i, j: (l, 0, j)),
                  pl.BlockSpec((tm, LANES), lambda i, j: (i, 0))],
        out_specs=pl.BlockSpec((tm, tn), lambda i, j: (i, j)),
        out_shape=jax.ShapeDtypeStruct((T, N), _BF),
        compiler_params=_params(("parallel", "parallel"), 32),
        name="q_proj",
    )(y_in, g_q.reshape(1, q_rank), w_uq_p, cs)


def _kvproj_kernel(ckv_ref, g_ref, wk_ref, wv_ref, kr_ref, cs_ref, k_ref, v_ref, *, heads):
    xn = _rms_bf16(ckv_ref[...].astype(_F32), g_ref[...])
    kn = jnp.dot(xn, wk_ref[0], preferred_element_type=_F32)
    v = jnp.dot(xn, wv_ref[0], preferred_element_type=_F32)
    t = kr_ref[...].astype(_F32) * cs_ref[...]
    lane = lax.broadcasted_iota(jnp.int32, t.shape, 1)
    krf = jnp.where(lane < HEAD_ROPE, _rope_pair(t), 0.0).astype(_BF)
    ones = jnp.ones((t.shape[0], HEAD_V), _BF)
    for h in range(heads):
        k_ref[:, h * HEAD_QK:h * HEAD_QK + HEAD_V] = kn[:, h * HEAD_V:(h + 1) * HEAD_V].astype(_BF)
        k_ref[:, h * HEAD_QK + HEAD_V:(h + 1) * HEAD_QK] = krf
        v_ref[:, h * HEAD_QK:h * HEAD_QK + HEAD_V] = v[:, h * HEAD_V:(h + 1) * HEAD_V].astype(_BF)
        v_ref[:, h * HEAD_QK + HEAD_V:(h + 1) * HEAD_QK] = ones


def _kv_proj(y_in, g_kv, w_ukv_p, l, cs, kv_rank, ckv_off, kr_off, n_heads):
    T = y_in.shape[0]
    tm = _tile(T, (1088, 768, 512, 256))
    hb = 4
    nb = n_heads // hb
    return pl.pallas_call(
        functools.partial(_kvproj_kernel, heads=hb),
        grid=(T // tm, nb),
        in_specs=[pl.BlockSpec((tm, kv_rank), lambda i, j: (i, ckv_off // kv_rank)),
                  pl.BlockSpec((1, kv_rank), lambda i, j: (0, 0)),
                  pl.BlockSpec((1, kv_rank, hb * HEAD_V), lambda i, j: (l, 0, j)),
                  pl.BlockSpec((1, kv_rank, hb * HEAD_V), lambda i, j: (l, 0, nb + j)),
                  pl.BlockSpec((tm, LANES), lambda i, j: (i, kr_off // LANES)),
                  pl.BlockSpec((tm, LANES), lambda i, j: (i, 0))],
        out_specs=[pl.BlockSpec((tm, hb * HEAD_QK), lambda i, j: (i, j)),
                   pl.BlockSpec((tm, hb * HEAD_QK), lambda i, j: (i, j))],
        out_shape=[jax.ShapeDtypeStruct((T, n_heads * HEAD_QK), _BF),
                   jax.ShapeDtypeStruct((T, n_heads * HEAD_QK), _BF)],
        compiler_params=_params(("parallel", "parallel"), 32),
        name="kv_proj",
    )(y_in, g_kv.reshape(1, kv_rank), w_ukv_p, w_ukv_p, y_in, cs)


_NT = (((1,), (1,)), ((), ()))


def _softmax_step(q, k, v1, m, acc):
    s = lax.dot_general(q, k, _NT, preferred_element_type=_F32)
    s_max = jnp.max(s, axis=-1, keepdims=True)
    m_new = s_max if m is None else jnp.maximum(m, s_max)
    p = jnp.exp2((s - m_new).astype(_BF))
    pv = jnp.dot(p, v1, preferred_element_type=_F32)
    if acc is not None:
        pv = jnp.exp2(m - m_new) * acc + pv
    return m_new, pv


def _attn_finish(acc, o_ref):
    o_ref[...] = (acc[:, :HEAD_V] / acc[:, HEAD_V:]).astype(_BF)


def _attn_lat_kernel(q_ref, kc_ref, vc_ref, kl_ref, vl_ref, o_ref, *, tk, n_chunks):
    q = q_ref[...]
    m, acc = _softmax_step(q, kc_ref[...], vc_ref[...], None, None)
    for c in range(n_chunks):
        m, acc = _softmax_step(q, kl_ref[c * tk:(c + 1) * tk, :], vl_ref[c * tk:(c + 1) * tk, :], m, acc)
    _attn_finish(acc, o_ref)


def _attn_ctx_kernel(q_ref, kc_ref, vc_ref, o_ref):
    _, acc = _softmax_step(q_ref[...], kc_ref[...], vc_ref[...], None, None)
    _attn_finish(acc, o_ref)


def _attention(q, k, v1, n_batch, seq, ctx, n_heads):
    tq = _tile(seq, (1024, 512, 256))
    tk = _tile(seq, (512, 256))
    nlt = seq // tq
    ctx_blk0 = n_batch * seq // ctx
    att_lat = pl.pallas_call(
        functools.partial(_attn_lat_kernel, tk=tk, n_chunks=seq // tk),
        grid=(n_batch, n_heads, nlt),
        in_specs=[pl.BlockSpec((tq, HEAD_QK), lambda b, h, i: (b * nlt + i, h)),
                  pl.BlockSpec((ctx, HEAD_QK), lambda b, h, i: (ctx_blk0 + b, h)),
                  pl.BlockSpec((ctx, HEAD_QK), lambda b, h, i: (ctx_blk0 + b, h)),
                  pl.BlockSpec((seq, HEAD_QK), lambda b, h, i: (b, h)),
                  pl.BlockSpec((seq, HEAD_QK), lambda b, h, i: (b, h))],
        out_specs=pl.BlockSpec((tq, HEAD_V), lambda b, h, i: (b * nlt + i, h)),
        out_shape=jax.ShapeDtypeStruct((n_batch * seq, n_heads * HEAD_V), _BF),
        compiler_params=_params(("parallel", "parallel", "arbitrary"), 48),
        name="attention",
    )(q, k, v1, k, v1)
    att_ctx = pl.pallas_call(
        _attn_ctx_kernel,
        grid=(n_batch, n_heads),
        in_specs=[pl.BlockSpec((ctx, HEAD_QK), lambda b, h: (ctx_blk0 + b, h)),
                  pl.BlockSpec((ctx, HEAD_QK), lambda b, h: (ctx_blk0 + b, h)),
                  pl.BlockSpec((ctx, HEAD_QK), lambda b, h: (ctx_blk0 + b, h))],
        out_specs=pl.BlockSpec((ctx, HEAD_V), lambda b, h: (b, h)),
        out_shape=jax.ShapeDtypeStruct((n_batch * ctx, n_heads * HEAD_V), _BF),
        compiler_params=_params(("parallel", "parallel"), 32),
        name="attention_ctx",
    )(q, k, v1)
    return att_lat, att_ctx


def _dft_mats(n):
    r = 1 << (n.bit_length() // 2)
    j = jnp.arange(n, dtype=jnp.int32)

    def table(k):
        ang = ((k[:, None] * j[None, :]) % n).astype(_F32) * (2.0 * math.pi / n)
        return jnp.cos(ang), jnp.sin(ang)

    ca, sa = table(jnp.arange(n // r, dtype=jnp.int32) * r)
    cb, sb = table(jnp.arange(r, dtype=jnp.int32))
    c = ca[:, None, :] * cb[None, :, :] - sa[:, None, :] * sb[None, :, :]
    s = sa[:, None, :] * cb[None, :, :] + ca[:, None, :] * sb[None, :, :]
    return c.reshape(n, n).astype(_BF), s.reshape(n, n).astype(_BF)


def _dft_ch_kernel(f_ref, cs_ref, pc_ref, ps_ref, *, fd):
    for g in range(F_GROUPS):
        y = jnp.dot(f_ref[:, g * fd:(g + 1) * fd], cs_ref[...], preferred_element_type=_F32)
        pc_ref[:, g * fd:(g + 1) * fd] = y[:, :fd].astype(_BF)
        ps_ref[:, g * fd:(g + 1) * fd] = y[:, fd:].astype(_BF)


def _dft_channels(y_in, cs_mat, f_off, fd):
    T = y_in.shape[0]
    tm = _tile(T, (1088, 768, 512, 256))
    width = F_GROUPS * fd
    out = jax.ShapeDtypeStruct((T, width), _BF)
    return pl.pallas_call(
        functools.partial(_dft_ch_kernel, fd=fd),
        grid=(T // tm,),
        in_specs=[pl.BlockSpec((tm, width), lambda i: (i, f_off // width)),
                  pl.BlockSpec((fd, 2 * fd), lambda i: (0, 0))],
        out_specs=[pl.BlockSpec((tm, width), lambda i: (i, 0)), pl.BlockSpec((tm, width), lambda i: (i, 0))],
        out_shape=[out, out],
        compiler_params=_params(("parallel",), 32),
        name="dft_channels",
    )(y_in, cs_mat)


def _dft_pos_kernel(c_ref, s_ref, pc_ref, ps_ref, wf_ref, o_ref, *, norm, fd, groups):
    acc = jnp.dot(c_ref[...], pc_ref[...], preferred_element_type=_F32)
    acc = acc - jnp.dot(s_ref[...], ps_ref[...], preferred_element_type=_F32)
    fr = (acc * norm).astype(_BF)
    for g in range(groups):
        o_ref[:, g * fd:(g + 1) * fd] = jnp.dot(fr[:, g * fd:(g + 1) * fd], wf_ref[g],
                                                preferred_element_type=_F32).astype(_BF)


def _dft_positions(pc, ps, cmat, smat, w_f, n_batch, n, row0, fd):
    tm = min(512, n)
    gb = 2
    tn = gb * fd
    blk0 = row0 // n
    return pl.pallas_call(
        functools.partial(_dft_pos_kernel, norm=1.0 / math.sqrt(n * fd), fd=fd, groups=gb),
        grid=(n_batch, F_GROUPS // gb, n // tm),
        in_specs=[pl.BlockSpec((tm, n), lambda b, j, m: (m, 0)),
                  pl.BlockSpec((tm, n), lambda b, j, m: (m, 0)),
                  pl.BlockSpec((n, tn), lambda b, j, m: (blk0 + b, j)),
                  pl.BlockSpec((n, tn), lambda b, j, m: (blk0 + b, j)),
                  pl.BlockSpec((gb, fd, fd), lambda b, j, m: (j, 0, 0))],
        out_specs=pl.BlockSpec((tm, tn), lambda b, j, m: (b * (n // tm) + m, j)),
        out_shape=jax.ShapeDtypeStruct((n_batch * n, F_GROUPS * fd), _BF),
        compiler_params=_params(("parallel", "parallel", "arbitrary"), 48),
        name=f"dft_positions_{n}",
    )(cmat, smat, pc, ps, w_f)


def _chunk_kernel(u_ref, v_ref, ws_ref, bs_ref, o_ref, *, heads, cdim, n_chunks):
    u = jax.nn.gelu(u_ref[...].astype(_F32))
    v = jax.nn.gelu(v_ref[...].astype(_F32))
    for h in range(heads):
        vh = v[:, h * cdim:(h + 1) * cdim]
        mu = jnp.mean(vh, axis=-1, keepdims=True)
        d = vh - mu
        var = jnp.mean(d * d, axis=-1, keepdims=True)
        vs = (d * lax.rsqrt(var + EPS)).astype(_BF)
        w = ws_ref[h]
        bias = bs_ref[h]
        for c in range(n_chunks):
            rows = slice(c * CHUNK, (c + 1) * CHUNK)
            sv = jnp.dot(w, vs[rows], preferred_element_type=_F32) + bias
            o_ref[rows, h * cdim:(h + 1) * cdim] = (u[rows, h * cdim:(h + 1) * cdim] * sv).astype(_BF)


def _chunk_mix(y_in, w_s, b_s, u_off, width):
    T = y_in.shape[0]
    tm = _tile(T, (512, 256))
    cdim = width // C_HEADS
    return pl.pallas_call(
        functools.partial(_chunk_kernel, heads=C_HEADS, cdim=cdim, n_chunks=tm // CHUNK),
        grid=(T // tm,),
        in_specs=[pl.BlockSpec((tm, width), lambda i: (i, u_off // width)),
                  pl.BlockSpec((tm, width), lambda i: (i, u_off // width + 1)),
                  pl.BlockSpec((C_HEADS, CHUNK, CHUNK), lambda i: (0, 0, 0)),
                  pl.BlockSpec((C_HEADS, CHUNK, 1), lambda i: (0, 0, 0))],
        out_specs=pl.BlockSpec((tm, width), lambda i: (i, 0)),
        out_shape=jax.ShapeDtypeStruct((T, width), _BF),
        compiler_params=_params(("parallel",), 32),
        name="chunk_mix",
    )(y_in, y_in, w_s, b_s.reshape(C_HEADS, CHUNK, 1))


_R_GIDX = N_GROUPS


def _route(lg):
    lane = lax.broadcasted_iota(jnp.int32, lg.shape, 1)

    def col(j):
        return jnp.sum(jnp.where(lane == j, lg, 0.0), axis=-1, keepdims=True)

    g = [col(j) for j in range(N_GROUPS)]
    gmax = functools.reduce(jnp.maximum, g)
    gi = jnp.full(gmax.shape, N_GROUPS - 1, jnp.int32)
    for j in reversed(range(N_GROUPS - 1)):
        gi = jnp.where(g[j] == gmax, j, gi)
    gw = 1.0 / functools.reduce(lambda a, b: a + b, [jnp.exp(x - gmax) for x in g])

    e = []
    for j in range(EXP_PER_GROUP):
        ej = col(N_GROUPS + j)
        for gg in range(1, N_GROUPS):
            ej = jnp.where(gi == gg, col(N_GROUPS + gg * EXP_PER_GROUP + j), ej)
        e.append(ej)
    v1 = functools.reduce(jnp.maximum, e)
    i1 = jnp.full(v1.shape, EXP_PER_GROUP - 1, jnp.int32)
    for j in reversed(range(EXP_PER_GROUP - 1)):
        i1 = jnp.where(e[j] == v1, j, i1)
    e2 = [jnp.where(i1 == j, -jnp.inf, e[j]) for j in range(EXP_PER_GROUP)]
    v2 = functools.reduce(jnp.maximum, e2)
    i2 = jnp.full(v2.shape, EXP_PER_GROUP - 1, jnp.int32)
    for j in reversed(range(EXP_PER_GROUP - 1)):
        i2 = jnp.where(e2[j] == v2, j, i2)
    t = jnp.exp(v2 - v1)
    w1 = (1.0 / (1.0 + t)) * gw
    w2 = (t / (1.0 + t)) * gw

    rec = jnp.where(lane == _R_GIDX, gi.astype(_F32), 0.0)
    for j in range(EXP_PER_GROUP):
        cw = jnp.where(i1 == j, w1, 0.0) + jnp.where(i2 == j, w2, 0.0)
        rec = rec + jnp.where(lane == j, cw, 0.0)
    return rec


def _route_plan(rec, tm):
    T = rec.shape[0]
    n_tiles = T // tm + N_GROUPS
    gidx = rec[:, _R_GIDX].astype(jnp.int32)
    onehot = (gidx[:, None] == jnp.arange(N_GROUPS, dtype=jnp.int32)[None, :]).astype(jnp.int32)
    rank = jnp.sum((jnp.cumsum(onehot, axis=0) - onehot) * onehot, axis=1)
    counts = jnp.sum(onehot, axis=0)
    tiles_g = (counts + tm - 1) // tm
    tile_end = jnp.cumsum(tiles_g)
    start = (tile_end - tiles_g) * tm
    dest = (jnp.sum(onehot * start[None, :], axis=1) + rank).astype(jnp.int32)
    src = jnp.zeros(((n_tiles + 1) * tm,), jnp.int32).at[dest].set(jnp.arange(T, dtype=jnp.int32))
    tile_id = jnp.arange(n_tiles, dtype=jnp.int32)
    tile_group = jnp.minimum(jnp.sum((tile_id[:, None] >= tile_end[None, :]).astype(jnp.int32), axis=1), N_GROUPS - 1)
    tile_used = (tile_id < tile_end[-1]).astype(jnp.int32)
    return dest, src, tile_group.astype(jnp.int32), tile_used


def _moe_kernel(src_ref, tg_ref, tv_ref, hp_hbm, wg_ref, wu_ref, wd_ref, o_ref, gbuf, sem, x_scr, rec_scr, hid_scr, *, tm, half):
    j, s = pl.program_id(0), pl.program_id(1)
    last_j, last_s = pl.num_programs(0) - 1, pl.num_programs(1) - 1
    used = tv_ref[j] == 1
    fetched = (j == 0) | (tv_ref[jnp.maximum(j - 1, 0)] == 1)
    quarter = tm // EXP_PER_GROUP
    F = wg_ref.shape[3]
    split = (F // 256) * 256
    tail = F - split

    @pl.when((s == 0) & fetched)
    def _():
        @pl.when(j == 0)
        def _():
            _gather_tile(hp_hbm, src_ref, j, tm, gbuf, sem, wait=False)

        _gather_tile(hp_hbm, src_ref, j, tm, gbuf, sem, wait=True)
        rows = gbuf[j % 2]
        lo, hi = _unpack_halves(rows[:, :half])
        x_scr[:, :half] = lo
        x_scr[:, half:] = hi
        rec_scr[...] = pltpu.bitcast(rows[:, half:], _F32)

    @pl.when(used & (s < EXP_PER_GROUP))
    def _():
        _gather_tile(hp_hbm, src_ref, j + 1, tm, gbuf, sem, wait=False, rows=(s * quarter, quarter), inline=True)
        x = x_scr[...]
        if split and 2 * tail == 256:
            w_tail = jnp.concatenate([wg_ref[0, 0, :, split:], wu_ref[0, 0, :, split:]], axis=1)
            t = jnp.dot(x, w_tail, preferred_element_type=_F32)
            gate = jnp.concatenate([jnp.dot(x, wg_ref[0, 0, :, :split], preferred_element_type=_F32), t[:, :tail]], axis=1)
            up = jnp.concatenate([jnp.dot(x, wu_ref[0, 0, :, :split], preferred_element_type=_F32), t[:, tail:]], axis=1)
        else:
            gate = jnp.dot(x, wg_ref[0, 0], preferred_element_type=_F32)
            up = jnp.dot(x, wu_ref[0, 0], preferred_element_type=_F32)
        rec = rec_scr[...]
        cw = rec[:, 0:1]
        for k in range(1, EXP_PER_GROUP):
            cw = jnp.where(s == k, rec[:, k:k + 1], cw)
        hid_scr[s] = ((jax.nn.silu(gate) * up) * cw).astype(_BF)

    @pl.when(used & (s >= EXP_PER_GROUP))
    def _():
        hid = jnp.concatenate([hid_scr[e] for e in range(EXP_PER_GROUP)], axis=1)
        wd = wd_ref[0]
        y = jnp.dot(hid, wd.reshape(wd.shape[0] * wd.shape[1], wd.shape[2]), preferred_element_type=_F32)
        o_ref[...] = _pack_halves(y.astype(_BF))

    @pl.when(jnp.logical_not(used) & (s >= EXP_PER_GROUP))
    def _():
        o_ref[...] = jnp.zeros(o_ref.shape, o_ref.dtype)

    @pl.when(used & (j == last_j) & (s == last_s))
    def _():
        _gather_tile(hp_hbm, src_ref, j + 1, tm, gbuf, sem, wait=True)


def _moe(hp, src, tile_group, tile_used, w_gate, w_up, w_down, l, tm):
    T, width = hp.shape
    half = width - LANES
    D = 2 * half
    n_tiles = tile_group.shape[0]
    L, E, _, F = w_gate.shape
    tn = MOE_TN
    w_down_g = w_down.reshape(L * N_GROUPS, EXP_PER_GROUP, F, D)
    assert src.shape[0] == (n_tiles + 1) * tm

    def expert(j, s, src, tg, tv):
        return tg[j] * EXP_PER_GROUP + jnp.minimum(s, EXP_PER_GROUP - 1)

    def col(j, s, src, tg, tv):
        return jnp.maximum(s - EXP_PER_GROUP, 0)

    return pl.pallas_call(
        functools.partial(_moe_kernel, tm=tm, half=half),
        grid_spec=pltpu.PrefetchScalarGridSpec(
            num_scalar_prefetch=3,
            grid=(n_tiles, EXP_PER_GROUP + D // tn),
            in_specs=[pl.BlockSpec(memory_space=pl.ANY),
                      pl.BlockSpec((1, 1, D, F), lambda j, s, *p: (l, expert(j, s, *p), 0, 0)),
                      pl.BlockSpec((1, 1, D, F), lambda j, s, *p: (l, expert(j, s, *p), 0, 0)),
                      pl.BlockSpec((1, EXP_PER_GROUP, F, tn), lambda j, s, *p: (l * N_GROUPS + p[1][j], 0, 0, col(j, s, *p)))],
            out_specs=pl.BlockSpec((tm, tn // 2), lambda j, s, *p: (j, col(j, s, *p))),
            scratch_shapes=[pltpu.VMEM((2, tm, width), jnp.uint32), pltpu.SemaphoreType.DMA((2,)),
                            pltpu.VMEM((tm, D), _BF), pltpu.VMEM((tm, LANES), _F32),
                            pltpu.VMEM((EXP_PER_GROUP, tm, F), _BF)]),
        out_shape=jax.ShapeDtypeStruct((n_tiles * tm, D // 2), jnp.uint32),
        compiler_params=_params(("arbitrary", "arbitrary"), 48),
        name="moe",
    )(src, tile_group, tile_used, hp, w_gate, w_up, w_down_g)


def _rot_cols(w):
    q = HEAD_ROPE // 4
    a, b, c, d = w[..., :q], w[..., q:2 * q], w[..., 2 * q:3 * q], w[..., 3 * q:]
    return jnp.concatenate([-b, a, -d, c], axis=-1)


def _prep_w_in(w_in, D):
    q, kv = D // 4, D // 8
    o1, o2, o3 = q, q + kv, q + kv + HEAD_ROPE
    kr = w_in[..., o2:o3]
    pieces = [w_in[..., :o1], w_in[..., o3:], w_in[..., o1:o2], kr, _rot_cols(kr)]
    width = sum(p.shape[-1] for p in pieces)
    total = width + (-width) % 512
    out, off = None, 0
    for p in pieces:
        cfg = [(0, 0, 0)] * (p.ndim - 1) + [(off, total - off - p.shape[-1], 0)]
        term = lax.pad(p, jnp.zeros((), p.dtype), cfg)
        out = term if out is None else out + term
        off += p.shape[-1]
    return out.astype(_BF)


def _prep_w_uq(w_uq, n_heads):
    L, r, _ = w_uq.shape
    w = w_uq.reshape(L, r, n_heads, HEAD_V + HEAD_ROPE)
    rope = w[..., HEAD_V:]
    return jnp.concatenate([w[..., :HEAD_V], rope, _rot_cols(rope)], axis=-1).reshape(L, r, n_heads * HEAD_QK).astype(_BF)


def _prep_w_ukv(w_ukv, n_heads):
    L, r, _ = w_ukv.shape
    w = w_ukv.reshape(L, r, n_heads, 2 * HEAD_V)
    return jnp.concatenate([w[..., :HEAD_V].reshape(L, r, -1), w[..., HEAD_V:].reshape(L, r, -1)], axis=-1).astype(_BF)


def _rope_table(n_batch, seq, ctx):
    half = HEAD_ROPE // 2
    inv_freq = ROPE_THETA ** (-jnp.arange(0, half, 2, dtype=_F32) / half)
    pos = jnp.arange(seq)
    ang_r = (pos // GRID_W).astype(_F32)[:, None] * inv_freq
    ang_c = (pos % GRID_W).astype(_F32)[:, None] * inv_freq
    ang = jnp.concatenate([ang_r, ang_r, ang_c, ang_c], axis=-1)
    lat = jnp.concatenate([jnp.cos(ang), jnp.sin(ang)], axis=-1)
    ctx_rows = jnp.concatenate([jnp.ones((n_batch * ctx, HEAD_ROPE), _F32), jnp.zeros((n_batch * ctx, HEAD_ROPE), _F32)], axis=-1)
    return jnp.concatenate([jnp.tile(lat, (n_batch, 1)), ctx_rows], axis=0)


def kernel(x, c, ctx, c_ctx, w_ada, b_ada, w_in, q_norm_g, kv_norm_g, w_uq, w_ukv, w_fourier, w_spatial, b_spatial, w_out, ln1_g, ln1_b, w_router_group, b_router_group, w_router_expert, b_router_expert, w_gate, w_up, w_down, ln2_g, ln2_b):
    B, SEQ, D = x.shape
    CTX = ctx.shape[1]
    L = w_ada.shape[0]
    n_heads = (D // 2) // HEAD_V
    q_rank, kv_rank = D // 4, D // 8
    fd = D // 16
    alpha = (2.0 * L) ** 0.25
    n_seg = B + 1
    assert n_seg <= MOD_ROWS and SEQ % 256 == 0 and (B * CTX) % 256 == 0 and CTX == 256 and (B * SEQ) % CTX == 0
    moe_tm = 512

    f_off, u_off, ckv_off, kr_off = q_rank, 2 * q_rank, D, D + kv_rank

    w_in_p = _prep_w_in(w_in, D)
    w_uq_p = _prep_w_uq(w_uq, n_heads)
    w_ukv_p = _prep_w_ukv(w_ukv, n_heads)
    w_out_b = w_out.astype(_BF)
    w_f_b = w_fourier.astype(_BF)
    w_s_b = w_spatial.astype(_BF)
    w_gate_b = w_gate.astype(_BF)
    w_up_b = w_up.astype(_BF)
    w_down_b = w_down.astype(_BF)
    n_logits = N_GROUPS + N_GROUPS * EXP_PER_GROUP
    w_r = jnp.concatenate([w_router_group, w_router_expert, jnp.zeros((L, D, LANES - n_logits), _F32)], axis=-1).astype(_BF)
    b_r = jnp.concatenate([b_router_group, b_router_expert, jnp.zeros((L, LANES - n_logits), _F32)], axis=-1).reshape(L, 1, LANES)
    cs_tab = _rope_table(B, SEQ, CTX)
    c_lat, s_lat = _dft_mats(SEQ)
    c_ctx_m, s_ctx_m = _dft_mats(CTX)
    c_ch, s_ch = _dft_mats(fd)
    cs_ch = jnp.concatenate([c_ch, s_ch], axis=1)

    c_rows = jnp.concatenate([c, c_ctx[None, :], jnp.zeros((MOD_ROWS - n_seg, D), _F32)], axis=0)
    mod_all = _adaln(c_rows, w_ada, b_ada)

    xt = (x.reshape(B * SEQ, D), ctx.reshape(B * CTX, D))
    mod3 = [mod_all[l].reshape(MOD_ROWS * 6, 1, D) for l in range(L)]
    h = _modulate(xt, mod3[0], 1, 0, SEQ, B)

    for l in range(L):
        last = l == L - 1
        y_in = _matmul(h, w_in_p, l, _BF, "in_proj")
        q = _q_proj(y_in, q_norm_g[l], w_uq_p, l, cs_tab, q_rank, math.log2(math.e) / math.sqrt(HEAD_V + HEAD_ROPE))
        k, v1 = _kv_proj(y_in, kv_norm_g[l], w_ukv_p, l, cs_tab, kv_rank, ckv_off, kr_off, n_heads)
        att = _attention(q, k, v1, B, SEQ, CTX, n_heads)
        pc, ps = _dft_channels(y_in, cs_ch, f_off, fd)
        fmix = (_dft_positions(pc, ps, c_lat, s_lat, w_f_b[l], B, SEQ, 0, fd),
                _dft_positions(pc, ps, c_ctx_m, s_ctx_m, w_f_b[l], B, CTX, B * SEQ, fd))
        cmix = _chunk_mix(y_in, w_s_b[l], b_spatial[l], u_off, q_rank)
        mixed = _out_proj(att, fmix, cmix, w_out_b, l)
        x1, h2p, rec = _resid_ln_route(xt, mixed, mod3[l], 2, ln1_g[l], ln1_b[l], 4, 3, w_r, b_r, l, SEQ, B, alpha)
        dest, src, tile_group, tile_used = _route_plan(rec, moe_tm)
        ys = _moe(h2p, src, tile_group, tile_used, w_gate_b, w_up_b, w_down_b, l, moe_tm)
        x2, h = _resid_ln_gather(x1, ys, dest, mod3[l], 5, ln2_g[l], ln2_b[l], 1, 0, SEQ, B, alpha,
                                 emit_h=not last, mod_next=mod3[min(l + 1, L - 1)],
                                 n_rows=B * SEQ if last else B * (SEQ + CTX))
        xt = (x2,)

    return xt[0].reshape(B, SEQ, D)
```

```python
import functools
import math

import jax
import jax.numpy as jnp
from jax import lax
from jax.experimental import pallas as pl
from jax.experimental.pallas import tpu as pltpu

_BF = jnp.bfloat16
_F32 = jnp.float32

ROPE_THETA = 10000.0
EPS = 1e-6
GRID_W = 64
HEAD_V = 128
HEAD_ROPE = 64
HEAD_QK = 256
F_GROUPS = 4
C_HEADS = 4
CHUNK = 128
N_GROUPS = 4
EXP_PER_GROUP = 4
LANES = 128
MOD_ROWS = 8
MOE_TN = 2048

_MIB = 1024 * 1024


def _params(sem, vmem_mib):
    return pltpu.CompilerParams(dimension_semantics=sem, vmem_limit_bytes=vmem_mib * _MIB)


def _tile(n, prefs):
    for t in prefs:
        if n % t == 0:
            return t
    raise ValueError(f"no tile in {prefs} divides {n}")


def _adaln_kernel(c_ref, w_ref, b_ref, o_ref):
    s = jax.nn.silu(c_ref[...]).astype(_BF)
    o_ref[0] = jnp.dot(s, w_ref[0].astype(_BF), preferred_element_type=_F32) + b_ref[0]


def _adaln(c_rows, w_ada, b_ada):
    L, D, N = w_ada.shape
    tn = 512
    return pl.pallas_call(
        _adaln_kernel,
        grid=(L, N // tn),
        in_specs=[pl.BlockSpec((MOD_ROWS, D), lambda l, j: (0, 0)),
                  pl.BlockSpec((1, D, tn), lambda l, j: (l, 0, j)),
                  pl.BlockSpec((1, 1, tn), lambda l, j: (l, 0, j))],
        out_specs=pl.BlockSpec((1, MOD_ROWS, tn), lambda l, j: (l, 0, j)),
        out_shape=jax.ShapeDtypeStruct((L, MOD_ROWS, N), _F32),
        compiler_params=_params(("parallel", "parallel"), 40),
        name="adaln",
    )(c_rows, w_ada, b_ada.reshape(L, 1, N))


def _mod_spec(chunk, D, tiles_per_seq, n_batch):
    return pl.BlockSpec((1, 1, D), lambda i, *_: (jnp.minimum(i // tiles_per_seq, n_batch) * 6 + chunk, 0, 0))


def _row_specs(parts, tm, width):
    if len(parts) == 1:
        return [pl.BlockSpec((tm, width), lambda i, *_: (i, 0))]
    n_lat = parts[0].shape[0] // tm
    return [pl.BlockSpec((tm, width), lambda i, *_: (jnp.minimum(i, n_lat - 1), 0)),
            pl.BlockSpec((tm, width), lambda i, *_: (jnp.maximum(i - n_lat, 0), 0))]


def _pick_rows(refs, is_lat):
    return refs[0][...] if len(refs) == 1 else jnp.where(is_lat, refs[0][...], refs[1][...])


def _modulate_kernel(*refs, n_x, n_lat):
    x_refs, (sc_ref, sh_ref, o_ref) = refs[:n_x], refs[n_x:]
    x = _pick_rows(x_refs, pl.program_id(0) < n_lat)
    o_ref[...] = (x * (1.0 + sc_ref[0]) + sh_ref[0]).astype(_BF)


def _modulate(xs, mod, sc_chunk, sh_chunk, seq, n_batch):
    D = xs[0].shape[1]
    T = sum(p.shape[0] for p in xs)
    tm = _tile(seq, (256,))
    tps = seq // tm
    return pl.pallas_call(
        functools.partial(_modulate_kernel, n_x=len(xs), n_lat=xs[0].shape[0] // tm),
        grid=(T // tm,),
        in_specs=_row_specs(xs, tm, D) + [_mod_spec(sc_chunk, D, tps, n_batch), _mod_spec(sh_chunk, D, tps, n_batch)],
        out_specs=pl.BlockSpec((tm, D), lambda i: (i, 0)),
        out_shape=jax.ShapeDtypeStruct((T, D), _BF),
        compiler_params=_params(("parallel",), 32),
        name="modulate",
    )(*xs, mod, mod)


def _pack_halves(hb):
    half = hb.shape[1] // 2
    bits = pltpu.bitcast(hb.astype(_F32), jnp.uint32)
    return (bits[:, half:] & jnp.uint32(0xFFFF0000)) | (bits[:, :half] >> 16)


def _unpack_halves(u):
    lo = pltpu.bitcast(u << 16, _F32).astype(_BF)
    hi = pltpu.bitcast(u & jnp.uint32(0xFFFF0000), _F32).astype(_BF)
    return lo, hi


def _ln_tail(y, g_ref, b_ref, sc_ref, sh_ref, xo_ref, ho_ref):
    mu = jnp.mean(y, axis=-1, keepdims=True)
    d = y - mu
    var = jnp.mean(d * d, axis=-1, keepdims=True)
    xn = d * lax.rsqrt(var + EPS) * g_ref[...] + b_ref[...]
    xo_ref[...] = xn
    if sc_ref is None:
        return None
    hb = (xn * (1.0 + sc_ref[0]) + sh_ref[0]).astype(_BF)
    if ho_ref is not None:
        ho_ref[...] = hb
    return hb


def _resid_ln_route_kernel(*refs, n_x, n_lat, alpha):
    x_refs = refs[:n_x]
    a_ref, gate_ref, g_ref, b_ref, sc_ref, sh_ref, wr_ref, br_ref, xo_ref, hp_ref, r_ref = refs[n_x:]
    x = _pick_rows(x_refs, pl.program_id(0) < n_lat)
    y = alpha * x + gate_ref[0] * a_ref[...].astype(_F32)
    hb = _ln_tail(y, g_ref, b_ref, sc_ref, sh_ref, xo_ref, None)
    rec = _route(jnp.dot(hb, wr_ref[0], preferred_element_type=_F32) + br_ref[0])
    half = hb.shape[1] // 2
    hp_ref[:, :half] = _pack_halves(hb)
    hp_ref[:, half:] = pltpu.bitcast(rec, jnp.uint32)
    r_ref[...] = rec


def _resid_ln_route(xs, a, mod, gate_chunk, ln_g, ln_b, sc_chunk, sh_chunk, w_r, b_r, l, seq, n_batch, alpha):
    T, D = a.shape
    tm = _tile(seq, (256,))
    tps = seq // tm
    row = pl.BlockSpec((tm, D), lambda i: (i, 0))
    vec = pl.BlockSpec((1, D), lambda i: (0, 0))
    return pl.pallas_call(
        functools.partial(_resid_ln_route_kernel, n_x=len(xs), n_lat=xs[0].shape[0] // tm, alpha=alpha),
        grid=(T // tm,),
        in_specs=_row_specs(xs, tm, D) + [row, _mod_spec(gate_chunk, D, tps, n_batch), vec, vec,
                                          _mod_spec(sc_chunk, D, tps, n_batch), _mod_spec(sh_chunk, D, tps, n_batch),
                                          pl.BlockSpec((1, D, LANES), lambda i: (l, 0, 0)),
                                          pl.BlockSpec((1, 1, LANES), lambda i: (l, 0, 0))],
        out_specs=[row, pl.BlockSpec((tm, D // 2 + LANES), lambda i: (i, 0)),
                   pl.BlockSpec((tm, LANES), lambda i: (i, 0))],
        out_shape=[jax.ShapeDtypeStruct((T, D), _F32), jax.ShapeDtypeStruct((T, D // 2 + LANES), jnp.uint32),
                   jax.ShapeDtypeStruct((T, LANES), _F32)],
        compiler_params=_params(("parallel",), 40 + 8 * len(xs)),
        name="resid_ln_route",
    )(*xs, a, mod, ln_g.reshape(1, D), ln_b.reshape(1, D), mod, mod, w_r, b_r)


def _gather_tile(src_hbm, idx_ref, tile, tm, buf, sem, wait, rows=None, inline=False):
    slot = tile % 2
    lo, n = rows if rows is not None else (0, tm)
    def _step(r, carry):
        cp = pltpu.make_async_copy(src_hbm.at[pl.ds(idx_ref[tile * tm + r], 1)], buf.at[slot, pl.ds(r, 1)],
                                   sem.at[slot])
        if wait:
            cp.wait()
        else:
            cp.start()
        return carry

    if inline:
        for r in range(n):
            _step(lo + r, 0)
    else:
        lax.fori_loop(lo, lo + n, _step, 0, unroll=8)


def _gather_pipelined(src_hbm, idx_ref, tile, n_tiles, tm, buf, sem):
    @pl.when(tile == 0)
    def _():
        _gather_tile(src_hbm, idx_ref, tile, tm, buf, sem, wait=False)

    _gather_tile(src_hbm, idx_ref, tile, tm, buf, sem, wait=True)

    @pl.when(tile + 1 < n_tiles)
    def _():
        _gather_tile(src_hbm, idx_ref, tile + 1, tm, buf, sem, wait=False)


def _resid_ln_gather_kernel(idx_ref, x_ref, ys_hbm, gate_ref, g_ref, b_ref, sc_ref, sh_ref, *rest, alpha, tm, emit_h):
    if emit_h:
        xo_ref, ho_ref, buf, sem = rest
    else:
        xo_ref, buf, sem = rest
        ho_ref = sc_ref = sh_ref = None
    i = pl.program_id(0)
    _gather_pipelined(ys_hbm, idx_ref, i, pl.num_programs(0), tm, buf, sem)
    rows = buf[i % 2]
    pieces = []
    for c in range(rows.shape[1] // (MOE_TN // 2)):
        pieces.extend(_unpack_halves(rows[:, c * (MOE_TN // 2):(c + 1) * (MOE_TN // 2)]))
    a = jnp.concatenate(pieces, axis=1).astype(_F32)
    y = alpha * x_ref[...] + gate_ref[0] * a
    _ln_tail(y, g_ref, b_ref, sc_ref, sh_ref, xo_ref, ho_ref)


def _resid_ln_gather(x, ys, dest, mod, gate_chunk, ln_g, ln_b, sc_chunk, sh_chunk, seq, n_batch, alpha, emit_h, mod_next,
                     n_rows):
    T, D = n_rows, x.shape[1]
    tm = _tile(seq, (256,))
    tps = seq // tm
    row = pl.BlockSpec((tm, D), lambda i, idx: (i, 0))
    vec = pl.BlockSpec((1, D), lambda i, idx: (0, 0))
    out_specs = [row, row] if emit_h else [row]
    out_shape = [jax.ShapeDtypeStruct((T, D), _F32)] + ([jax.ShapeDtypeStruct((T, D), _BF)] if emit_h else [])
    res = pl.pallas_call(
        functools.partial(_resid_ln_gather_kernel, alpha=alpha, tm=tm, emit_h=emit_h),
        grid_spec=pltpu.PrefetchScalarGridSpec(
            num_scalar_prefetch=1,
            grid=(T // tm,),
            in_specs=[row, pl.BlockSpec(memory_space=pl.ANY), _mod_spec(gate_chunk, D, tps, n_batch), vec, vec,
                      _mod_spec(sc_chunk, D, tps, n_batch), _mod_spec(sh_chunk, D, tps, n_batch)],
            out_specs=out_specs,
            scratch_shapes=[pltpu.VMEM((2, tm, D // 2), jnp.uint32), pltpu.SemaphoreType.DMA((2,))]),
        out_shape=out_shape,
        compiler_params=_params(("arbitrary",), 40),
        name="resid_ln_gather",
    )(dest, x, ys, mod, ln_g.reshape(1, D), ln_b.reshape(1, D), mod_next, mod_next)
    return res if emit_h else (res[0], None)


def _mm_kernel(a_ref, bt_ref, o_ref):
    o_ref[...] = lax.dot_general(a_ref[...], bt_ref[0], (((1,), (1,)), ((), ())),
                                 preferred_element_type=_F32).astype(o_ref.dtype)


def _matmul(a, bt, l, out_dtype, name):
    M, K = a.shape
    N = bt.shape[1]
    tm = _tile(M, (1088, 768, 512, 256))
    tn = 512
    return pl.pallas_call(
        _mm_kernel,
        grid=(M // tm, N // tn),
        in_specs=[pl.BlockSpec((tm, K), lambda i, j: (i, 0)), pl.BlockSpec((1, tn, K), lambda i, j: (l, j, 0))],
        out_specs=pl.BlockSpec((tm, tn), lambda i, j: (i, j)),
        out_shape=jax.ShapeDtypeStruct((M, N), out_dtype),
        compiler_params=_params(("parallel", "parallel"), 48),
        name=name,
    )(a, bt)


def _mm3_kernel(al_ref, ac_ref, fl_ref, fc_ref, c_ref, b1_ref, b2_ref, b3_ref, o_ref, *, n_lat):
    is_lat = pl.program_id(1) < n_lat
    acc = jnp.dot(_pick_rows((al_ref, ac_ref), is_lat), b1_ref[0], preferred_element_type=_F32)
    acc = acc + jnp.dot(_pick_rows((fl_ref, fc_ref), is_lat), b2_ref[0], preferred_element_type=_F32)
    acc = acc + jnp.dot(c_ref[...], b3_ref[0], preferred_element_type=_F32)
    o_ref[...] = acc.astype(o_ref.dtype)


def _out_proj(att, fmix, cmix, w_out, l):
    M = cmix.shape[0]
    k1, k2, k3 = att[0].shape[1], fmix[0].shape[1], cmix.shape[1]
    N = w_out.shape[2]
    assert k2 == k3 and k1 % k2 == 0
    tm = _tile(att[1].shape[0], (512, 256))
    tn = 1024
    n_lat = att[0].shape[0] // tm

    def rows(parts, width):
        return [pl.BlockSpec((tm, width), lambda j, i: (jnp.minimum(i, n_lat - 1), 0)),
                pl.BlockSpec((tm, width), lambda j, i: (jnp.maximum(i - n_lat, 0), 0))]

    return pl.pallas_call(
        functools.partial(_mm3_kernel, n_lat=n_lat),
        grid=(N // tn, M // tm),
        in_specs=rows(att, k1) + rows(fmix, k2) + [
            pl.BlockSpec((tm, k3), lambda j, i: (i, 0)),
            pl.BlockSpec((1, k1, tn), lambda j, i: (l, 0, j)),
            pl.BlockSpec((1, k2, tn), lambda j, i: (l, k1 // k2, j)),
            pl.BlockSpec((1, k3, tn), lambda j, i: (l, k1 // k2 + 1, j))],
        out_specs=pl.BlockSpec((tm, tn), lambda j, i: (i, j)),
        out_shape=jax.ShapeDtypeStruct((M, N), _BF),
        compiler_params=_params(("parallel", "parallel"), 48),
        name="out_proj",
    )(*att, *fmix, cmix, w_out, w_out, w_out)


def _rms_bf16(x, g):
    return (x * lax.rsqrt(jnp.mean(x * x, axis=-1, keepdims=True) + EPS) * g).astype(_BF)


def _rope_pair(t):
    return t + pltpu.roll(t, HEAD_ROPE, 1)


def _qproj_kernel(cq_ref, g_ref, w_ref, cs_ref, o_ref, *, heads, scale):
    xn = _rms_bf16(cq_ref[...].astype(_F32), g_ref[...])
    y = jnp.dot(xn, w_ref[0], preferred_element_type=_F32)
    cs = cs_ref[...]
    for h in range(heads):
        lo = h * HEAD_QK
        o_ref[:, lo:lo + HEAD_V] = (y[:, lo:lo + HEAD_V] * scale).astype(_BF)
        t = y[:, lo + HEAD_V:lo + HEAD_QK] * cs
        o_ref[:, lo + HEAD_V:lo + HEAD_QK] = (_rope_pair(t) * scale).astype(_BF)


def _q_proj(y_in, g_q, w_uq_p, l, cs, q_rank, scale):
    T = y_in.shape[0]
    N = w_uq_p.shape[2]
    tm = _tile(T, (1088, 768, 512, 256))
    hb = 4
    tn = hb * HEAD_QK
    return pl.pallas_call(
        functools.partial(_qproj_kernel, heads=hb, scale=scale),
        grid=(T // tm, N // tn),
        in_specs=[pl.BlockSpec((tm, q_rank), lambda i, j: (i, 0)),
                  pl.BlockSpec((1, q_rank), lambda i, j: (0, 0)),
                  pl.BlockSpec((1, q_rank, tn), lambda i, j: (l, 0, j)),
                  pl.BlockSpec((tm, LANES), lambda i, j: (i, 0))],
        out_specs=pl.BlockSpec((tm, tn), lambda i, j: (i, j)),
        out_shape=jax.ShapeDtypeStruct((T, N), _BF),
        compiler_params=_params(("parallel", "parallel"), 32),
        name="q_proj",
    )(y_in, g_q.reshape(1, q_rank), w_uq_p, cs)


def _kvproj_kernel(ckv_ref, g_ref, wk_ref, wv_ref, kr_ref, cs_ref, k_ref, v_ref, *, heads):
    xn = _rms_bf16(ckv_ref[...].astype(_F32), g_ref[...])
    kn = jnp.dot(xn, wk_ref[0], preferred_element_type=_F32)
    v = jnp.dot(xn, wv_ref[0], preferred_element_type=_F32)
    t = kr_ref[...].astype(_F32) * cs_ref[...]
    lane = lax.broadcasted_iota(jnp.int32, t.shape, 1)
    krf = jnp.where(lane < HEAD_ROPE, _rope_pair(t), 0.0).astype(_BF)
    ones = jnp.ones((t.shape[0], HEAD_V), _BF)
    for h in range(heads):
        k_ref[:, h * HEAD_QK:h * HEAD_QK + HEAD_V] = kn[:, h * HEAD_V:(h + 1) * HEAD_V].astype(_BF)
        k_ref[:, h * HEAD_QK + HEAD_V:(h + 1) * HEAD_QK] = krf
        v_ref[:, h * HEAD_QK:h * HEAD_QK + HEAD_V] = v[:, h * HEAD_V:(h + 1) * HEAD_V].astype(_BF)
        v_ref[:, h * HEAD_QK + HEAD_V:(h + 1) * HEAD_QK] = ones


def _kv_proj(y_in, g_kv, w_ukv_p, l, cs, kv_rank, ckv_off, kr_off, n_heads):
    T = y_in.shape[0]
    tm = _tile(T, (1088, 768, 512, 256))
    hb = 4
    nb = n_heads // hb
    return pl.pallas_call(
        functools.partial(_kvproj_kernel, heads=hb),
        grid=(T // tm, nb),
        in_specs=[pl.BlockSpec((tm, kv_rank), lambda i, j: (i, ckv_off // kv_rank)),
                  pl.BlockSpec((1, kv_rank), lambda i, j: (0, 0)),
                  pl.BlockSpec((1, kv_rank, hb * HEAD_V), lambda i, j: (l, 0, j)),
                  pl.BlockSpec((1, kv_rank, hb * HEAD_V), lambda i, j: (l, 0, nb + j)),
                  pl.BlockSpec((tm, LANES), lambda i, j: (i, kr_off // LANES)),
                  pl.BlockSpec((tm, LANES), lambda i, j: (i, 0))],
        out_specs=[pl.BlockSpec((tm, hb * HEAD_QK), lambda i, j: (i, j)),
                   pl.BlockSpec((tm, hb * HEAD_QK), lambda i, j: (i, j))],
        out_shape=[jax.ShapeDtypeStruct((T, n_heads * HEAD_QK), _BF),
                   jax.ShapeDtypeStruct((T, n_heads * HEAD_QK), _BF)],
        compiler_params=_params(("parallel", "parallel"), 32),
        name="kv_proj",
    )(y_in, g_kv.reshape(1, kv_rank), w_ukv_p, w_ukv_p, y_in, cs)


_NT = (((1,), (1,)), ((), ()))


def _softmax_step(q, k, v1, m, acc):
    s = lax.dot_general(q, k, _NT, preferred_element_type=_F32)
    s_max = jnp.max(s, axis=-1, keepdims=True)
    m_new = s_max if m is None else jnp.maximum(m, s_max)
    p = jnp.exp2((s - m_new).astype(_BF))
    pv = jnp.dot(p, v1, preferred_element_type=_F32)
    if acc is not None:
        pv = jnp.exp2(m - m_new) * acc + pv
    return m_new, pv


def _attn_finish(acc, o_ref):
    o_ref[...] = (acc[:, :HEAD_V] / acc[:, HEAD_V:]).astype(_BF)


def _attn_lat_kernel(q_ref, kc_ref, vc_ref, kl_ref, vl_ref, o_ref, *, tk, n_chunks):
    q = q_ref[...]
    m, acc = _softmax_step(q, kc_ref[...], vc_ref[...], None, None)
    for c in range(n_chunks):
        m, acc = _softmax_step(q, kl_ref[c * tk:(c + 1) * tk, :], vl_ref[c * tk:(c + 1) * tk, :], m, acc)
    _attn_finish(acc, o_ref)


def _attn_ctx_kernel(q_ref, kc_ref, vc_ref, o_ref):
    _, acc = _softmax_step(q_ref[...], kc_ref[...], vc_ref[...], None, None)
    _attn_finish(acc, o_ref)


def _attention(q, k, v1, n_batch, seq, ctx, n_heads):
    tq = _tile(seq, (1024, 512, 256))
    tk = _tile(seq, (512, 256))
    nlt = seq // tq
    ctx_blk0 = n_batch * seq // ctx
    att_lat = pl.pallas_call(
        functools.partial(_attn_lat_kernel, tk=tk, n_chunks=seq // tk),
        grid=(n_batch, n_heads, nlt),
        in_specs=[pl.BlockSpec((tq, HEAD_QK), lambda b, h, i: (b * nlt + i, h)),
                  pl.BlockSpec((ctx, HEAD_QK), lambda b, h, i: (ctx_blk0 + b, h)),
                  pl.BlockSpec((ctx, HEAD_QK), lambda b, h, i: (ctx_blk0 + b, h)),
                  pl.BlockSpec((seq, HEAD_QK), lambda b, h, i: (b, h)),
                  pl.BlockSpec((seq, HEAD_QK), lambda b, h, i: (b, h))],
        out_specs=pl.BlockSpec((tq, HEAD_V), lambda b, h, i: (b * nlt + i, h)),
        out_shape=jax.ShapeDtypeStruct((n_batch * seq, n_heads * HEAD_V), _BF),
        compiler_params=_params(("parallel", "parallel", "arbitrary"), 48),
        name="attention",
    )(q, k, v1, k, v1)
    att_ctx = pl.pallas_call(
        _attn_ctx_kernel,
        grid=(n_batch, n_heads),
        in_specs=[pl.BlockSpec((ctx, HEAD_QK), lambda b, h: (ctx_blk0 + b, h)),
                  pl.BlockSpec((ctx, HEAD_QK), lambda b, h: (ctx_blk0 + b, h)),
                  pl.BlockSpec((ctx, HEAD_QK), lambda b, h: (ctx_blk0 + b, h))],
        out_specs=pl.BlockSpec((ctx, HEAD_V), lambda b, h: (b, h)),
        out_shape=jax.ShapeDtypeStruct((n_batch * ctx, n_heads * HEAD_V), _BF),
        compiler_params=_params(("parallel", "parallel"), 32),
        name="attention_ctx",
    )(q, k, v1)
    return att_lat, att_ctx


def _dft_mats(n):
    r = 1 << (n.bit_length() // 2)
    j = jnp.arange(n, dtype=jnp.int32)

    def table(k):
        ang = ((k[:, None] * j[None, :]) % n).astype(_F32) * (2.0 * math.pi / n)
        return jnp.cos(ang), jnp.sin(ang)

    ca, sa = table(jnp.arange(n // r, dtype=jnp.int32) * r)
    cb, sb = table(jnp.arange(r, dtype=jnp.int32))
    c = ca[:, None, :] * cb[None, :, :] - sa[:, None, :] * sb[None, :, :]
    s = sa[:, None, :] * cb[None, :, :] + ca[:, None, :] * sb[None, :, :]
    return c.reshape(n, n).astype(_BF), s.reshape(n, n).astype(_BF)


def _dft_ch_kernel(f_ref, cs_ref, pc_ref, ps_ref, *, fd):
    for g in range(F_GROUPS):
        y = jnp.dot(f_ref[:, g * fd:(g + 1) * fd], cs_ref[...], preferred_element_type=_F32)
        pc_ref[:, g * fd:(g + 1) * fd] = y[:, :fd].astype(_BF)
        ps_ref[:, g * fd:(g + 1) * fd] = y[:, fd:].astype(_BF)


def _dft_channels(y_in, cs_mat, f_off, fd):
    T = y_in.shape[0]
    tm = _tile(T, (1088, 768, 512, 256))
    width = F_GROUPS * fd
    out = jax.ShapeDtypeStruct((T, width), _BF)
    return pl.pallas_call(
        functools.partial(_dft_ch_kernel, fd=fd),
        grid=(T // tm,),
        in_specs=[pl.BlockSpec((tm, width), lambda i: (i, f_off // width)),
                  pl.BlockSpec((fd, 2 * fd), lambda i: (0, 0))],
        out_specs=[pl.BlockSpec((tm, width), lambda i: (i, 0)), pl.BlockSpec((tm, width), lambda i: (i, 0))],
        out_shape=[out, out],
        compiler_params=_params(("parallel",), 32),
        name="dft_channels",
    )(y_in, cs_mat)


def _dft_pos_kernel(c_ref, s_ref, pc_ref, ps_ref, wf_ref, o_ref, *, norm, fd, groups):
    acc = jnp.dot(c_ref[...], pc_ref[...], preferred_element_type=_F32)
    acc = acc - jnp.dot(s_ref[...], ps_ref[...], preferred_element_type=_F32)
    fr = (acc * norm).astype(_BF)
    for g in range(groups):
        o_ref[:, g * fd:(g + 1) * fd] = jnp.dot(fr[:, g * fd:(g + 1) * fd], wf_ref[g],
                                                preferred_element_type=_F32).astype(_BF)


def _dft_positions(pc, ps, cmat, smat, w_f, n_batch, n, row0, fd):
    tm = min(512, n)
    gb = 2
    tn = gb * fd
    blk0 = row0 // n
    return pl.pallas_call(
        functools.partial(_dft_pos_kernel, norm=1.0 / math.sqrt(n * fd), fd=fd, groups=gb),
        grid=(n_batch, F_GROUPS // gb, n // tm),
        in_specs=[pl.BlockSpec((tm, n), lambda b, j, m: (m, 0)),
                  pl.BlockSpec((tm, n), lambda b, j, m: (m, 0)),
                  pl.BlockSpec((n, tn), lambda b, j, m: (blk0 + b, j)),
                  pl.BlockSpec((n, tn), lambda b, j, m: (blk0 + b, j)),
                  pl.BlockSpec((gb, fd, fd), lambda b, j, m: (j, 0, 0))],
        out_specs=pl.BlockSpec((tm, tn), lambda b, j, m: (b * (n // tm) + m, j)),
        out_shape=jax.ShapeDtypeStruct((n_batch * n, F_GROUPS * fd), _BF),
        compiler_params=_params(("parallel", "parallel", "arbitrary"), 48),
        name=f"dft_positions_{n}",
    )(cmat, smat, pc, ps, w_f)


def _chunk_kernel(u_ref, v_ref, ws_ref, bs_ref, o_ref, *, heads, cdim, n_chunks):
    u = jax.nn.gelu(u_ref[...].astype(_F32))
    v = jax.nn.gelu(v_ref[...].astype(_F32))
    for h in range(heads):
        vh = v[:, h * cdim:(h + 1) * cdim]
        mu = jnp.mean(vh, axis=-1, keepdims=True)
        d = vh - mu
        var = jnp.mean(d * d, axis=-1, keepdims=True)
        vs = (d * lax.rsqrt(var + EPS)).astype(_BF)
        w = ws_ref[h]
        bias = bs_ref[h]
        for c in range(n_chunks):
            rows = slice(c * CHUNK, (c + 1) * CHUNK)
            sv = jnp.dot(w, vs[rows], preferred_element_type=_F32) + bias
            o_ref[rows, h * cdim:(h + 1) * cdim] = (u[rows, h * cdim:(h + 1) * cdim] * sv).astype(_BF)


def _chunk_mix(y_in, w_s, b_s, u_off, width):
    T = y_in.shape[0]
    tm = _tile(T, (512, 256))
    cdim = width // C_HEADS
    return pl.pallas_call(
        functools.partial(_chunk_kernel, heads=C_HEADS, cdim=cdim, n_chunks=tm // CHUNK),
        grid=(T // tm,),
        in_specs=[pl.BlockSpec((tm, width), lambda i: (i, u_off // width)),
                  pl.BlockSpec((tm, width), lambda i: (i, u_off // width + 1)),
                  pl.BlockSpec((C_HEADS, CHUNK, CHUNK), lambda i: (0, 0, 0)),
                  pl.BlockSpec((C_HEADS, CHUNK, 1), lambda i: (0, 0, 0))],
        out_specs=pl.BlockSpec((tm, width), lambda i: (i, 0)),
        out_shape=jax.ShapeDtypeStruct((T, width), _BF),
        compiler_params=_params(("parallel",), 32),
        name="chunk_mix",
    )(y_in, y_in, w_s, b_s.reshape(C_HEADS, CHUNK, 1))


_R_GIDX = N_GROUPS


def _route(lg):
    lane = lax.broadcasted_iota(jnp.int32, lg.shape, 1)

    def col(j):
        return jnp.sum(jnp.where(lane == j, lg, 0.0), axis=-1, keepdims=True)

    g = [col(j) for j in range(N_GROUPS)]
    gmax = functools.reduce(jnp.maximum, g)
    gi = jnp.full(gmax.shape, N_GROUPS - 1, jnp.int32)
    for j in reversed(range(N_GROUPS - 1)):
        gi = jnp.where(g[j] == gmax, j, gi)
    gw = 1.0 / functools.reduce(lambda a, b: a + b, [jnp.exp(x - gmax) for x in g])

    e = []
    for j in range(EXP_PER_GROUP):
        ej = col(N_GROUPS + j)
        for gg in range(1, N_GROUPS):
            ej = jnp.where(gi == gg, col(N_GROUPS + gg * EXP_PER_GROUP + j), ej)
        e.append(ej)
    v1 = functools.reduce(jnp.maximum, e)
    i1 = jnp.full(v1.shape, EXP_PER_GROUP - 1, jnp.int32)
    for j in reversed(range(EXP_PER_GROUP - 1)):
        i1 = jnp.where(e[j] == v1, j, i1)
    e2 = [jnp.where(i1 == j, -jnp.inf, e[j]) for j in range(EXP_PER_GROUP)]
    v2 = functools.reduce(jnp.maximum, e2)
    i2 = jnp.full(v2.shape, EXP_PER_GROUP - 1, jnp.int32)
    for j in reversed(range(EXP_PER_GROUP - 1)):
        i2 = jnp.where(e2[j] == v2, j, i2)
    t = jnp.exp(v2 - v1)
    w1 = (1.0 / (1.0 + t)) * gw
    w2 = (t / (1.0 + t)) * gw

    rec = jnp.where(lane == _R_GIDX, gi.astype(_F32), 0.0)
    for j in range(EXP_PER_GROUP):
        cw = jnp.where(i1 == j, w1, 0.0) + jnp.where(i2 == j, w2, 0.0)
        rec = rec + jnp.where(lane == j, cw, 0.0)
    return rec


def _route_plan(rec, tm):
    T = rec.shape[0]
    n_tiles = T // tm + N_GROUPS
    gidx = rec[:, _R_GIDX].astype(jnp.int32)
    onehot = (gidx[:, None] == jnp.arange(N_GROUPS, dtype=jnp.int32)[None, :]).astype(jnp.int32)
    rank = jnp.sum((jnp.cumsum(onehot, axis=0) - onehot) * onehot, axis=1)
    counts = jnp.sum(onehot, axis=0)
    tiles_g = (counts + tm - 1) // tm
    tile_end = jnp.cumsum(tiles_g)
    start = (tile_end - tiles_g) * tm
    dest = (jnp.sum(onehot * start[None, :], axis=1) + rank).astype(jnp.int32)
    src = jnp.zeros(((n_tiles + 1) * tm,), jnp.int32).at[dest].set(jnp.arange(T, dtype=jnp.int32))
    tile_id = jnp.arange(n_tiles, dtype=jnp.int32)
    tile_group = jnp.minimum(jnp.sum((tile_id[:, None] >= tile_end[None, :]).astype(jnp.int32), axis=1), N_GROUPS - 1)
    tile_used = (tile_id < tile_end[-1]).astype(jnp.int32)
    return dest, src, tile_group.astype(jnp.int32), tile_used


def _moe_kernel(src_ref, tg_ref, tv_ref, hp_hbm, wg_ref, wu_ref, wd_ref, o_ref, gbuf, sem, x_scr, rec_scr, hid_scr, *, tm, half):
    j, s = pl.program_id(0), pl.program_id(1)
    last_j, last_s = pl.num_programs(0) - 1, pl.num_programs(1) - 1
    used = tv_ref[j] == 1
    fetched = (j == 0) | (tv_ref[jnp.maximum(j - 1, 0)] == 1)
    quarter = tm // EXP_PER_GROUP
    F = wg_ref.shape[3]
    split = (F // 256) * 256
    tail = F - split

    @pl.when((s == 0) & fetched)
    def _():
        @pl.when(j == 0)
        def _():
            _gather_tile(hp_hbm, src_ref, j, tm, gbuf, sem, wait=False)

        _gather_tile(hp_hbm, src_ref, j, tm, gbuf, sem, wait=True)
        rows = gbuf[j % 2]
        lo, hi = _unpack_halves(rows[:, :half])
        x_scr[:, :half] = lo
        x_scr[:, half:] = hi
        rec_scr[...] = pltpu.bitcast(rows[:, half:], _F32)

    @pl.when(used & (s < EXP_PER_GROUP))
    def _():
        _gather_tile(hp_hbm, src_ref, j + 1, tm, gbuf, sem, wait=False, rows=(s * quarter, quarter), inline=True)
        x = x_scr[...]
        if split and 2 * tail == 256:
            w_tail = jnp.concatenate([wg_ref[0, 0, :, split:], wu_ref[0, 0, :, split:]], axis=1)
            t = jnp.dot(x, w_tail, preferred_element_type=_F32)
            gate = jnp.concatenate([jnp.dot(x, wg_ref[0, 0, :, :split], preferred_element_type=_F32), t[:, :tail]], axis=1)
            up = jnp.concatenate([jnp.dot(x, wu_ref[0, 0, :, :split], preferred_element_type=_F32), t[:, tail:]], axis=1)
        else:
            gate = jnp.dot(x, wg_ref[0, 0], preferred_element_type=_F32)
            up = jnp.dot(x, wu_ref[0, 0], preferred_element_type=_F32)
        rec = rec_scr[...]
        cw = rec[:, 0:1]
        for k in range(1, EXP_PER_GROUP):
            cw = jnp.where(s == k, rec[:, k:k + 1], cw)
        hid_scr[s] = ((jax.nn.silu(gate) * up) * cw).astype(_BF)

    @pl.when(used & (s >= EXP_PER_GROUP))
    def _():
        hid = jnp.concatenate([hid_scr[e] for e in range(EXP_PER_GROUP)], axis=1)
        wd = wd_ref[0]
        y = jnp.dot(hid, wd.reshape(wd.shape[0] * wd.shape[1], wd.shape[2]), preferred_element_type=_F32)
        o_ref[...] = _pack_halves(y.astype(_BF))

    @pl.when(jnp.logical_not(used) & (s >= EXP_PER_GROUP))
    def _():
        o_ref[...] = jnp.zeros(o_ref.shape, o_ref.dtype)

    @pl.when(used & (j == last_j) & (s == last_s))
    def _():
        _gather_tile(hp_hbm, src_ref, j + 1, tm, gbuf, sem, wait=True)


def _moe(hp, src, tile_group, tile_used, w_gate, w_up, w_down, l, tm):
    T, width = hp.shape
    half = width - LANES
    D = 2 * half
    n_tiles = tile_group.shape[0]
    L, E, _, F = w_gate.shape
    tn = MOE_TN
    w_down_g = w_down.reshape(L * N_GROUPS, EXP_PER_GROUP, F, D)
    assert src.shape[0] == (n_tiles + 1) * tm

    def expert(j, s, src, tg, tv):
        return tg[j] * EXP_PER_GROUP + jnp.minimum(s, EXP_PER_GROUP - 1)

    def col(j, s, src, tg, tv):
        return jnp.maximum(s - EXP_PER_GROUP, 0)

    return pl.pallas_call(
        functools.partial(_moe_kernel, tm=tm, half=half),
        grid_spec=pltpu.PrefetchScalarGridSpec(
            num_scalar_prefetch=3,
            grid=(n_tiles, EXP_PER_GROUP + D // tn),
            in_specs=[pl.BlockSpec(memory_space=pl.ANY),
                      pl.BlockSpec((1, 1, D, F), lambda j, s, *p: (l, expert(j, s, *p), 0, 0)),
                      pl.BlockSpec((1, 1, D, F), lambda j, s, *p: (l, expert(j, s, *p), 0, 0)),
                      pl.BlockSpec((1, EXP_PER_GROUP, F, tn), lambda j, s, *p: (l * N_GROUPS + p[1][j], 0, 0, col(j, s, *p)))],
            out_specs=pl.BlockSpec((tm, tn // 2), lambda j, s, *p: (j, col(j, s, *p))),
            scratch_shapes=[pltpu.VMEM((2, tm, width), jnp.uint32), pltpu.SemaphoreType.DMA((2,)),
                            pltpu.VMEM((tm, D), _BF), pltpu.VMEM((tm, LANES), _F32),
                            pltpu.VMEM((EXP_PER_GROUP, tm, F), _BF)]),
        out_shape=jax.ShapeDtypeStruct((n_tiles * tm, D // 2), jnp.uint32),
        compiler_params=_params(("arbitrary", "arbitrary"), 48),
        name="moe",
    )(src, tile_group, tile_used, hp, w_gate, w_up, w_down_g)


def _rot_cols(w):
    q = HEAD_ROPE // 4
    a, b, c, d = w[..., :q], w[..., q:2 * q], w[..., 2 * q:3 * q], w[..., 3 * q:]
    return jnp.concatenate([-b, a, -d, c], axis=-1)


_WROWS = 64


def _w_in_rows_kernel(w_ref, o_ref, *, kr_blk, n_src):
    r = pl.program_id(0)

    @pl.when(r < n_src)
    def _():
        o_ref[...] = w_ref[...].astype(_BF)

    @pl.when(r == n_src)
    def _():
        q = HEAD_ROPE // 4
        w = w_ref[...]
        o_ref[:, 0 * q:1 * q, :] = (-w[:, 1 * q:2 * q, :]).astype(_BF)
        o_ref[:, 1 * q:2 * q, :] = w[:, 0 * q:1 * q, :].astype(_BF)
        o_ref[:, 2 * q:3 * q, :] = (-w[:, 3 * q:4 * q, :]).astype(_BF)
        o_ref[:, 3 * q:4 * q, :] = w[:, 2 * q:3 * q, :].astype(_BF)

    @pl.when(r > n_src)
    def _():
        o_ref[...] = jnp.zeros(o_ref.shape, o_ref.dtype)


def _prep_w_in(w_in, D):
    L, _, n_in = w_in.shape
    wt = jnp.swapaxes(w_in, 1, 2)
    q, kv = D // 4, D // 8
    o1, o2, o3 = q, q + kv, q + kv + HEAD_ROPE
    assert HEAD_ROPE == _WROWS and o1 % _WROWS == 0 and o2 % _WROWS == 0 and n_in % _WROWS == 0
    b1, b2, b3, nb = o1 // _WROWS, o2 // _WROWS, o3 // _WROWS, n_in // _WROWS
    n_rest, n_kv = nb - b3, b2 - b1
    n_src = b1 + n_rest + n_kv + 1
    total = n_in + HEAD_ROPE
    total += (-total) % 512

    def src_block(r, l):
        blk = jnp.where(r < b1, r, jnp.where(r < b1 + n_rest, r - b1 + b3, jnp.where(r < b1 + n_rest + n_kv, r - b1 - n_rest + b1, b2)))
        return (0, jnp.minimum(blk, nb - 1), 0)

    return pl.pallas_call(
        functools.partial(_w_in_rows_kernel, kr_blk=b2, n_src=n_src),
        grid=(total // _WROWS,),
        in_specs=[pl.BlockSpec((L, _WROWS, D), lambda r: src_block(r, 0))],
        out_specs=pl.BlockSpec((L, _WROWS, D), lambda r: (0, r, 0)),
        out_shape=jax.ShapeDtypeStruct((L, total, D), _BF),
        compiler_params=_params(("parallel",), 32),
        name="w_in_rows",
    )(wt)


def _prep_w_uq(w_uq, n_heads):
    L, r, _ = w_uq.shape
    w = w_uq.reshape(L, r, n_heads, HEAD_V + HEAD_ROPE)
    rope = w[..., HEAD_V:]
    return jnp.concatenate([w[..., :HEAD_V], rope, _rot_cols(rope)], axis=-1).reshape(L, r, n_heads * HEAD_QK).astype(_BF)


def _prep_w_ukv(w_ukv, n_heads):
    L, r, _ = w_ukv.shape
    w = w_ukv.reshape(L, r, n_heads, 2 * HEAD_V)
    return jnp.concatenate([w[..., :HEAD_V].reshape(L, r, -1), w[..., HEAD_V:].reshape(L, r, -1)], axis=-1).astype(_BF)


def _rope_table(n_batch, seq, ctx):
    half = HEAD_ROPE // 2
    inv_freq = ROPE_THETA ** (-jnp.arange(0, half, 2, dtype=_F32) / half)
    pos = jnp.arange(seq)
    ang_r = (pos // GRID_W).astype(_F32)[:, None] * inv_freq
    ang_c = (pos % GRID_W).astype(_F32)[:, None] * inv_freq
    ang = jnp.concatenate([ang_r, ang_r, ang_c, ang_c], axis=-1)
    lat = jnp.concatenate([jnp.cos(ang), jnp.sin(ang)], axis=-1)
    ctx_rows = jnp.concatenate([jnp.ones((n_batch * ctx, HEAD_ROPE), _F32), jnp.zeros((n_batch * ctx, HEAD_ROPE), _F32)], axis=-1)
    return jnp.concatenate([jnp.tile(lat, (n_batch, 1)), ctx_rows], axis=0)


def kernel(x, c, ctx, c_ctx, w_ada, b_ada, w_in, q_norm_g, kv_norm_g, w_uq, w_ukv, w_fourier, w_spatial, b_spatial, w_out, ln1_g, ln1_b, w_router_group, b_router_group, w_router_expert, b_router_expert, w_gate, w_up, w_down, ln2_g, ln2_b):
    B, SEQ, D = x.shape
    CTX = ctx.shape[1]
    L = w_ada.shape[0]
    n_heads = (D // 2) // HEAD_V
    q_rank, kv_rank = D // 4, D // 8
    fd = D // 16
    alpha = (2.0 * L) ** 0.25
    n_seg = B + 1
    assert n_seg <= MOD_ROWS and SEQ % 256 == 0 and (B * CTX) % 256 == 0 and CTX == 256 and (B * SEQ) % CTX == 0
    moe_tm = 512

    f_off, u_off, ckv_off, kr_off = q_rank, 2 * q_rank, D, D + kv_rank

    w_in_p = _prep_w_in(w_in, D)
    w_uq_p = _prep_w_uq(w_uq, n_heads)
    w_ukv_p = _prep_w_ukv(w_ukv, n_heads)
    w_out_b = w_out.astype(_BF)
    w_f_b = w_fourier.astype(_BF)
    w_s_b = w_spatial.astype(_BF)
    w_gate_b = w_gate.astype(_BF)
    w_up_b = w_up.astype(_BF)
    w_down_b = w_down.astype(_BF)
    n_logits = N_GROUPS + N_GROUPS * EXP_PER_GROUP
    w_r = jnp.concatenate([w_router_group, w_router_expert, jnp.zeros((L, D, LANES - n_logits), _F32)], axis=-1).astype(_BF)
    b_r = jnp.concatenate([b_router_group, b_router_expert, jnp.zeros((L, LANES - n_logits), _F32)], axis=-1).reshape(L, 1, LANES)
    cs_tab = _rope_table(B, SEQ, CTX)
    c_lat, s_lat = _dft_mats(SEQ)
    c_ctx_m, s_ctx_m = _dft_mats(CTX)
    c_ch, s_ch = _dft_mats(fd)
    cs_ch = jnp.concatenate([c_ch, s_ch], axis=1)

    c_rows = jnp.concatenate([c, c_ctx[None, :], jnp.zeros((MOD_ROWS - n_seg, D), _F32)], axis=0)
    mod_all = _adaln(c_rows, w_ada, b_ada)

    xt = (x.reshape(B * SEQ, D), ctx.reshape(B * CTX, D))
    mod3 = [mod_all[l].reshape(MOD_ROWS * 6, 1, D) for l in range(L)]
    h = _modulate(xt, mod3[0], 1, 0, SEQ, B)

    for l in range(L):
        last = l == L - 1
        y_in = _matmul(h, w_in_p, l, _BF, "in_proj")
        q = _q_proj(y_in, q_norm_g[l], w_uq_p, l, cs_tab, q_rank, math.log2(math.e) / math.sqrt(HEAD_V + HEAD_ROPE))
        k, v1 = _kv_proj(y_in, kv_norm_g[l], w_ukv_p, l, cs_tab, kv_rank, ckv_off, kr_off, n_heads)
        att = _attention(q, k, v1, B, SEQ, CTX, n_heads)
        pc, ps = _dft_channels(y_in, cs_ch, f_off, fd)
        fmix = (_dft_positions(pc, ps, c_lat, s_lat, w_f_b[l], B, SEQ, 0, fd),
                _dft_positions(pc, ps, c_ctx_m, s_ctx_m, w_f_b[l], B, CTX, B * SEQ, fd))
        cmix = _chunk_mix(y_in, w_s_b[l], b_spatial[l], u_off, q_rank)
        mixed = _out_proj(att, fmix, cmix, w_out_b, l)
        x1, h2p, rec = _resid_ln_route(xt, mixed, mod3[l], 2, ln1_g[l], ln1_b[l], 4, 3, w_r, b_r, l, SEQ, B, alpha)
        dest, src, tile_group, tile_used = _route_plan(rec, moe_tm)
        ys = _moe(h2p, src, tile_group, tile_used, w_gate_b, w_up_b, w_down_b, l, moe_tm)
        x2, h = _resid_ln_gather(x1, ys, dest, mod3[l], 5, ln2_g[l], ln2_b[l], 1, 0, SEQ, B, alpha,
                                 emit_h=not last, mod_next=mod3[min(l + 1, L - 1)],
                                 n_rows=B * SEQ if last else B * (SEQ + CTX))
        xt = (x2,)

    return xt[0].reshape(B, SEQ, D)
```
